```python
import math
import jax, jax.numpy as jnp
from jax import lax
import numpy as np

D_MODEL = 1024
BATCH = 8
SEQ = 2048
DEPTH = 2
DEC_BATCH = 128
DEC_SEQ = 4
PAST_LEN = 16384
PAGE_SIZE = 128

N_MIXERS = 2
GLA_HEADS = 4
GLA_DK = D_MODEL // 2
GLA_DV = D_MODEL
GLA_DK_HEAD = GLA_DK // GLA_HEADS
GLA_DV_HEAD = GLA_DV // GLA_HEADS
GLA_GATE_RANK = 16
GLA_TAU = 16.0
GLA_CHUNK = 64
GLA_IN = 2 * GLA_DK + 2 * GLA_DV + GLA_GATE_RANK
CONV_WIDTH = 3
D_FF = 4 * D_MODEL
N_GLA = (DEPTH + 1) // 2
N_CONV = DEPTH // 2
ALPHA = (2 * DEPTH) ** 0.25
BETA = (8 * DEPTH) ** -0.25
LN_EPS = 1e-5
RMS_EPS = 1e-6

kernel_name = "gla_shortconv_hybrid_step"


def layer_norm(x, g, b):
    xf = x.astype(jnp.float32)
    mu = jnp.mean(xf, axis=-1, keepdims=True)
    var = jnp.mean(jnp.square(xf - mu), axis=-1, keepdims=True)
    return ((xf - mu) * lax.rsqrt(var + LN_EPS) * g.astype(jnp.float32) + b.astype(jnp.float32)).astype(x.dtype)


def gla_core(q, k, v, g, s0):
    bsz, t, nh, dk = q.shape
    dv = v.shape[-1]
    c = GLA_CHUNK if t % GLA_CHUNK == 0 else t
    n = t // c

    def blk(a):
        return a.reshape(bsz, n, c, nh, a.shape[-1]).transpose(0, 3, 1, 2, 4)

    q, k, v, g = blk(q), blk(k), blk(v), blk(g)
    b = jnp.cumsum(g, axis=3)
    b_last = b[..., -1:, :]
    q_dec = q * jnp.exp(b)
    k_dec = k * jnp.exp(-b)
    causal = jnp.tril(jnp.ones((c, c), dtype=bool))
    scores = jnp.where(causal, jnp.einsum('bhnid,bhnjd->bhnij', q_dec, k_dec), 0.0)
    o_intra = jnp.einsum('bhnij,bhnjv->bhniv', scores, v)
    kv = jnp.einsum('bhncd,bhncv->nbhdv', k * jnp.exp(b_last - b), v)
    decay = jnp.exp(b_last[..., 0, :]).transpose(2, 0, 1, 3)

    def step(s, inp):
        dec, kv_n = inp
        return dec[..., None] * s + kv_n, s

    s_final, s_prev = lax.scan(step, s0, (decay, kv))
    o_inter = jnp.einsum('bhncd,nbhdv->bhncv', q_dec, s_prev)
    o = (o_intra + o_inter).transpose(0, 2, 3, 1, 4).reshape(bsz, t, nh, dv)
    return o, s_final


def gla_mixer(x, s0, w_in, w_gate_up, b_gate, norm_g, w_o):
    bsz, t, _ = x.shape
    proj = x @ w_in
    q, k, v, r, gl = jnp.split(proj, [GLA_DK, 2 * GLA_DK, 2 * GLA_DK + GLA_DV, 2 * GLA_DK + 2 * GLA_DV], axis=-1)
    log_a = jax.nn.log_sigmoid((gl @ w_gate_up + b_gate).astype(jnp.float32)) / GLA_TAU
    q = q.astype(jnp.float32).reshape(bsz, t, GLA_HEADS, GLA_DK_HEAD) * (GLA_DK_HEAD ** -0.5)
    k = k.astype(jnp.float32).reshape(bsz, t, GLA_HEADS, GLA_DK_HEAD)
    v = v.astype(jnp.float32).reshape(bsz, t, GLA_HEADS, GLA_DV_HEAD)
    g = log_a.reshape(bsz, t, GLA_HEADS, GLA_DK_HEAD)
    o, s = gla_core(q, k, v, g, s0.astype(jnp.float32))
    o = o * lax.rsqrt(jnp.mean(jnp.square(o), axis=-1, keepdims=True) + RMS_EPS) * norm_g.astype(jnp.float32)
    o = o.reshape(bsz, t, GLA_DV).astype(x.dtype) * jax.nn.silu(r)
    return o @ w_o, s.astype(s0.dtype)


def conv_mixer(x, buf, w_in, w_conv, w_out):
    t = x.shape[1]
    bg, cg, h = jnp.split(x @ w_in, 3, axis=-1)
    u = cg * h
    full = jnp.concatenate([buf.astype(u.dtype), u], axis=1)
    conv = full[:, 0:t, :] * w_conv[0]
    for i in range(1, CONV_WIDTH):
        conv = conv + full[:, i:i + t, :] * w_conv[i]
    return (bg * conv) @ w_out, full[:, -(CONV_WIDTH - 1):, :].astype(buf.dtype)


def mlp(x, w_up, w_down):
    return jnp.square(jax.nn.relu(x @ w_up)) @ w_down


def trunk(x, gla_states, conv_bufs, gla_w_in, gla_w_gate_up, gla_b_gate, gla_norm_g, gla_w_o,
          conv_w_in, conv_w_conv, conv_w_out, mlp_w_up, mlp_w_down, ln1_g, ln1_b, ln2_g, ln2_b):
    new_gla, new_conv = [], []
    for i in range(DEPTH):
        j = i // N_MIXERS
        if i % N_MIXERS == 0:
            h, s = gla_mixer(x, gla_states[j], gla_w_in[j], gla_w_gate_up[j], gla_b_gate[j], gla_norm_g[j], gla_w_o[j])
            new_gla.append(s)
        else:
            h, s = conv_mixer(x, conv_bufs[j], conv_w_in[j], conv_w_conv[j], conv_w_out[j])
            new_conv.append(s)
        x = layer_norm(ALPHA * x + h, ln1_g[i], ln1_b[i])
        x = layer_norm(ALPHA * x + mlp(x, mlp_w_up[i], mlp_w_down[i]), ln2_g[i], ln2_b[i])
    return x, jnp.stack(new_gla), jnp.stack(new_conv)


def setup_inputs(seed: int = 0) -> dict:
    key = jax.random.key(seed)
    ks = jax.random.split(key, 20)
    nrm = jax.random.normal
    f32 = jnp.float32
    return {
        "x_prompt": nrm(ks[0], (BATCH, SEQ, D_MODEL), f32),
        "x_sample": nrm(ks[1], (DEC_BATCH, DEC_SEQ, D_MODEL), f32),
        "state_gla": 0.5 * nrm(ks[2], (N_GLA, DEC_BATCH, GLA_HEADS, GLA_DK_HEAD, GLA_DV_HEAD), f32),
        "state_conv": nrm(ks[3], (N_CONV, DEC_BATCH, CONV_WIDTH - 1, D_MODEL), f32),
        "gla_w_in": nrm(ks[4], (N_GLA, D_MODEL, GLA_IN), f32) * D_MODEL ** -0.5,
        "gla_w_gate_up": nrm(ks[5], (N_GLA, GLA_GATE_RANK, GLA_DK), f32) * GLA_GATE_RANK ** -0.5,
        "gla_b_gate": 0.1 * nrm(ks[6], (N_GLA, GLA_DK), f32),
        "gla_norm_g": 1.0 + 0.01 * nrm(ks[7], (N_GLA, GLA_HEADS, GLA_DV_HEAD), f32),
        "gla_w_o": nrm(ks[8], (N_GLA, GLA_DV, D_MODEL), f32) * (GLA_DV ** -0.5 * BETA),
        "conv_w_in": nrm(ks[9], (N_CONV, D_MODEL, 3 * D_MODEL), f32) * D_MODEL ** -0.5,
        "conv_w_conv": nrm(ks[10], (N_CONV, CONV_WIDTH, D_MODEL), f32) * CONV_WIDTH ** -0.5,
        "conv_w_out": nrm(ks[11], (N_CONV, D_MODEL, D_MODEL), f32) * (D_MODEL ** -0.5 * BETA),
        "mlp_w_up": nrm(ks[12], (DEPTH, D_MODEL, D_FF), f32) * D_MODEL ** -0.5,
        "mlp_w_down": nrm(ks[13], (DEPTH, D_FF, D_MODEL), f32) * (D_FF ** -0.5 * BETA),
        "ln1_g": 1.0 + 0.01 * nrm(ks[14], (DEPTH, D_MODEL), f32),
        "ln1_b": 0.01 * nrm(ks[15], (DEPTH, D_MODEL), f32),
        "ln2_g": 1.0 + 0.01 * nrm(ks[16], (DEPTH, D_MODEL), f32),
        "ln2_b": 0.01 * nrm(ks[17], (DEPTH, D_MODEL), f32),
    }


def reference(x_prompt, x_sample, state_gla, state_conv, gla_w_in, gla_w_gate_up, gla_b_gate, gla_norm_g,
              gla_w_o, conv_w_in, conv_w_conv, conv_w_out, mlp_w_up, mlp_w_down, ln1_g, ln1_b, ln2_g, ln2_b):
    weights = (gla_w_in, gla_w_gate_up, gla_b_gate, gla_norm_g, gla_w_o, conv_w_in, conv_w_conv, conv_w_out,
               mlp_w_up, mlp_w_down, ln1_g, ln1_b, ln2_g, ln2_b)
    gla0 = jnp.zeros((N_GLA, BATCH, GLA_HEADS, GLA_DK_HEAD, GLA_DV_HEAD), state_gla.dtype)
    conv0 = jnp.zeros((N_CONV, BATCH, CONV_WIDTH - 1, D_MODEL), state_conv.dtype)
    y_prompt, gla_p, conv_p = trunk(x_prompt, gla0, conv0, *weights)
    y_sample, gla_s, conv_s = trunk(x_sample, state_gla, state_conv, *weights)
    return (y_prompt, y_sample, gla_p, gla_s, conv_p, conv_s)
```

```python
import functools

import jax
import jax.numpy as jnp
from jax import lax
from jax.experimental import pallas as pl
from jax.experimental.pallas import tpu as pltpu

bf16 = jnp.bfloat16
f32 = jnp.float32

D = 1024
H = 4
DK = 512
DV = 1024
DKH = DK // H
DVH = DV // H
RANK = 16
RANK_PAD = 128
TAU = 16.0
CHUNK = 64
D_FF = 4 * D
DEPTH = 2
ALPHA = (2 * DEPTH) ** 0.25
LN_EPS = 1e-5
RMS_EPS = 1e-6
Q_SCALE = DKH ** -0.5

SUB = 8
VMEM_LIMIT = 56 * 1024 * 1024

_NT = (((1,), (1,)), ((), ()))
_TN = (((0,), (0,)), ((), ()))


def _dot(a, b):
    return jnp.dot(a, b, preferred_element_type=f32)


def _dg(a, b, dims):
    return lax.dot_general(a, b, dims, preferred_element_type=f32)


def _layer_norm(y, g, b):
    mu = jnp.mean(y, axis=-1, keepdims=True)
    yc = y - mu
    var = jnp.mean(yc * yc, axis=-1, keepdims=True)
    return yc * lax.rsqrt(var + LN_EPS) * g + b


def _log_sigmoid(z):
    return -(jnp.maximum(-z, 0.0) + jnp.log1p(jnp.exp(-jnp.abs(z))))


def _gla_project(xb, wq_ref, wgl_ref, wgu_ref, bgate_ref):
    q = _dot(xb, wq_ref[:, 0:DK]) * Q_SCALE
    k = _dot(xb, wq_ref[:, DK:2 * DK])
    v = _dot(xb, wq_ref[:, 2 * DK:2 * DK + DV])
    r = _dot(xb, wq_ref[:, 2 * DK + DV:2 * DK + 2 * DV])
    gl = _dot(xb, wgl_ref[...])
    z = _dot(gl.astype(bf16), wgu_ref[...]) + bgate_ref[...]
    g = _log_sigmoid(z) / TAU
    return q, k, v, r, g


def _gla_post(o, r, x, ng, wo_ref, lg, lb):
    parts = []
    for h in range(H):
        vs = slice(h * DVH, (h + 1) * DVH)
        oh = o[:, vs]
        ms = jnp.mean(oh * oh, axis=-1, keepdims=True)
        parts.append(oh * lax.rsqrt(ms + RMS_EPS) * ng[:, vs])
    on = jnp.concatenate(parts, axis=1)
    gated = on * (r * jax.nn.sigmoid(r))
    y = _dot(gated.astype(bf16), wo_ref[...])
    return _layer_norm(ALPHA * x + y, lg, lb)


def _gla_prompt_kernel(x_ref, wq_ref, wgl_ref, wgu_ref, bgate_ref, ng_ref, wo_ref, lg_ref, lb_ref,
                       y_ref, st_ref, s_scr, *, n_chunks):
    j = pl.program_id(1)

    @pl.when(j == 0)
    def _():
        s_scr[...] = jnp.zeros_like(s_scr)

    x = x_ref[0]
    q, k, v, r, g = _gla_project(x.astype(bf16), wq_ref, wgl_ref, wgu_ref, bgate_ref)

    row = lax.broadcasted_iota(jnp.int32, (CHUNK, CHUNK), 0)
    col = lax.broadcasted_iota(jnp.int32, (CHUNK, CHUNK), 1)
    causal = row >= col
    tri = causal.astype(f32)

    o_chunks = []
    for c in range(n_chunks):
        rs = slice(c * CHUNK, (c + 1) * CHUNK)
        b = jnp.dot(tri, g[rs], precision=lax.Precision.HIGHEST, preferred_element_type=f32)
        b_last = b[CHUNK - 1:CHUNK, :]
        qd = (q[rs] * jnp.exp(b)).astype(bf16)
        kd = (k[rs] * jnp.exp(-b)).astype(bf16)
        kk = (k[rs] * jnp.exp(b_last - b)).astype(bf16)
        vb = v[rs].astype(bf16)
        dec = jnp.exp(b_last)
        o_heads = []
        for h in range(H):
            ks = slice(h * DKH, (h + 1) * DKH)
            vs = slice(h * DVH, (h + 1) * DVH)
            sc = jnp.where(causal, _dg(qd[:, ks], kd[:, ks], _NT), 0.0).astype(bf16)
            st = s_scr[h]
            o_heads.append(_dot(sc, vb[:, vs]) + _dg(qd[:, ks], st.astype(bf16), _NT))
            s_scr[h] = dec[:, ks] * st + _dg(vb[:, vs], kk[:, ks], _TN)
        o_chunks.append(jnp.concatenate(o_heads, axis=1))
    o = jnp.concatenate(o_chunks, axis=0)

    y_ref[0] = _gla_post(o, r, x, ng_ref[...], wo_ref, lg_ref[...], lb_ref[...])

    @pl.when(j == pl.num_programs(1) - 1)
    def _():
        for h in range(H):
            st_ref[0, h] = s_scr[h].T


def _gla_sample_kernel(x_ref, st_ref, wq_ref, wgl_ref, wgu_ref, bgate_ref, ng_ref, wo_ref, lg_ref, lb_ref,
                       y_ref, sto_ref, *, n_seq, t_valid):
    rows = n_seq * SUB
    x = x_ref[...]
    q, k, v, r, g = _gla_project(x.astype(bf16), wq_ref, wgl_ref, wgu_ref, bgate_ref)

    row = lax.broadcasted_iota(jnp.int32, (rows, rows), 0)
    col = lax.broadcasted_iota(jnp.int32, (rows, rows), 1)
    same_seq = (row // SUB) == (col // SUB)
    tri = (same_seq & (row >= col)).astype(f32)
    tri_last = (same_seq & ((col % SUB) < t_valid)).astype(f32)
    b = jnp.dot(tri, g, precision=lax.Precision.HIGHEST, preferred_element_type=f32)
    b_last = jnp.dot(tri_last, g, precision=lax.Precision.HIGHEST, preferred_element_type=f32)

    valid = (lax.broadcasted_iota(jnp.int32, (rows, DK), 0) % SUB) < t_valid
    qd = (q * jnp.exp(b)).astype(bf16)
    kd = jnp.where(valid, k * jnp.exp(-b), 0.0).astype(bf16)
    kk = jnp.where(valid, k * jnp.exp(b_last - b), 0.0).astype(bf16)
    vb = v.astype(bf16)
    dec_t = jnp.exp(b_last.T)

    r8 = lax.broadcasted_iota(jnp.int32, (SUB, SUB), 0)
    c8 = lax.broadcasted_iota(jnp.int32, (SUB, SUB), 1)
    causal = r8 >= c8

    o_seqs = []
    for n in range(n_seq):
        rs = slice(n * SUB, (n + 1) * SUB)
        o_heads = []
        for h in range(H):
            ks = slice(h * DKH, (h + 1) * DKH)
            vs = slice(h * DVH, (h + 1) * DVH)
            s0 = st_ref[n, h]
            sc = jnp.where(causal, _dg(qd[rs, ks], kd[rs, ks], _NT), 0.0).astype(bf16)
            o_heads.append(_dot(sc, vb[rs, vs]) + _dot(qd[rs, ks], s0.astype(bf16)))
            dcol = dec_t[ks, n * SUB:n * SUB + 1]
            sto_ref[n, h] = dcol * s0 + _dg(kk[rs, ks], vb[rs, vs], _TN)
        o_seqs.append(jnp.concatenate(o_heads, axis=1))
    o = jnp.concatenate(o_seqs, axis=0)

    y_ref[...] = _gla_post(o, r, x, ng_ref[...], wo_ref, lg_ref[...], lb_ref[...])


def _conv_kernel(*refs, t_seq, carry):
    if carry:
        x_ref, win_ref, wc_ref, wout_ref, lg_ref, lb_ref, y_ref, st_ref, c_scr = refs
        x = x_ref[0]
    else:
        x_ref, e1_ref, e2_ref, win_ref, wc_ref, wout_ref, lg_ref, lb_ref, y_ref, u_ref = refs
        x = x_ref[...]
    rows = x.shape[0]
    xb = x.astype(bf16)
    bg = _dot(xb, win_ref[:, 0:D])
    cg = _dot(xb, win_ref[:, D:2 * D])
    hh = _dot(xb, win_ref[:, 2 * D:3 * D])
    u = cg * hh
    r1 = pltpu.roll(u, 1, 0)
    r2 = pltpu.roll(u, 2, 0)
    t = lax.broadcasted_iota(jnp.int32, (rows, D), 0)
    if carry:
        @pl.when(pl.program_id(1) == 0)
        def _():
            c_scr[...] = jnp.zeros_like(c_scr)
        c0 = c_scr[0:1, :]
        c1 = c_scr[1:2, :]
        p1 = jnp.where(t == 0, c1, r1)
        p2 = jnp.where(t == 0, c0, jnp.where(t == 1, c1, r2))
    else:
        t = t % t_seq
        p1 = jnp.where(t >= 1, r1, e1_ref[...])
        p2 = jnp.where(t >= 2, r2, e2_ref[...])
    conv = p2 * wc_ref[0:1, :] + p1 * wc_ref[1:2, :] + u * wc_ref[2:3, :]
    y = _dot((bg * conv).astype(bf16), wout_ref[...])
    out = _layer_norm(ALPHA * x + y, lg_ref[...], lb_ref[...])
    if carry:
        y_ref[0] = out
        last2 = u[rows - 2:rows, :]
        c_scr[0:2, :] = last2
        st_ref[0] = last2
    else:
        y_ref[...] = out
        u_ref[...] = u


def _mlp_kernel(x_ref, wup_ref, wdn_ref, lg_ref, lb_ref, y_ref, *, ff_chunk):
    x = x_ref[...]
    xb = x.astype(bf16)
    acc = ALPHA * x
    for c in range(D_FF // ff_chunk):
        cs = slice(c * ff_chunk, (c + 1) * ff_chunk)
        hcol = _dot(xb, wup_ref[:, cs])
        hcol = jnp.square(jnp.maximum(hcol, 0.0)).astype(bf16)
        acc = acc + _dot(hcol, wdn_ref[cs, :])
    y_ref[...] = _layer_norm(acc, lg_ref[...], lb_ref[...])


def _const_spec(shape):
    nd = len(shape)
    return pl.BlockSpec(shape, lambda *_: (0,) * nd, pipeline_mode=pl.Buffered(1))


def _mlp(x2d, wup, wdn, lg, lb, tm):
    m = x2d.shape[0]
    return pl.pallas_call(
        functools.partial(_mlp_kernel, ff_chunk=1024),
        grid=(m // tm,),
        in_specs=[pl.BlockSpec((tm, D), lambda i: (i, 0)),
                  _const_spec((D, D_FF)), _const_spec((D_FF, D)),
                  _const_spec((1, D)), _const_spec((1, D))],
        out_specs=pl.BlockSpec((tm, D), lambda i: (i, 0)),
        out_shape=jax.ShapeDtypeStruct((m, D), f32),
        compiler_params=pltpu.CompilerParams(dimension_semantics=("arbitrary",), vmem_limit_bytes=VMEM_LIMIT),
        name="mlp",
    )(x2d, wup, wdn, lg, lb)


def _gla_prompt(x, wq, wgl, wgu, bgate, ng, wo, lg, lb, tb):
    bsz, t, _ = x.shape
    n_chunks = tb // CHUNK
    return pl.pallas_call(
        functools.partial(_gla_prompt_kernel, n_chunks=n_chunks),
        grid=(bsz, t // tb),
        in_specs=[pl.BlockSpec((1, tb, D), lambda b, j: (b, j, 0)),
                  _const_spec(wq.shape), _const_spec(wgl.shape), _const_spec(wgu.shape),
                  _const_spec(bgate.shape), _const_spec(ng.shape), _const_spec(wo.shape),
                  _const_spec(lg.shape), _const_spec(lb.shape)],
        out_specs=[pl.BlockSpec((1, tb, D), lambda b, j: (b, j, 0)),
                   pl.BlockSpec((1, H, DKH, DVH), lambda b, j: (b, 0, 0, 0))],
        out_shape=[jax.ShapeDtypeStruct((bsz, t, D), f32),
                   jax.ShapeDtypeStruct((bsz, H, DKH, DVH), f32)],
        scratch_shapes=[pltpu.VMEM((H, DVH, DKH), f32)],
        compiler_params=pltpu.CompilerParams(dimension_semantics=("arbitrary", "arbitrary"),
                                             vmem_limit_bytes=VMEM_LIMIT),
        name="gla_prompt",
    )(x, wq, wgl, wgu, bgate, ng, wo, lg, lb)


def _gla_sample(x_pad, state, wq, wgl, wgu, bgate, ng, wo, lg, lb, n_seq, t_valid):
    s_total = state.shape[0]
    rows = n_seq * SUB
    return pl.pallas_call(
        functools.partial(_gla_sample_kernel, n_seq=n_seq, t_valid=t_valid),
        grid=(s_total // n_seq,),
        in_specs=[pl.BlockSpec((rows, D), lambda i: (i, 0)),
                  pl.BlockSpec((n_seq, H, DKH, DVH), lambda i: (i, 0, 0, 0)),
                  _const_spec(wq.shape), _const_spec(wgl.shape), _const_spec(wgu.shape),
                  _const_spec(bgate.shape), _const_spec(ng.shape), _const_spec(wo.shape),
                  _const_spec(lg.shape), _const_spec(lb.shape)],
        out_specs=[pl.BlockSpec((rows, D), lambda i: (i, 0)),
                   pl.BlockSpec((n_seq, H, DKH, DVH), lambda i: (i, 0, 0, 0))],
        out_shape=[jax.ShapeDtypeStruct((s_total * SUB, D), f32),
                   jax.ShapeDtypeStruct(state.shape, f32)],
        compiler_params=pltpu.CompilerParams(dimension_semantics=("arbitrary",), vmem_limit_bytes=VMEM_LIMIT),
        name="gla_sample",
    )(x_pad, state, wq, wgl, wgu, bgate, ng, wo, lg, lb)


def _conv_prompt(x, win, wc, wout, lg, lb, tb):
    bsz, t, _ = x.shape
    return pl.pallas_call(
        functools.partial(_conv_kernel, t_seq=t, carry=True),
        grid=(bsz, t // tb),
        in_specs=[pl.BlockSpec((1, tb, D), lambda b, j: (b, j, 0)),
                  _const_spec(win.shape), _const_spec(wc.shape), _const_spec(wout.shape),
                  _const_spec(lg.shape), _const_spec(lb.shape)],
        out_specs=[pl.BlockSpec((1, tb, D), lambda b, j: (b, j, 0)),
                   pl.BlockSpec((1, 2, D), lambda b, j: (b, 0, 0))],
        out_shape=[jax.ShapeDtypeStruct((bsz, t, D), f32),
                   jax.ShapeDtypeStruct((bsz, 2, D), f32)],
        scratch_shapes=[pltpu.VMEM((SUB, D), f32)],
        compiler_params=pltpu.CompilerParams(dimension_semantics=("arbitrary", "arbitrary"),
                                             vmem_limit_bytes=VMEM_LIMIT),
        name="conv_prompt",
    )(x, win, wc, wout, lg, lb)


def _conv_sample(x2d, e1, e2, win, wc, wout, lg, lb, t_seq):
    m = x2d.shape[0]
    full = lambda shape: pl.BlockSpec(shape, lambda i: (0,) * len(shape))
    return pl.pallas_call(
        functools.partial(_conv_kernel, t_seq=t_seq, carry=False),
        grid=(1,),
        in_specs=[full((m, D)), full((m, D)), full((m, D)),
                  _const_spec(win.shape), _const_spec(wc.shape), _const_spec(wout.shape),
                  _const_spec(lg.shape), _const_spec(lb.shape)],
        out_specs=[full((m, D)), full((m, D))],
        out_shape=[jax.ShapeDtypeStruct((m, D), f32), jax.ShapeDtypeStruct((m, D), f32)],
        compiler_params=pltpu.CompilerParams(dimension_semantics=("arbitrary",), vmem_limit_bytes=VMEM_LIMIT),
        name="conv_sample",
    )(x2d, e1, e2, win, wc, wout, lg, lb)


def kernel(x_prompt, x_sample, state_gla, state_conv, gla_w_in, gla_w_gate_up, gla_b_gate, gla_norm_g, gla_w_o, conv_w_in, conv_w_conv, conv_w_out, mlp_w_up, mlp_w_down, ln1_g, ln1_b, ln2_g, ln2_b):
    bsz, t, _ = x_prompt.shape
    n_dec, t_dec, _ = x_sample.shape
    assert t % CHUNK == 0 and t_dec <= SUB and t_dec >= 2

    w_in = gla_w_in[0]
    wq = w_in[:, :2 * DK + 2 * DV].astype(bf16)
    wgl = jnp.pad(w_in[:, 2 * DK + 2 * DV:], ((0, 0), (0, RANK_PAD - RANK))).astype(bf16)
    wgu = jnp.pad(gla_w_gate_up[0], ((0, RANK_PAD - RANK), (0, 0))).astype(bf16)
    bgate = gla_b_gate[0].reshape(1, DK)
    ng = gla_norm_g[0].reshape(1, DV)
    wo = gla_w_o[0].astype(bf16)
    cwin = conv_w_in[0].astype(bf16)
    cwc = conv_w_conv[0]
    cwout = conv_w_out[0].astype(bf16)
    wup = mlp_w_up.astype(bf16)
    wdn = mlp_w_down.astype(bf16)
    row = lambda a, i: a[i].reshape(1, D)

    xp, gla_p = _gla_prompt(x_prompt, wq, wgl, wgu, bgate, ng, wo, row(ln1_g, 0), row(ln1_b, 0), tb=256)
    xs_pad = jnp.pad(x_sample, ((0, 0), (0, SUB - t_dec), (0, 0))).reshape(n_dec * SUB, D)
    xs_pad, gla_s = _gla_sample(xs_pad, state_gla[0], wq, wgl, wgu, bgate, ng, wo, row(ln1_g, 0), row(ln1_b, 0),
                                n_seq=16, t_valid=t_dec)
    xs = xs_pad.reshape(n_dec, SUB, D)[:, :t_dec].reshape(n_dec * t_dec, D)
    xp = _mlp(xp.reshape(bsz * t, D), wup[0], wdn[0], row(ln2_g, 0), row(ln2_b, 0), tm=512)
    xs = _mlp(xs, wup[0], wdn[0], row(ln2_g, 0), row(ln2_b, 0), tm=n_dec * t_dec)

    xp, conv_p = _conv_prompt(xp.reshape(bsz, t, D), cwin, cwc, cwout, row(ln1_g, 1), row(ln1_b, 1), tb=512)
    buf = state_conv[0]
    e1 = jnp.pad(buf[:, 1:2], ((0, 0), (0, t_dec - 1), (0, 0))).reshape(n_dec * t_dec, D)
    e2 = jnp.pad(buf, ((0, 0), (0, t_dec - 2), (0, 0))).reshape(n_dec * t_dec, D)
    xs, u_s = _conv_sample(xs, e1, e2, cwin, cwc, cwout, row(ln1_g, 1), row(ln1_b, 1), t_seq=t_dec)
    conv_s = u_s.reshape(n_dec, t_dec, D)[:, t_dec - 2:]
    xp = _mlp(xp.reshape(bsz * t, D), wup[1], wdn[1], row(ln2_g, 1), row(ln2_b, 1), tm=512)
    xs = _mlp(xs, wup[1], wdn[1], row(ln2_g, 1), row(ln2_b, 1), tm=n_dec * t_dec)

    return (xp.reshape(bsz, t, D), xs.reshape(n_dec, t_dec, D), gla_p[None], gla_s[None],
            conv_p[None], conv_s[None])
```

```python
import functools

import jax
import jax.numpy as jnp
from jax import lax
from jax.experimental import pallas as pl
from jax.experimental.pallas import tpu as pltpu

bf16 = jnp.bfloat16
f32 = jnp.float32

D = 1024
H = 4
DK = 512
DV = 1024
DKH = DK // H
DVH = DV // H
RANK = 16
RANK_PAD = 128
TAU = 16.0
CHUNK = 64
D_FF = 4 * D
DEPTH = 2
ALPHA = (2 * DEPTH) ** 0.25
LN_EPS = 1e-5
RMS_EPS = 1e-6
Q_SCALE = DKH ** -0.5

SUB = 8
VMEM_LIMIT = 56 * 1024 * 1024

_NT = (((1,), (1,)), ((), ()))
_TN = (((0,), (0,)), ((), ()))


def _dot(a, b):
    return jnp.dot(a, b, preferred_element_type=f32)


def _dg(a, b, dims):
    return lax.dot_general(a, b, dims, preferred_element_type=f32)


def _layer_norm(y, g, b):
    mu = jnp.mean(y, axis=-1, keepdims=True)
    yc = y - mu
    var = jnp.mean(yc * yc, axis=-1, keepdims=True)
    return yc * lax.rsqrt(var + LN_EPS) * g + b


def _log_sigmoid(z):
    return -(jnp.maximum(-z, 0.0) + jnp.log1p(jnp.exp(-jnp.abs(z))))


def _gla_project(xb, wq_ref, wgl_ref, wgu_ref, bgate_ref):
    q = _dot(xb, wq_ref[:, 0:DK]) * Q_SCALE
    k = _dot(xb, wq_ref[:, DK:2 * DK])
    v = _dot(xb, wq_ref[:, 2 * DK:2 * DK + DV])
    r = _dot(xb, wq_ref[:, 2 * DK + DV:2 * DK + 2 * DV])
    gl = _dot(xb, wgl_ref[...])
    z = _dot(gl.astype(bf16), wgu_ref[...]) + bgate_ref[...]
    g = _log_sigmoid(z) / TAU
    return q, k, v, r, g


def _gla_post(o, r, x, ng, wo_ref, lg, lb):
    parts = []
    for h in range(H):
        vs = slice(h * DVH, (h + 1) * DVH)
        oh = o[:, vs]
        ms = jnp.mean(oh * oh, axis=-1, keepdims=True)
        parts.append(oh * lax.rsqrt(ms + RMS_EPS) * ng[:, vs])
    on = jnp.concatenate(parts, axis=1)
    gated = on * (r * jax.nn.sigmoid(r))
    y = _dot(gated.astype(bf16), wo_ref[...])
    return _layer_norm(ALPHA * x + y, lg, lb)


PROJ_TILE = 512


def _interleave(main_tasks, filler_tasks):
    n_main, n_fill = len(main_tasks), len(filler_tasks)
    done = 0
    for idx, task in enumerate(main_tasks):
        task()
        want = ((idx + 1) * n_fill) // n_main
        while done < want:
            filler_tasks[done]()
            done += 1


def _gla_stage1_tasks(x_ref, slot, w_refs, scr):
    wq_ref, wgl_ref, wgu_ref, bgate_ref = w_refs
    xb_scr, q_scr, k_scr, v_scr, r_scr, g_scr = scr[:6]

    def cast_x():
        xb_scr[slot] = x_ref[...].astype(bf16)

    def proj(dst, c0, w0, scale):
        def run():
            acc = _dot(xb_scr[slot], wq_ref[:, w0:w0 + PROJ_TILE])
            dst[slot, :, c0:c0 + PROJ_TILE] = acc * scale if scale is not None else acc
        return run

    def gate():
        gl = _dot(xb_scr[slot], wgl_ref[...])
        z = _dot(gl.astype(bf16), wgu_ref[...]) + bgate_ref[...]
        g_scr[slot] = _log_sigmoid(z) / TAU

    tiles = []
    for dst, w_base, width, scale in ((q_scr, 0, DK, Q_SCALE), (k_scr, DK, DK, None),
                                      (v_scr, 2 * DK, DV, None), (r_scr, 2 * DK + DV, DV, None)):
        for c0 in range(0, width, PROJ_TILE):
            tiles.append(proj(dst, c0, w_base + c0, scale))
    return cast_x, gate, tiles


def _split3(a):
    a1 = a.astype(bf16)
    r1 = a - a1.astype(f32)
    a2 = r1.astype(bf16)
    a3 = (r1 - a2.astype(f32)).astype(bf16)
    return a1, a2, a3


def _gla_stage2_tasks(xres_ref, y_ref, r0, slot, scr, s_scr, ng_ref, wo_ref, lg_ref, lb_ref, n_chunks):
    _, q_scr, k_scr, v_scr, r_scr, g_scr, b_scr, o_scr, gated_scr = scr
    tb = n_chunks * CHUNK

    def cumsum():
        row = lax.broadcasted_iota(jnp.int32, (tb, tb), 0)
        col = lax.broadcasted_iota(jnp.int32, (tb, tb), 1)
        tri = ((row >= col) & ((row // CHUNK) == (col // CHUNK))).astype(bf16)
        g1, g2, g3 = _split3(g_scr[slot])
        b_scr[...] = _dot(tri, g1) + _dot(tri, g2) + _dot(tri, g3)

    def head_chunk(c, h, cell):
        rs = slice(c * CHUNK, (c + 1) * CHUNK)
        ks = slice(h * DKH, (h + 1) * DKH)
        vs = slice(h * DVH, (h + 1) * DVH)

        def run():
            if h == 0:
                b = b_scr[rs, :]
                b_last = b[CHUNK - 1:CHUNK, :]
                qc = q_scr[slot, rs, :]
                kc = k_scr[slot, rs, :]
                cell["qd"] = (qc * jnp.exp(b)).astype(bf16)
                cell["kd"] = (kc * jnp.exp(-b)).astype(bf16)
                cell["kk"] = (kc * jnp.exp(b_last - b)).astype(bf16)
                cell["vb"] = v_scr[slot, rs, :].astype(bf16)
                cell["dec"] = jnp.exp(b_last)
            r64 = lax.broadcasted_iota(jnp.int32, (CHUNK, CHUNK), 0)
            c64 = lax.broadcasted_iota(jnp.int32, (CHUNK, CHUNK), 1)
            qd, kd, kk, vb, dec = cell["qd"], cell["kd"], cell["kk"], cell["vb"], cell["dec"]
            sc = jnp.where(r64 >= c64, _dg(qd[:, ks], kd[:, ks], _NT), 0.0).astype(bf16)
            st = s_scr[h]
            o_scr[rs, vs] = _dot(sc, vb[:, vs]) + _dg(qd[:, ks], st.astype(bf16), _NT)
            s_scr[h] = dec[:, ks] * st + _dg(vb[:, vs], kk[:, ks], _TN)
        return run

    def norm_gate(h):
        vs = slice(h * DVH, (h + 1) * DVH)

        def run():
            oh = o_scr[:, vs]
            rh = r_scr[slot, :, vs]
            ms = jnp.mean(oh * oh, axis=-1, keepdims=True)
            on = oh * lax.rsqrt(ms + RMS_EPS) * ng_ref[:, vs]
            gated_scr[:, vs] = (on * (rh * jax.nn.sigmoid(rh))).astype(bf16)
        return run

    def out_rows(m0, m1):
        def run():
            y = _dot(gated_scr[m0:m1, :], wo_ref[...])
            x = xres_ref[r0 + m0:r0 + m1, :]
            y_ref[r0 + m0:r0 + m1, :] = _layer_norm(ALPHA * x + y, lg_ref[...], lb_ref[...])
        return run

    tasks = [cumsum]
    for c in range(n_chunks):
        cell = {}
        for h in range(H):
            tasks.append(head_chunk(c, h, cell))
    tasks += [norm_gate(h) for h in range(H)]
    half = tb // 2
    tasks += [out_rows(0, half), out_rows(half, tb)]
    return tasks


def _gla_prompt_kernel(x0_ref, xa_ref, xb_ref, xres_ref, wq_ref, wgl_ref, wgu_ref, bgate_ref, ng_ref, wo_ref,
                       lg_ref, lb_ref, y_ref, st_ref, xb_scr, q_scr, k_scr, v_scr, r_scr, g_scr, b_scr, o_scr,
                       gated_scr, s_scr, *, n_chunks, steps_per_seq):
    i = pl.program_id(0)
    tb = n_chunks * CHUNK
    w_refs = (wq_ref, wgl_ref, wgu_ref, bgate_ref)
    scr = (xb_scr, q_scr, k_scr, v_scr, r_scr, g_scr, b_scr, o_scr, gated_scr)
    post = (ng_ref, wo_ref, lg_ref, lb_ref)

    @pl.when(i == 0)
    def _():
        cast_x, gate, tiles = _gla_stage1_tasks(x0_ref, 0, w_refs, scr)
        for task in [cast_x, gate] + tiles:
            task()

    @pl.when(i % steps_per_seq == 0)
    def _():
        s_scr[...] = jnp.zeros_like(s_scr)

    for r0, slot, x_next in ((0, 0, xa_ref), (tb, 1, xb_ref)):
        cast_x, gate, tiles = _gla_stage1_tasks(x_next, 1 - slot, w_refs, scr)
        stage2 = _gla_stage2_tasks(xres_ref, y_ref, r0, slot, scr, s_scr, *post, n_chunks)
        cast_x()
        _interleave([stage2[0], gate] + stage2[1:], tiles)

    @pl.when(i % steps_per_seq == steps_per_seq - 1)
    def _():
        for h in range(H):
            st_ref[0, h] = s_scr[h].T


def _gla_sample_kernel(x_ref, st_ref, wq_ref, wgl_ref, wgu_ref, bgate_ref, ng_ref, wo_ref, lg_ref, lb_ref,
                       y_ref, sto_ref, *, n_seq, t_valid):
    rows = n_seq * SUB
    x = x_ref[...]
    q, k, v, r, g = _gla_project(x.astype(bf16), wq_ref, wgl_ref, wgu_ref, bgate_ref)

    row = lax.broadcasted_iota(jnp.int32, (rows, rows), 0)
    col = lax.broadcasted_iota(jnp.int32, (rows, rows), 1)
    same_seq = (row // SUB) == (col // SUB)
    tri = (same_seq & (row >= col)).astype(f32)
    tri_last = (same_seq & ((col % SUB) < t_valid)).astype(f32)
    b = jnp.dot(tri, g, precision=lax.Precision.HIGHEST, preferred_element_type=f32)
    b_last = jnp.dot(tri_last, g, precision=lax.Precision.HIGHEST, preferred_element_type=f32)

    valid = (lax.broadcasted_iota(jnp.int32, (rows, DK), 0) % SUB) < t_valid
    qd = (q * jnp.exp(b)).astype(bf16)
    kd = jnp.where(valid, k * jnp.exp(-b), 0.0).astype(bf16)
    kk = jnp.where(valid, k * jnp.exp(b_last - b), 0.0).astype(bf16)
    vb = v.astype(bf16)
    dec_t = jnp.exp(b_last.T)

    r8 = lax.broadcasted_iota(jnp.int32, (SUB, SUB), 0)
    c8 = lax.broadcasted_iota(jnp.int32, (SUB, SUB), 1)
    causal = r8 >= c8

    o_seqs = []
    for n in range(n_seq):
        rs = slice(n * SUB, (n + 1) * SUB)
        o_heads = []
        for h in range(H):
            ks = slice(h * DKH, (h + 1) * DKH)
            vs = slice(h * DVH, (h + 1) * DVH)
            s0 = st_ref[n, h]
            sc = jnp.where(causal, _dg(qd[rs, ks], kd[rs, ks], _NT), 0.0).astype(bf16)
            o_heads.append(_dot(sc, vb[rs, vs]) + _dot(qd[rs, ks], s0.astype(bf16)))
            dcol = dec_t[ks, n * SUB:n * SUB + 1]
            sto_ref[n, h] = dcol * s0 + _dg(kk[rs, ks], vb[rs, vs], _TN)
        o_seqs.append(jnp.concatenate(o_heads, axis=1))
    o = jnp.concatenate(o_seqs, axis=0)

    y_ref[...] = _gla_post(o, r, x, ng_ref[...], wo_ref, lg_ref[...], lb_ref[...])


def _conv_kernel(*refs, t_seq, carry):
    if carry:
        x_ref, win_ref, wc_ref, wout_ref, lg_ref, lb_ref, y_ref, st_ref, c_scr = refs
        x = x_ref[0]
    else:
        x_ref, e1_ref, e2_ref, win_ref, wc_ref, wout_ref, lg_ref, lb_ref, y_ref, u_ref = refs
        x = x_ref[...]
    rows = x.shape[0]
    xb = x.astype(bf16)
    bg = _dot(xb, win_ref[:, 0:D])
    cg = _dot(xb, win_ref[:, D:2 * D])
    hh = _dot(xb, win_ref[:, 2 * D:3 * D])
    u = cg * hh
    r1 = pltpu.roll(u, 1, 0)
    r2 = pltpu.roll(u, 2, 0)
    t = lax.broadcasted_iota(jnp.int32, (rows, D), 0)
    if carry:
        @pl.when(pl.program_id(1) == 0)
        def _():
            c_scr[...] = jnp.zeros_like(c_scr)
        c0 = c_scr[0:1, :]
        c1 = c_scr[1:2, :]
        p1 = jnp.where(t == 0, c1, r1)
        p2 = jnp.where(t == 0, c0, jnp.where(t == 1, c1, r2))
    else:
        t = t % t_seq
        p1 = jnp.where(t >= 1, r1, e1_ref[...])
        p2 = jnp.where(t >= 2, r2, e2_ref[...])
    conv = p2 * wc_ref[0:1, :] + p1 * wc_ref[1:2, :] + u * wc_ref[2:3, :]
    y = _dot((bg * conv).astype(bf16), wout_ref[...])
    out = _layer_norm(ALPHA * x + y, lg_ref[...], lb_ref[...])
    if carry:
        y_ref[0] = out
        last2 = u[rows - 2:rows, :]
        c_scr[0:2, :] = last2
        st_ref[0] = last2
    else:
        y_ref[...] = out
        u_ref[...] = u


def _mlp_kernel(x_ref, wup_ref, wdn_ref, lg_ref, lb_ref, y_ref, *, ff_chunk):
    x = x_ref[...]
    xb = x.astype(bf16)
    acc = ALPHA * x
    for c in range(D_FF // ff_chunk):
        cs = slice(c * ff_chunk, (c + 1) * ff_chunk)
        hcol = _dot(xb, wup_ref[:, cs])
        hcol = jnp.square(jnp.maximum(hcol, 0.0)).astype(bf16)
        acc = acc + _dot(hcol, wdn_ref[cs, :])
    y_ref[...] = _layer_norm(acc, lg_ref[...], lb_ref[...])


def _const_spec(shape):
    nd = len(shape)
    return pl.BlockSpec(shape, lambda *_: (0,) * nd, pipeline_mode=pl.Buffered(1))


def _mlp(x2d, wup, wdn, lg, lb, tm):
    m = x2d.shape[0]
    return pl.pallas_call(
        functools.partial(_mlp_kernel, ff_chunk=1024),
        grid=(m // tm,),
        in_specs=[pl.BlockSpec((tm, D), lambda i: (i, 0)),
                  _const_spec((D, D_FF)), _const_spec((D_FF, D)),
                  _const_spec((1, D)), _const_spec((1, D))],
        out_specs=pl.BlockSpec((tm, D), lambda i: (i, 0)),
        out_shape=jax.ShapeDtypeStruct((m, D), f32),
        compiler_params=pltpu.CompilerParams(dimension_semantics=("arbitrary",), vmem_limit_bytes=VMEM_LIMIT),
        name="mlp",
    )(x2d, wup, wdn, lg, lb)


def _gla_prompt(x, wq, wgl, wgu, bgate, ng, wo, lg, lb, tb):
    bsz, t, _ = x.shape
    n_chunks = tb // CHUNK
    n_blocks = bsz * t // tb
    steps_per_seq = t // (2 * tb)
    x2d = x.reshape(bsz * t, D)
    y, st = pl.pallas_call(
        functools.partial(_gla_prompt_kernel, n_chunks=n_chunks, steps_per_seq=steps_per_seq),
        grid=(n_blocks // 2,),
        in_specs=[pl.BlockSpec((tb, D), lambda i: (0, 0), pipeline_mode=pl.Buffered(1)),
                  pl.BlockSpec((tb, D), lambda i: (2 * i + 1, 0)),
                  pl.BlockSpec((tb, D), lambda i: (jnp.minimum(2 * i + 2, n_blocks - 1), 0)),
                  pl.BlockSpec((2 * tb, D), lambda i: (i, 0)),
                  _const_spec(wq.shape), _const_spec(wgl.shape), _const_spec(wgu.shape),
                  _const_spec(bgate.shape), _const_spec(ng.shape), _const_spec(wo.shape),
                  _const_spec(lg.shape), _const_spec(lb.shape)],
        out_specs=[pl.BlockSpec((2 * tb, D), lambda i: (i, 0)),
                   pl.BlockSpec((1, H, DKH, DVH), lambda i: (i // steps_per_seq, 0, 0, 0))],
        out_shape=[jax.ShapeDtypeStruct((bsz * t, D), f32),
                   jax.ShapeDtypeStruct((bsz, H, DKH, DVH), f32)],
        scratch_shapes=[pltpu.VMEM((2, tb, D), bf16),
                        pltpu.VMEM((2, tb, DK), f32), pltpu.VMEM((2, tb, DK), f32),
                        pltpu.VMEM((2, tb, DV), f32), pltpu.VMEM((2, tb, DV), f32),
                        pltpu.VMEM((2, tb, DK), f32),
                        pltpu.VMEM((tb, DK), f32), pltpu.VMEM((tb, DV), f32), pltpu.VMEM((tb, DV), bf16),
                        pltpu.VMEM((H, DVH, DKH), f32)],
        compiler_params=pltpu.CompilerParams(dimension_semantics=("arbitrary",),
                                             vmem_limit_bytes=VMEM_LIMIT),
        name="gla_prompt",
    )(x2d, x2d, x2d, x2d, wq, wgl, wgu, bgate, ng, wo, lg, lb)
    return y.reshape(bsz, t, D), st


def _gla_sample(x_pad, state, wq, wgl, wgu, bgate, ng, wo, lg, lb, n_seq, t_valid):
    s_total = state.shape[0]
    rows = n_seq * SUB
    return pl.pallas_call(
        functools.partial(_gla_sample_kernel, n_seq=n_seq, t_valid=t_valid),
        grid=(s_total // n_seq,),
        in_specs=[pl.BlockSpec((rows, D), lambda i: (i, 0)),
                  pl.BlockSpec((n_seq, H, DKH, DVH), lambda i: (i, 0, 0, 0)),
                  _const_spec(wq.shape), _const_spec(wgl.shape), _const_spec(wgu.shape),
                  _const_spec(bgate.shape), _const_spec(ng.shape), _const_spec(wo.shape),
                  _const_spec(lg.shape), _const_spec(lb.shape)],
        out_specs=[pl.BlockSpec((rows, D), lambda i: (i, 0)),
                   pl.BlockSpec((n_seq, H, DKH, DVH), lambda i: (i, 0, 0, 0))],
        out_shape=[jax.ShapeDtypeStruct((s_total * SUB, D), f32),
                   jax.ShapeDtypeStruct(state.shape, f32)],
        compiler_params=pltpu.CompilerParams(dimension_semantics=("arbitrary",), vmem_limit_bytes=VMEM_LIMIT),
        name="gla_sample",
    )(x_pad, state, wq, wgl, wgu, bgate, ng, wo, lg, lb)


def _conv_prompt(x, win, wc, wout, lg, lb, tb):
    bsz, t, _ = x.shape
    return pl.pallas_call(
        functools.partial(_conv_kernel, t_seq=t, carry=True),
        grid=(bsz, t // tb),
        in_specs=[pl.BlockSpec((1, tb, D), lambda b, j: (b, j, 0)),
                  _const_spec(win.shape), _const_spec(wc.shape), _const_spec(wout.shape),
                  _const_spec(lg.shape), _const_spec(lb.shape)],
        out_specs=[pl.BlockSpec((1, tb, D), lambda b, j: (b, j, 0)),
                   pl.BlockSpec((1, 2, D), lambda b, j: (b, 0, 0))],
        out_shape=[jax.ShapeDtypeStruct((bsz, t, D), f32),
                   jax.ShapeDtypeStruct((bsz, 2, D), f32)],
        scratch_shapes=[pltpu.VMEM((SUB, D), f32)],
        compiler_params=pltpu.CompilerParams(dimension_semantics=("arbitrary", "arbitrary"),
                                             vmem_limit_bytes=VMEM_LIMIT),
        name="conv_prompt",
    )(x, win, wc, wout, lg, lb)


def _conv_sample(x2d, e1, e2, win, wc, wout, lg, lb, t_seq):
    m = x2d.shape[0]
    full = lambda shape: pl.BlockSpec(shape, lambda i: (0,) * len(shape))
    return pl.pallas_call(
        functools.partial(_conv_kernel, t_seq=t_seq, carry=False),
        grid=(1,),
        in_specs=[full((m, D)), full((m, D)), full((m, D)),
                  _const_spec(win.shape), _const_spec(wc.shape), _const_spec(wout.shape),
                  _const_spec(lg.shape), _const_spec(lb.shape)],
        out_specs=[full((m, D)), full((m, D))],
        out_shape=[jax.ShapeDtypeStruct((m, D), f32), jax.ShapeDtypeStruct((m, D), f32)],
        compiler_params=pltpu.CompilerParams(dimension_semantics=("arbitrary",), vmem_limit_bytes=VMEM_LIMIT),
        name="conv_sample",
    )(x2d, e1, e2, win, wc, wout, lg, lb)


def kernel(x_prompt, x_sample, state_gla, state_conv, gla_w_in, gla_w_gate_up, gla_b_gate, gla_norm_g, gla_w_o, conv_w_in, conv_w_conv, conv_w_out, mlp_w_up, mlp_w_down, ln1_g, ln1_b, ln2_g, ln2_b):
    bsz, t, _ = x_prompt.shape
    n_dec, t_dec, _ = x_sample.shape
    assert t % CHUNK == 0 and t_dec <= SUB and t_dec >= 2

    w_in = gla_w_in[0]
    wq = w_in[:, :2 * DK + 2 * DV].astype(bf16)
    wgl = jnp.pad(w_in[:, 2 * DK + 2 * DV:], ((0, 0), (0, RANK_PAD - RANK))).astype(bf16)
    wgu = jnp.pad(gla_w_gate_up[0], ((0, RANK_PAD - RANK), (0, 0))).astype(bf16)
    bgate = gla_b_gate[0].reshape(1, DK)
    ng = gla_norm_g[0].reshape(1, DV)
    wo = gla_w_o[0].astype(bf16)
    cwin = conv_w_in[0].astype(bf16)
    cwc = conv_w_conv[0]
    cwout = conv_w_out[0].astype(bf16)
    wup = mlp_w_up.astype(bf16)
    wdn = mlp_w_down.astype(bf16)
    row = lambda a, i: a[i].reshape(1, D)

    xp, gla_p = _gla_prompt(x_prompt, wq, wgl, wgu, bgate, ng, wo, row(ln1_g, 0), row(ln1_b, 0), tb=256)
    xs_pad = jnp.pad(x_sample, ((0, 0), (0, SUB - t_dec), (0, 0))).reshape(n_dec * SUB, D)
    xs_pad, gla_s = _gla_sample(xs_pad, state_gla[0], wq, wgl, wgu, bgate, ng, wo, row(ln1_g, 0), row(ln1_b, 0),
                                n_seq=16, t_valid=t_dec)
    xs = xs_pad.reshape(n_dec, SUB, D)[:, :t_dec].reshape(n_dec * t_dec, D)
    xp = _mlp(xp.reshape(bsz * t, D), wup[0], wdn[0], row(ln2_g, 0), row(ln2_b, 0), tm=512)
    xs = _mlp(xs, wup[0], wdn[0], row(ln2_g, 0), row(ln2_b, 0), tm=n_dec * t_dec)

    xp, conv_p = _conv_prompt(xp.reshape(bsz, t, D), cwin, cwc, cwout, row(ln1_g, 1), row(ln1_b, 1), tb=512)
    buf = state_conv[0]
    e1 = jnp.pad(buf[:, 1:2], ((0, 0), (0, t_dec - 1), (0, 0))).reshape(n_dec * t_dec, D)
    e2 = jnp.pad(buf, ((0, 0), (0, t_dec - 2), (0, 0))).reshape(n_dec * t_dec, D)
    xs, u_s = _conv_sample(xs, e1, e2, cwin, cwc, cwout, row(ln1_g, 1), row(ln1_b, 1), t_seq=t_dec)
    conv_s = u_s.reshape(n_dec, t_dec, D)[:, t_dec - 2:]
    xp = _mlp(xp.reshape(bsz * t, D), wup[1], wdn[1], row(ln2_g, 1), row(ln2_b, 1), tm=512)
    xs = _mlp(xs, wup[1], wdn[1], row(ln2_g, 1), row(ln2_b, 1), tm=n_dec * t_dec)

    return (xp.reshape(bsz, t, D), xs.reshape(n_dec, t_dec, D), gla_p[None], gla_s[None],
            conv_p[None], conv_s[None])
```

```python
import functools

import jax
import jax.numpy as jnp
from jax import lax
from jax.experimental import pallas as pl
from jax.experimental.pallas import tpu as pltpu

bf16 = jnp.bfloat16
f32 = jnp.float32

D = 1024
H = 4
DK = 512
DV = 1024
DKH = DK // H
DVH = DV // H
RANK = 16
RANK_PAD = 128
TAU = 16.0
CHUNK = 64
D_FF = 4 * D
DEPTH = 2
ALPHA = (2 * DEPTH) ** 0.25
LN_EPS = 1e-5
RMS_EPS = 1e-6
Q_SCALE = DKH ** -0.5

SUB = 8
VMEM_LIMIT = 56 * 1024 * 1024

_NT = (((1,), (1,)), ((), ()))
_TN = (((0,), (0,)), ((), ()))


def _dot(a, b):
    return jnp.dot(a, b, preferred_element_type=f32)


def _dg(a, b, dims):
    return lax.dot_general(a, b, dims, preferred_element_type=f32)


def _layer_norm(y, g, b):
    mu = jnp.mean(y, axis=-1, keepdims=True)
    yc = y - mu
    var = jnp.mean(yc * yc, axis=-1, keepdims=True)
    return yc * lax.rsqrt(var + LN_EPS) * g + b


def _log_sigmoid(z):
    return -(jnp.maximum(-z, 0.0) + jnp.log1p(jnp.exp(-jnp.abs(z))))


def _gla_project(xb, wq_ref, wgl_ref, wgu_ref, bgate_ref):
    q = _dot(xb, wq_ref[:, 0:DK]) * Q_SCALE
    k = _dot(xb, wq_ref[:, DK:2 * DK])
    v = _dot(xb, wq_ref[:, 2 * DK:2 * DK + DV])
    r = _dot(xb, wq_ref[:, 2 * DK + DV:2 * DK + 2 * DV])
    gl = _dot(xb, wgl_ref[...])
    z = _dot(gl.astype(bf16), wgu_ref[...]) + bgate_ref[...]
    g = _log_sigmoid(z) / TAU
    return q, k, v, r, g


def _gla_post(o, r, x, ng, wo_ref, lg, lb):
    parts = []
    for h in range(H):
        vs = slice(h * DVH, (h + 1) * DVH)
        oh = o[:, vs]
        ms = jnp.mean(oh * oh, axis=-1, keepdims=True)
        parts.append(oh * lax.rsqrt(ms + RMS_EPS) * ng[:, vs])
    on = jnp.concatenate(parts, axis=1)
    gated = on * (r * jax.nn.sigmoid(r))
    y = _dot(gated.astype(bf16), wo_ref[...])
    return _layer_norm(ALPHA * x + y, lg, lb)


PROJ_TILE = 512


def _interleave(main_tasks, filler_tasks):
    n_main, n_fill = len(main_tasks), len(filler_tasks)
    done = 0
    for idx, task in enumerate(main_tasks):
        task()
        want = ((idx + 1) * n_fill) // n_main
        while done < want:
            filler_tasks[done]()
            done += 1


def _gla_stage1_tasks(x_ref, slot, w_refs, scr):
    wq_ref, wgl_ref, wgu_ref, bgate_ref = w_refs
    xb_scr, q_scr, k_scr, v_scr, r_scr, g_scr = scr[:6]

    def cast_x():
        xb_scr[slot] = x_ref[...].astype(bf16)

    def proj(dst, c0, w0, scale):
        def run():
            acc = _dot(xb_scr[slot], wq_ref[:, w0:w0 + PROJ_TILE])
            dst[slot, :, c0:c0 + PROJ_TILE] = acc * scale if scale is not None else acc
        return run

    def gate():
        gl = _dot(xb_scr[slot], wgl_ref[...])
        z = _dot(gl.astype(bf16), wgu_ref[...]) + bgate_ref[...]
        g_scr[slot] = _log_sigmoid(z) / TAU

    tiles = []
    for dst, w_base, width, scale in ((q_scr, 0, DK, Q_SCALE), (k_scr, DK, DK, None),
                                      (v_scr, 2 * DK, DV, None), (r_scr, 2 * DK + DV, DV, None)):
        for c0 in range(0, width, PROJ_TILE):
            tiles.append(proj(dst, c0, w_base + c0, scale))
    return cast_x, gate, tiles


def _split3(a):
    a1 = a.astype(bf16)
    r1 = a - a1.astype(f32)
    a2 = r1.astype(bf16)
    a3 = (r1 - a2.astype(f32)).astype(bf16)
    return a1, a2, a3


def _gla_stage2_tasks(xres_ref, y_ref, r0, slot, scr, s_scr, ng_ref, wo_ref, lg_ref, lb_ref, n_chunks):
    _, q_scr, k_scr, v_scr, r_scr, g_scr, b_scr, o_scr, gated_scr = scr
    tb = n_chunks * CHUNK

    def cumsum():
        row = lax.broadcasted_iota(jnp.int32, (tb, tb), 0)
        col = lax.broadcasted_iota(jnp.int32, (tb, tb), 1)
        tri = ((row >= col) & ((row // CHUNK) == (col // CHUNK))).astype(bf16)
        g1, g2, g3 = _split3(g_scr[slot])
        b_scr[...] = _dot(tri, g1) + _dot(tri, g2) + _dot(tri, g3)

    def head_chunk(c, h, cell):
        rs = slice(c * CHUNK, (c + 1) * CHUNK)
        ks = slice(h * DKH, (h + 1) * DKH)
        vs = slice(h * DVH, (h + 1) * DVH)

        def run():
            if h == 0:
                b = b_scr[rs, :]
                b_last = b[CHUNK - 1:CHUNK, :]
                qc = q_scr[slot, rs, :]
                kc = k_scr[slot, rs, :]
                cell["qd"] = (qc * jnp.exp(b)).astype(bf16)
                cell["kd"] = (kc * jnp.exp(-b)).astype(bf16)
                cell["kk"] = (kc * jnp.exp(b_last - b)).astype(bf16)
                cell["vb"] = v_scr[slot, rs, :].astype(bf16)
                cell["dec"] = jnp.exp(b_last)
            r64 = lax.broadcasted_iota(jnp.int32, (CHUNK, CHUNK), 0)
            c64 = lax.broadcasted_iota(jnp.int32, (CHUNK, CHUNK), 1)
            qd, kd, kk, vb, dec = cell["qd"], cell["kd"], cell["kk"], cell["vb"], cell["dec"]
            sc = jnp.where(r64 >= c64, _dg(qd[:, ks], kd[:, ks], _NT), 0.0).astype(bf16)
            st = s_scr[h]
            o_scr[rs, vs] = _dot(sc, vb[:, vs]) + _dg(qd[:, ks], st.astype(bf16), _NT)
            s_scr[h] = dec[:, ks] * st + _dg(vb[:, vs], kk[:, ks], _TN)
        return run

    def norm_gate(h):
        vs = slice(h * DVH, (h + 1) * DVH)

        def run():
            oh = o_scr[:, vs]
            rh = r_scr[slot, :, vs]
            ms = jnp.mean(oh * oh, axis=-1, keepdims=True)
            on = oh * lax.rsqrt(ms + RMS_EPS) * ng_ref[:, vs]
            gated_scr[:, vs] = (on * (rh * jax.nn.sigmoid(rh))).astype(bf16)
        return run

    def out_rows(m0, m1):
        def run():
            y = _dot(gated_scr[m0:m1, :], wo_ref[...])
            x = xres_ref[r0 + m0:r0 + m1, :]
            y_ref[r0 + m0:r0 + m1, :] = _layer_norm(ALPHA * x + y, lg_ref[...], lb_ref[...])
        return run

    tasks = [cumsum]
    for c in range(n_chunks):
        cell = {}
        for h in range(H):
            tasks.append(head_chunk(c, h, cell))
    tasks += [norm_gate(h) for h in range(H)]
    half = tb // 2
    tasks += [out_rows(0, half), out_rows(half, tb)]
    return tasks


def _gla_prompt_kernel(x0_ref, xa_ref, xb_ref, xres_ref, wq_ref, wgl_ref, wgu_ref, bgate_ref, ng_ref, wo_ref,
                       lg_ref, lb_ref, y_ref, st_ref, xb_scr, q_scr, k_scr, v_scr, r_scr, g_scr, b_scr, o_scr,
                       gated_scr, s_scr, *, n_chunks, steps_per_seq):
    i = pl.program_id(0)
    tb = n_chunks * CHUNK
    w_refs = (wq_ref, wgl_ref, wgu_ref, bgate_ref)
    scr = (xb_scr, q_scr, k_scr, v_scr, r_scr, g_scr, b_scr, o_scr, gated_scr)
    post = (ng_ref, wo_ref, lg_ref, lb_ref)

    @pl.when(i == 0)
    def _():
        cast_x, gate, tiles = _gla_stage1_tasks(x0_ref, 0, w_refs, scr)
        for task in [cast_x, gate] + tiles:
            task()

    @pl.when(i % steps_per_seq == 0)
    def _():
        s_scr[...] = jnp.zeros_like(s_scr)

    for r0, slot, x_next in ((0, 0, xa_ref), (tb, 1, xb_ref)):
        cast_x, gate, tiles = _gla_stage1_tasks(x_next, 1 - slot, w_refs, scr)
        stage2 = _gla_stage2_tasks(xres_ref, y_ref, r0, slot, scr, s_scr, *post, n_chunks)
        cast_x()
        _interleave([stage2[0], gate] + stage2[1:], tiles)

    @pl.when(i % steps_per_seq == steps_per_seq - 1)
    def _():
        for h in range(H):
            st_ref[0, h] = s_scr[h].T


def _gla_sample_kernel(x_ref, st_ref, wq_ref, wgl_ref, wgu_ref, bgate_ref, ng_ref, wo_ref, lg_ref, lb_ref,
                       y_ref, sto_ref, *, n_seq, t_valid):
    rows = n_seq * SUB
    x = x_ref[...]
    q, k, v, r, g = _gla_project(x.astype(bf16), wq_ref, wgl_ref, wgu_ref, bgate_ref)

    row = lax.broadcasted_iota(jnp.int32, (rows, rows), 0)
    col = lax.broadcasted_iota(jnp.int32, (rows, rows), 1)
    same_seq = (row // SUB) == (col // SUB)
    tri = (same_seq & (row >= col)).astype(f32)
    tri_last = (same_seq & ((col % SUB) < t_valid)).astype(f32)
    b = jnp.dot(tri, g, precision=lax.Precision.HIGHEST, preferred_element_type=f32)
    b_last = jnp.dot(tri_last, g, precision=lax.Precision.HIGHEST, preferred_element_type=f32)

    valid = (lax.broadcasted_iota(jnp.int32, (rows, DK), 0) % SUB) < t_valid
    qd = (q * jnp.exp(b)).astype(bf16)
    kd = jnp.where(valid, k * jnp.exp(-b), 0.0).astype(bf16)
    kk = jnp.where(valid, k * jnp.exp(b_last - b), 0.0).astype(bf16)
    vb = v.astype(bf16)
    dec_t = jnp.exp(b_last.T)

    r8 = lax.broadcasted_iota(jnp.int32, (SUB, SUB), 0)
    c8 = lax.broadcasted_iota(jnp.int32, (SUB, SUB), 1)
    causal = r8 >= c8

    o_seqs = []
    for n in range(n_seq):
        rs = slice(n * SUB, (n + 1) * SUB)
        o_heads = []
        for h in range(H):
            ks = slice(h * DKH, (h + 1) * DKH)
            vs = slice(h * DVH, (h + 1) * DVH)
            s0 = st_ref[n, h]
            sc = jnp.where(causal, _dg(qd[rs, ks], kd[rs, ks], _NT), 0.0).astype(bf16)
            o_heads.append(_dot(sc, vb[rs, vs]) + _dot(qd[rs, ks], s0.astype(bf16)))
            dcol = dec_t[ks, n * SUB:n * SUB + 1]
            sto_ref[n, h] = dcol * s0 + _dg(kk[rs, ks], vb[rs, vs], _TN)
        o_seqs.append(jnp.concatenate(o_heads, axis=1))
    o = jnp.concatenate(o_seqs, axis=0)

    y_ref[...] = _gla_post(o, r, x, ng_ref[...], wo_ref, lg_ref[...], lb_ref[...])


def _conv_kernel(*refs, t_seq, carry):
    if carry:
        x_ref, win_ref, wc_ref, wout_ref, lg_ref, lb_ref, y_ref, st_ref, c_scr = refs
        x = x_ref[0]
    else:
        x_ref, e1_ref, e2_ref, win_ref, wc_ref, wout_ref, lg_ref, lb_ref, y_ref, u_ref = refs
        x = x_ref[...]
    rows = x.shape[0]
    xb = x.astype(bf16)
    bg = _dot(xb, win_ref[:, 0:D])
    cg = _dot(xb, win_ref[:, D:2 * D])
    hh = _dot(xb, win_ref[:, 2 * D:3 * D])
    u = cg * hh
    r1 = pltpu.roll(u, 1, 0)
    r2 = pltpu.roll(u, 2, 0)
    t = lax.broadcasted_iota(jnp.int32, (rows, D), 0)
    if carry:
        @pl.when(pl.program_id(1) == 0)
        def _():
            c_scr[...] = jnp.zeros_like(c_scr)
        c0 = c_scr[0:1, :]
        c1 = c_scr[1:2, :]
        p1 = jnp.where(t == 0, c1, r1)
        p2 = jnp.where(t == 0, c0, jnp.where(t == 1, c1, r2))
    else:
        t = t % t_seq
        p1 = jnp.where(t >= 1, r1, e1_ref[...])
        p2 = jnp.where(t >= 2, r2, e2_ref[...])
    conv = p2 * wc_ref[0:1, :] + p1 * wc_ref[1:2, :] + u * wc_ref[2:3, :]
    y = _dot((bg * conv).astype(bf16), wout_ref[...])
    out = _layer_norm(ALPHA * x + y, lg_ref[...], lb_ref[...])
    if carry:
        y_ref[0] = out
        last2 = u[rows - 2:rows, :]
        c_scr[0:2, :] = last2
        st_ref[0] = last2
    else:
        y_ref[...] = out
        u_ref[...] = u


def _mlp_kernel(x_ref, wup_ref, wdn_ref, lg_ref, lb_ref, y_ref, xb_scr, h_scr, *, n_sub, sub_rows, ff_chunk):
    def sub_tasks(s):
        rs = slice(s * sub_rows, (s + 1) * sub_rows)
        buf = s % 2

        def cast():
            xb_scr[buf] = x_ref[rs, :].astype(bf16)

        def up(c):
            cs = slice(c * ff_chunk, (c + 1) * ff_chunk)

            def run():
                hcol = _dot(xb_scr[buf], wup_ref[0, :, cs])
                h_scr[buf, :, cs] = jnp.square(jnp.maximum(hcol, 0.0)).astype(bf16)
            return run

        def down():
            y = _dot(h_scr[buf], wdn_ref[0])
            y_ref[rs, :] = _layer_norm(ALPHA * x_ref[rs, :] + y, lg_ref[0], lb_ref[0])

        return [cast] + [up(c) for c in range(D_FF // ff_chunk)], down

    pending = None
    for s in range(n_sub):
        ups, down = sub_tasks(s)
        ups[0]()
        ups[1]()
        if pending is not None:
            pending()
        for task in ups[2:]:
            task()
        pending = down
    pending()


def _const_spec(shape):
    nd = len(shape)
    return pl.BlockSpec(shape, lambda *_: (0,) * nd, pipeline_mode=pl.Buffered(1))


def _layer_spec(shape, layer):
    nd = len(shape)
    return pl.BlockSpec((1,) + tuple(shape[1:]), lambda *_: (layer,) + (0,) * (nd - 1),
                        pipeline_mode=pl.Buffered(1))


def _mlp(x2d, wup, wdn, lg, lb, layer, n_sub, sub_rows):
    m = x2d.shape[0]
    tm = n_sub * sub_rows
    return pl.pallas_call(
        functools.partial(_mlp_kernel, n_sub=n_sub, sub_rows=sub_rows, ff_chunk=1024),
        grid=(m // tm,),
        in_specs=[pl.BlockSpec((tm, D), lambda i: (i, 0)),
                  _layer_spec(wup.shape, layer), _layer_spec(wdn.shape, layer),
                  _layer_spec(lg.shape, layer), _layer_spec(lb.shape, layer)],
        out_specs=pl.BlockSpec((tm, D), lambda i: (i, 0)),
        out_shape=jax.ShapeDtypeStruct((m, D), f32),
        scratch_shapes=[pltpu.VMEM((2, sub_rows, D), bf16), pltpu.VMEM((2, sub_rows, D_FF), bf16)],
        compiler_params=pltpu.CompilerParams(dimension_semantics=("arbitrary",), vmem_limit_bytes=VMEM_LIMIT),
        name="mlp",
    )(x2d, wup, wdn, lg, lb)


def _gla_prompt(x, wq, wgl, wgu, bgate, ng, wo, lg, lb, tb):
    bsz, t, _ = x.shape
    n_chunks = tb // CHUNK
    n_blocks = bsz * t // tb
    steps_per_seq = t // (2 * tb)
    x2d = x.reshape(bsz * t, D)
    y, st = pl.pallas_call(
        functools.partial(_gla_prompt_kernel, n_chunks=n_chunks, steps_per_seq=steps_per_seq),
        grid=(n_blocks // 2,),
        in_specs=[pl.BlockSpec((tb, D), lambda i: (0, 0), pipeline_mode=pl.Buffered(1)),
                  pl.BlockSpec((tb, D), lambda i: (2 * i + 1, 0)),
                  pl.BlockSpec((tb, D), lambda i: (jnp.minimum(2 * i + 2, n_blocks - 1), 0)),
                  pl.BlockSpec((2 * tb, D), lambda i: (i, 0)),
                  _const_spec(wq.shape), _const_spec(wgl.shape), _const_spec(wgu.shape),
                  _const_spec(bgate.shape), _const_spec(ng.shape), _const_spec(wo.shape),
                  _const_spec(lg.shape), _const_spec(lb.shape)],
        out_specs=[pl.BlockSpec((2 * tb, D), lambda i: (i, 0)),
                   pl.BlockSpec((1, H, DKH, DVH), lambda i: (i // steps_per_seq, 0, 0, 0))],
        out_shape=[jax.ShapeDtypeStruct((bsz * t, D), f32),
                   jax.ShapeDtypeStruct((bsz, H, DKH, DVH), f32)],
        scratch_shapes=[pltpu.VMEM((2, tb, D), bf16),
                        pltpu.VMEM((2, tb, DK), f32), pltpu.VMEM((2, tb, DK), f32),
                        pltpu.VMEM((2, tb, DV), f32), pltpu.VMEM((2, tb, DV), f32),
                        pltpu.VMEM((2, tb, DK), f32),
                        pltpu.VMEM((tb, DK), f32), pltpu.VMEM((tb, DV), f32), pltpu.VMEM((tb, DV), bf16),
                        pltpu.VMEM((H, DVH, DKH), f32)],
        compiler_params=pltpu.CompilerParams(dimension_semantics=("arbitrary",),
                                             vmem_limit_bytes=VMEM_LIMIT),
        name="gla_prompt",
    )(x2d, x2d, x2d, x2d, wq, wgl, wgu, bgate, ng, wo, lg, lb)
    return y.reshape(bsz, t, D), st


def _gla_sample(x_pad, state, wq, wgl, wgu, bgate, ng, wo, lg, lb, n_seq, t_valid):
    s_total = state.shape[0]
    rows = n_seq * SUB
    return pl.pallas_call(
        functools.partial(_gla_sample_kernel, n_seq=n_seq, t_valid=t_valid),
        grid=(s_total // n_seq,),
        in_specs=[pl.BlockSpec((rows, D), lambda i: (i, 0)),
                  pl.BlockSpec((n_seq, H, DKH, DVH), lambda i: (i, 0, 0, 0)),
                  _const_spec(wq.shape), _const_spec(wgl.shape), _const_spec(wgu.shape),
                  _const_spec(bgate.shape), _const_spec(ng.shape), _const_spec(wo.shape),
                  _const_spec(lg.shape), _const_spec(lb.shape)],
        out_specs=[pl.BlockSpec((rows, D), lambda i: (i, 0)),
                   pl.BlockSpec((n_seq, H, DKH, DVH), lambda i: (i, 0, 0, 0))],
        out_shape=[jax.ShapeDtypeStruct((s_total * SUB, D), f32),
                   jax.ShapeDtypeStruct(state.shape, f32)],
        compiler_params=pltpu.CompilerParams(dimension_semantics=("arbitrary",), vmem_limit_bytes=VMEM_LIMIT),
        name="gla_sample",
    )(x_pad, state, wq, wgl, wgu, bgate, ng, wo, lg, lb)


def _conv_prompt(x, win, wc, wout, lg, lb, tb):
    bsz, t, _ = x.shape
    return pl.pallas_call(
        functools.partial(_conv_kernel, t_seq=t, carry=True),
        grid=(bsz, t // tb),
        in_specs=[pl.BlockSpec((1, tb, D), lambda b, j: (b, j, 0)),
                  _const_spec(win.shape), _const_spec(wc.shape), _const_spec(wout.shape),
                  _const_spec(lg.shape), _const_spec(lb.shape)],
        out_specs=[pl.BlockSpec((1, tb, D), lambda b, j: (b, j, 0)),
                   pl.BlockSpec((1, 2, D), lambda b, j: (b, 0, 0))],
        out_shape=[jax.ShapeDtypeStruct((bsz, t, D), f32),
                   jax.ShapeDtypeStruct((bsz, 2, D), f32)],
        scratch_shapes=[pltpu.VMEM((SUB, D), f32)],
        compiler_params=pltpu.CompilerParams(dimension_semantics=("arbitrary", "arbitrary"),
                                             vmem_limit_bytes=VMEM_LIMIT),
        name="conv_prompt",
    )(x, win, wc, wout, lg, lb)


def _conv_sample(x2d, e1, e2, win, wc, wout, lg, lb, t_seq):
    m = x2d.shape[0]
    full = lambda shape: pl.BlockSpec(shape, lambda i: (0,) * len(shape))
    return pl.pallas_call(
        functools.partial(_conv_kernel, t_seq=t_seq, carry=False),
        grid=(1,),
        in_specs=[full((m, D)), full((m, D)), full((m, D)),
                  _const_spec(win.shape), _const_spec(wc.shape), _const_spec(wout.shape),
                  _const_spec(lg.shape), _const_spec(lb.shape)],
        out_specs=[full((m, D)), full((m, D))],
        out_shape=[jax.ShapeDtypeStruct((m, D), f32), jax.ShapeDtypeStruct((m, D), f32)],
        compiler_params=pltpu.CompilerParams(dimension_semantics=("arbitrary",), vmem_limit_bytes=VMEM_LIMIT),
        name="conv_sample",
    )(x2d, e1, e2, win, wc, wout, lg, lb)


def kernel(x_prompt, x_sample, state_gla, state_conv, gla_w_in, gla_w_gate_up, gla_b_gate, gla_norm_g, gla_w_o, conv_w_in, conv_w_conv, conv_w_out, mlp_w_up, mlp_w_down, ln1_g, ln1_b, ln2_g, ln2_b):
    bsz, t, _ = x_prompt.shape
    n_dec, t_dec, _ = x_sample.shape
    assert t % CHUNK == 0 and t_dec <= SUB and t_dec >= 2

    w_in = gla_w_in[0]
    wq = w_in.astype(bf16)
    wgl = jnp.pad(w_in[:, 2 * DK + 2 * DV:], ((0, 0), (0, RANK_PAD - RANK))).astype(bf16)
    wgu = jnp.pad(gla_w_gate_up[0], ((0, RANK_PAD - RANK), (0, 0))).astype(bf16)
    bgate = gla_b_gate[0].reshape(1, DK)
    ng = gla_norm_g[0].reshape(1, DV)
    wo = gla_w_o[0].astype(bf16)
    cwin = conv_w_in[0].astype(bf16)
    cwc = conv_w_conv[0]
    cwout = conv_w_out[0].astype(bf16)
    wup = mlp_w_up.astype(bf16)
    wdn = mlp_w_down.astype(bf16)
    ln2g = ln2_g.reshape(DEPTH, 1, D)
    ln2b = ln2_b.reshape(DEPTH, 1, D)
    row = lambda a, i: a[i].reshape(1, D)
    mlp_prompt = functools.partial(_mlp, n_sub=4, sub_rows=256)
    mlp_sample = functools.partial(_mlp, n_sub=2, sub_rows=n_dec * t_dec // 2)

    xp, gla_p = _gla_prompt(x_prompt, wq, wgl, wgu, bgate, ng, wo, row(ln1_g, 0), row(ln1_b, 0), tb=256)
    xs_pad = jnp.pad(x_sample, ((0, 0), (0, SUB - t_dec), (0, 0))).reshape(n_dec * SUB, D)
    xs_pad, gla_s = _gla_sample(xs_pad, state_gla[0], wq, wgl, wgu, bgate, ng, wo, row(ln1_g, 0), row(ln1_b, 0),
                                n_seq=16, t_valid=t_dec)
    xs = xs_pad.reshape(n_dec, SUB, D)[:, :t_dec].reshape(n_dec * t_dec, D)
    xp = mlp_prompt(xp.reshape(bsz * t, D), wup, wdn, ln2g, ln2b, layer=0)
    xs = mlp_sample(xs, wup, wdn, ln2g, ln2b, layer=0)

    xp, conv_p = _conv_prompt(xp.reshape(bsz, t, D), cwin, cwc, cwout, row(ln1_g, 1), row(ln1_b, 1), tb=512)
    buf = state_conv[0]
    e1 = jnp.pad(buf[:, 1:2], ((0, 0), (0, t_dec - 1), (0, 0))).reshape(n_dec * t_dec, D)
    e2 = jnp.pad(buf, ((0, 0), (0, t_dec - 2), (0, 0))).reshape(n_dec * t_dec, D)
    xs, u_s = _conv_sample(xs, e1, e2, cwin, cwc, cwout, row(ln1_g, 1), row(ln1_b, 1), t_seq=t_dec)
    conv_s = u_s.reshape(n_dec, t_dec, D)[:, t_dec - 2:]
    xp = mlp_prompt(xp.reshape(bsz * t, D), wup, wdn, ln2g, ln2b, layer=1)
    xs = mlp_sample(xs, wup, wdn, ln2g, ln2b, layer=1)

    return (xp.reshape(bsz, t, D), xs.reshape(n_dec, t_dec, D), gla_p[None], gla_s[None],
            conv_p[None], conv_s[None])
```

```python
import functools

import jax
import jax.numpy as jnp
from jax import lax
from jax.experimental import pallas as pl
from jax.experimental.pallas import tpu as pltpu

bf16 = jnp.bfloat16
f32 = jnp.float32

D = 1024
H = 4
DK = 512
DV = 1024
DKH = DK // H
DVH = DV // H
RANK = 16
RANK_PAD = 128
TAU = 16.0
CHUNK = 64
D_FF = 4 * D
DEPTH = 2
ALPHA = (2 * DEPTH) ** 0.25
LN_EPS = 1e-5
RMS_EPS = 1e-6
Q_SCALE = DKH ** -0.5

SUB = 8
VMEM_LIMIT = 56 * 1024 * 1024

_NT = (((1,), (1,)), ((), ()))
_TN = (((0,), (0,)), ((), ()))


def _dot(a, b):
    return jnp.dot(a, b, preferred_element_type=f32)


def _dg(a, b, dims):
    return lax.dot_general(a, b, dims, preferred_element_type=f32)


def _layer_norm(y, g, b):
    mu = jnp.mean(y, axis=-1, keepdims=True)
    yc = y - mu
    var = jnp.mean(yc * yc, axis=-1, keepdims=True)
    return yc * lax.rsqrt(var + LN_EPS) * g + b


def _log_sigmoid(z):
    return -(jnp.maximum(-z, 0.0) + jnp.log1p(jnp.exp(-jnp.abs(z))))


def _gla_project(xb, wq_ref, wgl_ref, wgu_ref, bgate_ref):
    q = _dot(xb, wq_ref[:, 0:DK]) * Q_SCALE
    k = _dot(xb, wq_ref[:, DK:2 * DK])
    v = _dot(xb, wq_ref[:, 2 * DK:2 * DK + DV])
    r = _dot(xb, wq_ref[:, 2 * DK + DV:2 * DK + 2 * DV])
    gl = _dot(xb, wgl_ref[...])
    z = _dot(gl.astype(bf16), wgu_ref[...]) + bgate_ref[...]
    g = _log_sigmoid(z) / TAU
    return q, k, v, r, g


def _gla_post(o, r, x, ng, wo_ref, lg, lb):
    parts = []
    for h in range(H):
        vs = slice(h * DVH, (h + 1) * DVH)
        oh = o[:, vs]
        ms = jnp.mean(oh * oh, axis=-1, keepdims=True)
        parts.append(oh * lax.rsqrt(ms + RMS_EPS) * ng[:, vs])
    on = jnp.concatenate(parts, axis=1)
    gated = on * (r * jax.nn.sigmoid(r))
    y = _dot(gated.astype(bf16), wo_ref[...])
    return _layer_norm(ALPHA * x + y, lg, lb)


PROJ_TILE = 512


def _interleave(main_tasks, filler_tasks):
    n_main, n_fill = len(main_tasks), len(filler_tasks)
    done = 0
    for idx, task in enumerate(main_tasks):
        task()
        want = ((idx + 1) * n_fill) // n_main
        while done < want:
            filler_tasks[done]()
            done += 1


def _gla_stage1_tasks(x_ref, slot, w_refs, scr):
    wq_ref, wgl_ref, wgu_ref, bgate_ref = w_refs
    xb_scr, q_scr, k_scr, v_scr, r_scr, g_scr = scr[:6]

    def cast_x():
        xb_scr[slot] = x_ref[...].astype(bf16)

    def proj(dst, c0, w0, scale):
        def run():
            acc = _dot(xb_scr[slot], wq_ref[:, w0:w0 + PROJ_TILE])
            dst[slot, :, c0:c0 + PROJ_TILE] = acc * scale if scale is not None else acc
        return run

    def gate():
        gl = _dot(xb_scr[slot], wgl_ref[...])
        z = _dot(gl.astype(bf16), wgu_ref[...]) + bgate_ref[...]
        g_scr[slot] = _log_sigmoid(z) / TAU

    tiles = []
    for dst, w_base, width, scale in ((q_scr, 0, DK, Q_SCALE), (k_scr, DK, DK, None),
                                      (v_scr, 2 * DK, DV, None), (r_scr, 2 * DK + DV, DV, None)):
        for c0 in range(0, width, PROJ_TILE):
            tiles.append(proj(dst, c0, w_base + c0, scale))
    return cast_x, gate, tiles


def _split3(a):
    a1 = a.astype(bf16)
    r1 = a - a1.astype(f32)
    a2 = r1.astype(bf16)
    a3 = (r1 - a2.astype(f32)).astype(bf16)
    return a1, a2, a3


def _gla_stage2_tasks(xres_ref, y_ref, r0, slot, scr, s_scr, ng_ref, wo_ref, lg_ref, lb_ref, n_chunks):
    _, q_scr, k_scr, v_scr, r_scr, g_scr, b_scr, o_scr, gated_scr = scr
    tb = n_chunks * CHUNK

    def cumsum():
        row = lax.broadcasted_iota(jnp.int32, (tb, tb), 0)
        col = lax.broadcasted_iota(jnp.int32, (tb, tb), 1)
        tri = ((row >= col) & ((row // CHUNK) == (col // CHUNK))).astype(bf16)
        g1, g2, g3 = _split3(g_scr[slot])
        b_scr[...] = _dot(tri, g1) + _dot(tri, g2) + _dot(tri, g3)

    def head_chunk(c, h, cell):
        rs = slice(c * CHUNK, (c + 1) * CHUNK)
        ks = slice(h * DKH, (h + 1) * DKH)
        vs = slice(h * DVH, (h + 1) * DVH)

        def run():
            if h == 0:
                b = b_scr[rs, :]
                b_last = b[CHUNK - 1:CHUNK, :]
                qc = q_scr[slot, rs, :]
                kc = k_scr[slot, rs, :]
                cell["qd"] = (qc * jnp.exp(b)).astype(bf16)
                cell["kd"] = (kc * jnp.exp(-b)).astype(bf16)
                cell["kk"] = (kc * jnp.exp(b_last - b)).astype(bf16)
                cell["vb"] = v_scr[slot, rs, :].astype(bf16)
                cell["dec"] = jnp.exp(b_last)
            r64 = lax.broadcasted_iota(jnp.int32, (CHUNK, CHUNK), 0)
            c64 = lax.broadcasted_iota(jnp.int32, (CHUNK, CHUNK), 1)
            qd, kd, kk, vb, dec = cell["qd"], cell["kd"], cell["kk"], cell["vb"], cell["dec"]
            sc = jnp.where(r64 >= c64, _dg(qd[:, ks], kd[:, ks], _NT), 0.0).astype(bf16)
            st = s_scr[h]
            o_scr[rs, vs] = _dot(sc, vb[:, vs]) + _dg(qd[:, ks], st.astype(bf16), _NT)
            s_scr[h] = dec[:, ks] * st + _dg(vb[:, vs], kk[:, ks], _TN)
        return run

    def norm_gate(h):
        vs = slice(h * DVH, (h + 1) * DVH)

        def run():
            oh = o_scr[:, vs]
            rh = r_scr[slot, :, vs]
            ms = jnp.mean(oh * oh, axis=-1, keepdims=True)
            on = oh * lax.rsqrt(ms + RMS_EPS) * ng_ref[:, vs]
            gated_scr[:, vs] = (on * (rh * jax.nn.sigmoid(rh))).astype(bf16)
        return run

    def out_rows(m0, m1):
        def run():
            y = _dot(gated_scr[m0:m1, :], wo_ref[...])
            x = xres_ref[r0 + m0:r0 + m1, :]
            y_ref[r0 + m0:r0 + m1, :] = _layer_norm(ALPHA * x + y, lg_ref[...], lb_ref[...])
        return run

    tasks = [cumsum]
    for c in range(n_chunks):
        cell = {}
        for h in range(H):
            tasks.append(head_chunk(c, h, cell))
    tasks += [norm_gate(h) for h in range(H)]
    half = tb // 2
    tasks += [out_rows(0, half), out_rows(half, tb)]
    return tasks


def _gla_prompt_kernel(x0_ref, xa_ref, xb_ref, xres_ref, wq_ref, wgl_ref, wgu_ref, bgate_ref, ng_ref, wo_ref,
                       lg_ref, lb_ref, y_ref, st_ref, xb_scr, q_scr, k_scr, v_scr, r_scr, g_scr, b_scr, o_scr,
                       gated_scr, s_scr, *, n_chunks, steps_per_seq):
    i = pl.program_id(0)
    tb = n_chunks * CHUNK
    w_refs = (wq_ref, wgl_ref, wgu_ref, bgate_ref)
    scr = (xb_scr, q_scr, k_scr, v_scr, r_scr, g_scr, b_scr, o_scr, gated_scr)
    post = (ng_ref, wo_ref, lg_ref, lb_ref)

    @pl.when(i == 0)
    def _():
        cast_x, gate, tiles = _gla_stage1_tasks(x0_ref, 0, w_refs, scr)
        for task in [cast_x, gate] + tiles:
            task()

    @pl.when(i % steps_per_seq == 0)
    def _():
        s_scr[...] = jnp.zeros_like(s_scr)

    for r0, slot, x_next in ((0, 0, xa_ref), (tb, 1, xb_ref)):
        cast_x, gate, tiles = _gla_stage1_tasks(x_next, 1 - slot, w_refs, scr)
        stage2 = _gla_stage2_tasks(xres_ref, y_ref, r0, slot, scr, s_scr, *post, n_chunks)
        cast_x()
        _interleave([stage2[0], gate] + stage2[1:], tiles)

    @pl.when(i % steps_per_seq == steps_per_seq - 1)
    def _():
        for h in range(H):
            st_ref[0, h] = s_scr[h].T


def _gla_sample_kernel(x_ref, st_ref, wq_ref, wgl_ref, wgu_ref, bgate_ref, ng_ref, wo_ref, lg_ref, lb_ref,
                       y_ref, sto_ref, *, n_seq, t_valid):
    rows = n_seq * SUB
    x = x_ref[...]
    q, k, v, r, g = _gla_project(x.astype(bf16), wq_ref, wgl_ref, wgu_ref, bgate_ref)

    row = lax.broadcasted_iota(jnp.int32, (rows, rows), 0)
    col = lax.broadcasted_iota(jnp.int32, (rows, rows), 1)
    same_seq = (row // SUB) == (col // SUB)
    tri = (same_seq & (row >= col)).astype(f32)
    tri_last = (same_seq & ((col % SUB) < t_valid)).astype(f32)
    b = jnp.dot(tri, g, precision=lax.Precision.HIGHEST, preferred_element_type=f32)
    b_last = jnp.dot(tri_last, g, precision=lax.Precision.HIGHEST, preferred_element_type=f32)

    valid = (lax.broadcasted_iota(jnp.int32, (rows, DK), 0) % SUB) < t_valid
    qd = (q * jnp.exp(b)).astype(bf16)
    kd = jnp.where(valid, k * jnp.exp(-b), 0.0).astype(bf16)
    kk = jnp.where(valid, k * jnp.exp(b_last - b), 0.0).astype(bf16)
    vb = v.astype(bf16)
    dec_t = jnp.exp(b_last.T)

    r8 = lax.broadcasted_iota(jnp.int32, (SUB, SUB), 0)
    c8 = lax.broadcasted_iota(jnp.int32, (SUB, SUB), 1)
    causal = r8 >= c8

    o_seqs = []
    for n in range(n_seq):
        rs = slice(n * SUB, (n + 1) * SUB)
        o_heads = []
        for h in range(H):
            ks = slice(h * DKH, (h + 1) * DKH)
            vs = slice(h * DVH, (h + 1) * DVH)
            s0 = st_ref[n, h]
            sc = jnp.where(causal, _dg(qd[rs, ks], kd[rs, ks], _NT), 0.0).astype(bf16)
            o_heads.append(_dot(sc, vb[rs, vs]) + _dot(qd[rs, ks], s0.astype(bf16)))
            dcol = dec_t[ks, n * SUB:n * SUB + 1]
            sto_ref[n, h] = dcol * s0 + _dg(kk[rs, ks], vb[rs, vs], _TN)
        o_seqs.append(jnp.concatenate(o_heads, axis=1))
    o = jnp.concatenate(o_seqs, axis=0)

    y_ref[...] = _gla_post(o, r, x, ng_ref[...], wo_ref, lg_ref[...], lb_ref[...])


def _conv_prompt_kernel(x_ref, win_ref, wc_ref, wout_ref, lg_ref, lb_ref, y_ref, st_ref,
                        xb_scr, bg_scr, cg_scr, u_scr, gated_scr, c_scr, *, n_sub, sub_rows):
    @pl.when(pl.program_id(1) == 0)
    def _():
        c_scr[...] = jnp.zeros_like(c_scr)

    def sub_tasks(s):
        rs = slice(s * sub_rows, (s + 1) * sub_rows)
        buf = s % 2

        def cast():
            xb_scr[buf] = x_ref[0, rs, :].astype(bf16)

        def proj_bg():
            bg_scr[buf] = _dot(xb_scr[buf], win_ref[:, 0:D])

        def proj_cg():
            cg_scr[...] = _dot(xb_scr[buf], win_ref[:, D:2 * D])

        def proj_u():
            u_scr[buf] = cg_scr[...] * _dot(xb_scr[buf], win_ref[:, 2 * D:3 * D])

        def conv():
            u = u_scr[buf]
            t = lax.broadcasted_iota(jnp.int32, (sub_rows, D), 0)
            c0 = c_scr[0:1, :]
            c1 = c_scr[1:2, :]
            p1 = jnp.where(t == 0, c1, pltpu.roll(u, 1, 0))
            p2 = jnp.where(t == 0, c0, jnp.where(t == 1, c1, pltpu.roll(u, 2, 0)))
            cv = p2 * wc_ref[0:1, :] + p1 * wc_ref[1:2, :] + u * wc_ref[2:3, :]
            gated_scr[buf] = (bg_scr[buf] * cv).astype(bf16)
            c_scr[0:2, :] = u[sub_rows - 2:sub_rows, :]

        def out():
            y = _dot(gated_scr[buf], wout_ref[...])
            y_ref[0, rs, :] = _layer_norm(ALPHA * x_ref[0, rs, :] + y, lg_ref[...], lb_ref[...])

        return (cast, proj_bg, proj_cg, proj_u), (conv, out)

    pending = None
    for s in range(n_sub):
        (cast, proj_bg, proj_cg, proj_u), tail = sub_tasks(s)
        cast()
        proj_bg()
        if pending is not None:
            pending[0]()
        proj_cg()
        if pending is not None:
            pending[1]()
        proj_u()
        pending = tail
    pending[0]()
    pending[1]()
    st_ref[0] = c_scr[0:2, :]


def _conv_sample_kernel(x_ref, e1_ref, e2_ref, win_ref, wc_ref, wout_ref, lg_ref, lb_ref, y_ref, u_ref, *, t_seq):
    x = x_ref[...]
    rows = x.shape[0]
    xb = x.astype(bf16)
    bg = _dot(xb, win_ref[:, 0:D])
    cg = _dot(xb, win_ref[:, D:2 * D])
    hh = _dot(xb, win_ref[:, 2 * D:3 * D])
    u = cg * hh
    t = lax.broadcasted_iota(jnp.int32, (rows, D), 0) % t_seq
    p1 = jnp.where(t >= 1, pltpu.roll(u, 1, 0), e1_ref[...])
    p2 = jnp.where(t >= 2, pltpu.roll(u, 2, 0), e2_ref[...])
    conv = p2 * wc_ref[0:1, :] + p1 * wc_ref[1:2, :] + u * wc_ref[2:3, :]
    y = _dot((bg * conv).astype(bf16), wout_ref[...])
    y_ref[...] = _layer_norm(ALPHA * x + y, lg_ref[...], lb_ref[...])
    u_ref[...] = u


def _mlp_kernel(x_ref, wup_ref, wdn_ref, lg_ref, lb_ref, y_ref, xb_scr, h_scr, *, n_sub, sub_rows, ff_chunk):
    def sub_tasks(s):
        rs = slice(s * sub_rows, (s + 1) * sub_rows)
        buf = s % 2

        def cast():
            xb_scr[buf] = x_ref[rs, :].astype(bf16)

        def up(c):
            cs = slice(c * ff_chunk, (c + 1) * ff_chunk)

            def run():
                hcol = _dot(xb_scr[buf], wup_ref[0, :, cs])
                h_scr[buf, :, cs] = jnp.square(jnp.maximum(hcol, 0.0)).astype(bf16)
            return run

        def down():
            y = _dot(h_scr[buf], wdn_ref[0])
            y_ref[rs, :] = _layer_norm(ALPHA * x_ref[rs, :] + y, lg_ref[0], lb_ref[0])

        return [cast] + [up(c) for c in range(D_FF // ff_chunk)], down

    pending = None
    for s in range(n_sub):
        ups, down = sub_tasks(s)
        ups[0]()
        ups[1]()
        if pending is not None:
            pending()
        for task in ups[2:]:
            task()
        pending = down
    pending()


def _const_spec(shape):
    nd = len(shape)
    return pl.BlockSpec(shape, lambda *_: (0,) * nd, pipeline_mode=pl.Buffered(1))


def _layer_spec(shape, layer):
    nd = len(shape)
    return pl.BlockSpec((1,) + tuple(shape[1:]), lambda *_: (layer,) + (0,) * (nd - 1),
                        pipeline_mode=pl.Buffered(1))


def _mlp(x2d, wup, wdn, lg, lb, layer, n_sub, sub_rows):
    m = x2d.shape[0]
    tm = n_sub * sub_rows
    return pl.pallas_call(
        functools.partial(_mlp_kernel, n_sub=n_sub, sub_rows=sub_rows, ff_chunk=1024),
        grid=(m // tm,),
        in_specs=[pl.BlockSpec((tm, D), lambda i: (i, 0)),
                  _layer_spec(wup.shape, layer), _layer_spec(wdn.shape, layer),
                  _layer_spec(lg.shape, layer), _layer_spec(lb.shape, layer)],
        out_specs=pl.BlockSpec((tm, D), lambda i: (i, 0)),
        out_shape=jax.ShapeDtypeStruct((m, D), f32),
        scratch_shapes=[pltpu.VMEM((2, sub_rows, D), bf16), pltpu.VMEM((2, sub_rows, D_FF), bf16)],
        compiler_params=pltpu.CompilerParams(dimension_semantics=("arbitrary",), vmem_limit_bytes=VMEM_LIMIT),
        name="mlp",
    )(x2d, wup, wdn, lg, lb)


def _gla_prompt(x, wq, wgl, wgu, bgate, ng, wo, lg, lb, tb):
    bsz, t, _ = x.shape
    n_chunks = tb // CHUNK
    n_blocks = bsz * t // tb
    steps_per_seq = t // (2 * tb)
    x2d = x.reshape(bsz * t, D)
    y, st = pl.pallas_call(
        functools.partial(_gla_prompt_kernel, n_chunks=n_chunks, steps_per_seq=steps_per_seq),
        grid=(n_blocks // 2,),
        in_specs=[pl.BlockSpec((tb, D), lambda i: (0, 0), pipeline_mode=pl.Buffered(1)),
                  pl.BlockSpec((tb, D), lambda i: (2 * i + 1, 0)),
                  pl.BlockSpec((tb, D), lambda i: (jnp.minimum(2 * i + 2, n_blocks - 1), 0)),
                  pl.BlockSpec((2 * tb, D), lambda i: (i, 0)),
                  _const_spec(wq.shape), _const_spec(wgl.shape), _const_spec(wgu.shape),
                  _const_spec(bgate.shape), _const_spec(ng.shape), _const_spec(wo.shape),
                  _const_spec(lg.shape), _const_spec(lb.shape)],
        out_specs=[pl.BlockSpec((2 * tb, D), lambda i: (i, 0)),
                   pl.BlockSpec((1, H, DKH, DVH), lambda i: (i // steps_per_seq, 0, 0, 0))],
        out_shape=[jax.ShapeDtypeStruct((bsz * t, D), f32),
                   jax.ShapeDtypeStruct((bsz, H, DKH, DVH), f32)],
        scratch_shapes=[pltpu.VMEM((2, tb, D), bf16),
                        pltpu.VMEM((2, tb, DK), f32), pltpu.VMEM((2, tb, DK), f32),
                        pltpu.VMEM((2, tb, DV), f32), pltpu.VMEM((2, tb, DV), f32),
                        pltpu.VMEM((2, tb, DK), f32),
                        pltpu.VMEM((tb, DK), f32), pltpu.VMEM((tb, DV), f32), pltpu.VMEM((tb, DV), bf16),
                        pltpu.VMEM((H, DVH, DKH), f32)],
        compiler_params=pltpu.CompilerParams(dimension_semantics=("arbitrary",),
                                             vmem_limit_bytes=VMEM_LIMIT),
        name="gla_prompt",
    )(x2d, x2d, x2d, x2d, wq, wgl, wgu, bgate, ng, wo, lg, lb)
    return y.reshape(bsz, t, D), st


def _gla_sample(x_pad, state, wq, wgl, wgu, bgate, ng, wo, lg, lb, n_seq, t_valid):
    s_total = state.shape[0]
    rows = n_seq * SUB
    return pl.pallas_call(
        functools.partial(_gla_sample_kernel, n_seq=n_seq, t_valid=t_valid),
        grid=(s_total // n_seq,),
        in_specs=[pl.BlockSpec((rows, D), lambda i: (i, 0)),
                  pl.BlockSpec((n_seq, H, DKH, DVH), lambda i: (i, 0, 0, 0)),
                  _const_spec(wq.shape), _const_spec(wgl.shape), _const_spec(wgu.shape),
                  _const_spec(bgate.shape), _const_spec(ng.shape), _const_spec(wo.shape),
                  _const_spec(lg.shape), _const_spec(lb.shape)],
        out_specs=[pl.BlockSpec((rows, D), lambda i: (i, 0)),
                   pl.BlockSpec((n_seq, H, DKH, DVH), lambda i: (i, 0, 0, 0))],
        out_shape=[jax.ShapeDtypeStruct((s_total * SUB, D), f32),
                   jax.ShapeDtypeStruct(state.shape, f32)],
        compiler_params=pltpu.CompilerParams(dimension_semantics=("arbitrary",), vmem_limit_bytes=VMEM_LIMIT),
        name="gla_sample",
    )(x_pad, state, wq, wgl, wgu, bgate, ng, wo, lg, lb)


def _conv_prompt(x, win, wc, wout, lg, lb, n_sub, sub_rows):
    bsz, t, _ = x.shape
    tb = n_sub * sub_rows
    return pl.pallas_call(
        functools.partial(_conv_prompt_kernel, n_sub=n_sub, sub_rows=sub_rows),
        grid=(bsz, t // tb),
        in_specs=[pl.BlockSpec((1, tb, D), lambda b, j: (b, j, 0)),
                  _const_spec(win.shape), _const_spec(wc.shape), _const_spec(wout.shape),
                  _const_spec(lg.shape), _const_spec(lb.shape)],
        out_specs=[pl.BlockSpec((1, tb, D), lambda b, j: (b, j, 0)),
                   pl.BlockSpec((1, 2, D), lambda b, j: (b, 0, 0))],
        out_shape=[jax.ShapeDtypeStruct((bsz, t, D), f32),
                   jax.ShapeDtypeStruct((bsz, 2, D), f32)],
        scratch_shapes=[pltpu.VMEM((2, sub_rows, D), bf16), pltpu.VMEM((2, sub_rows, D), f32),
                        pltpu.VMEM((sub_rows, D), f32), pltpu.VMEM((2, sub_rows, D), f32),
                        pltpu.VMEM((2, sub_rows, D), bf16), pltpu.VMEM((SUB, D), f32)],
        compiler_params=pltpu.CompilerParams(dimension_semantics=("arbitrary", "arbitrary"),
                                             vmem_limit_bytes=VMEM_LIMIT),
        name="conv_prompt",
    )(x, win, wc, wout, lg, lb)


def _conv_sample(x2d, e1, e2, win, wc, wout, lg, lb, t_seq):
    m = x2d.shape[0]
    full = lambda shape: pl.BlockSpec(shape, lambda i: (0,) * len(shape))
    return pl.pallas_call(
        functools.partial(_conv_sample_kernel, t_seq=t_seq),
        grid=(1,),
        in_specs=[full((m, D)), full((m, D)), full((m, D)),
                  _const_spec(win.shape), _const_spec(wc.shape), _const_spec(wout.shape),
                  _const_spec(lg.shape), _const_spec(lb.shape)],
        out_specs=[full((m, D)), full((m, D))],
        out_shape=[jax.ShapeDtypeStruct((m, D), f32), jax.ShapeDtypeStruct((m, D), f32)],
        compiler_params=pltpu.CompilerParams(dimension_semantics=("arbitrary",), vmem_limit_bytes=VMEM_LIMIT),
        name="conv_sample",
    )(x2d, e1, e2, win, wc, wout, lg, lb)


def kernel(x_prompt, x_sample, state_gla, state_conv, gla_w_in, gla_w_gate_up, gla_b_gate, gla_norm_g, gla_w_o, conv_w_in, conv_w_conv, conv_w_out, mlp_w_up, mlp_w_down, ln1_g, ln1_b, ln2_g, ln2_b):
    bsz, t, _ = x_prompt.shape
    n_dec, t_dec, _ = x_sample.shape
    assert t % CHUNK == 0 and t_dec <= SUB and t_dec >= 2

    w_in = gla_w_in[0]
    wq = w_in.astype(bf16)
    wgl = jnp.pad(w_in[:, 2 * DK + 2 * DV:], ((0, 0), (0, RANK_PAD - RANK))).astype(bf16)
    wgu = jnp.pad(gla_w_gate_up[0], ((0, RANK_PAD - RANK), (0, 0))).astype(bf16)
    bgate = gla_b_gate[0].reshape(1, DK)
    ng = gla_norm_g[0].reshape(1, DV)
    wo = gla_w_o[0].astype(bf16)
    cwin = conv_w_in[0].astype(bf16)
    cwc = conv_w_conv[0]
    cwout = conv_w_out[0].astype(bf16)
    wup = mlp_w_up.astype(bf16)
    wdn = mlp_w_down.astype(bf16)
    ln2g = ln2_g.reshape(DEPTH, 1, D)
    ln2b = ln2_b.reshape(DEPTH, 1, D)
    row = lambda a, i: a[i].reshape(1, D)
    mlp_prompt = functools.partial(_mlp, n_sub=4, sub_rows=256)
    mlp_sample = functools.partial(_mlp, n_sub=2, sub_rows=n_dec * t_dec // 2)

    xp, gla_p = _gla_prompt(x_prompt, wq, wgl, wgu, bgate, ng, wo, row(ln1_g, 0), row(ln1_b, 0), tb=256)
    xs_pad = jnp.pad(x_sample, ((0, 0), (0, SUB - t_dec), (0, 0))).reshape(n_dec * SUB, D)
    xs_pad, gla_s = _gla_sample(xs_pad, state_gla[0], wq, wgl, wgu, bgate, ng, wo, row(ln1_g, 0), row(ln1_b, 0),
                                n_seq=16, t_valid=t_dec)
    xs = xs_pad.reshape(n_dec, SUB, D)[:, :t_dec].reshape(n_dec * t_dec, D)
    xp = mlp_prompt(xp.reshape(bsz * t, D), wup, wdn, ln2g, ln2b, layer=0)
    xs = mlp_sample(xs, wup, wdn, ln2g, ln2b, layer=0)

    xp, conv_p = _conv_prompt(xp.reshape(bsz, t, D), cwin, cwc, cwout, row(ln1_g, 1), row(ln1_b, 1),
                              n_sub=4, sub_rows=256)
    buf = state_conv[0]
    e1 = jnp.pad(buf[:, 1:2], ((0, 0), (0, t_dec - 1), (0, 0))).reshape(n_dec * t_dec, D)
    e2 = jnp.pad(buf, ((0, 0), (0, t_dec - 2), (0, 0))).reshape(n_dec * t_dec, D)
    xs, u_s = _conv_sample(xs, e1, e2, cwin, cwc, cwout, row(ln1_g, 1), row(ln1_b, 1), t_seq=t_dec)
    conv_s = u_s.reshape(n_dec, t_dec, D)[:, t_dec - 2:]
    xp = mlp_prompt(xp.reshape(bsz * t, D), wup, wdn, ln2g, ln2b, layer=1)
    xs = mlp_sample(xs, wup, wdn, ln2g, ln2b, layer=1)

    return (xp.reshape(bsz, t, D), xs.reshape(n_dec, t_dec, D), gla_p[None], gla_s[None],
            conv_p[None], conv_s[None])
```

```python
import functools

import jax
import jax.numpy as jnp
from jax import lax
from jax.experimental import pallas as pl
from jax.experimental.pallas import tpu as pltpu

bf16 = jnp.bfloat16
f32 = jnp.float32

D = 1024
H = 4
DK = 512
DV = 1024
DKH = DK // H
DVH = DV // H
RANK = 16
RANK_PAD = 128
TAU = 16.0
CHUNK = 64
KCHUNK = 128
D_FF = 4 * D
DEPTH = 2
ALPHA = (2 * DEPTH) ** 0.25
LN_EPS = 1e-5
RMS_EPS = 1e-6
Q_SCALE = DKH ** -0.5

SUB = 8
VMEM_LIMIT = 56 * 1024 * 1024

_NT = (((1,), (1,)), ((), ()))
_TN = (((0,), (0,)), ((), ()))


def _dot(a, b):
    return jnp.dot(a, b, preferred_element_type=f32)


def _dg(a, b, dims):
    return lax.dot_general(a, b, dims, preferred_element_type=f32)


def _layer_norm(y, g, b):
    mu = jnp.mean(y, axis=-1, keepdims=True)
    yc = y - mu
    var = jnp.mean(yc * yc, axis=-1, keepdims=True)
    return yc * lax.rsqrt(var + LN_EPS) * g + b


def _log_sigmoid(z):
    return -(jnp.maximum(-z, 0.0) + jnp.log(1.0 + jnp.exp(-jnp.abs(z))))


def _gla_project(xb, wq_ref, wgl_ref, wgu_ref, bgate_ref):
    q = _dot(xb, wq_ref[:, 0:DK]) * Q_SCALE
    k = _dot(xb, wq_ref[:, DK:2 * DK])
    v = _dot(xb, wq_ref[:, 2 * DK:2 * DK + DV])
    r = _dot(xb, wq_ref[:, 2 * DK + DV:2 * DK + 2 * DV])
    gl = _dot(xb, wgl_ref[...])
    z = _dot(gl.astype(bf16), wgu_ref[...]) + bgate_ref[...]
    g = _log_sigmoid(z) / TAU
    return q, k, v, r, g


def _gla_post(o, r, x, ng, wo_ref, lg, lb):
    parts = []
    for h in range(H):
        vs = slice(h * DVH, (h + 1) * DVH)
        oh = o[:, vs]
        ms = jnp.mean(oh * oh, axis=-1, keepdims=True)
        parts.append(oh * lax.rsqrt(ms + RMS_EPS) * ng[:, vs])
    on = jnp.concatenate(parts, axis=1)
    gated = on * (r * jax.nn.sigmoid(r))
    y = _dot(gated.astype(bf16), wo_ref[...])
    return _layer_norm(ALPHA * x + y, lg, lb)


PROJ_TILE = 512
W_VPU, W_NORM, W_OUT, W_MXU = 1.0, 1.0, 1.0, 1.0


def _interleave(main_tasks, filler_tasks):
    total = sum(w for _, w in main_tasks)
    n_fill = len(filler_tasks)
    done, acc = 0, 0.0
    for task, w in main_tasks:
        task()
        acc += w
        want = int(round(acc * n_fill / total))
        while done < want:
            filler_tasks[done]()
            done += 1


def _gla_stage1_tasks(x_ref, slot, w_refs, scr):
    wq_ref, wgl_ref, wgu_ref, bgate_ref = w_refs
    xb_scr, q_scr, k_scr, v_scr, r_scr, g_scr = scr[:6]

    def cast_x():
        xb_scr[slot] = x_ref[...].astype(bf16)

    def proj(dst, c0, w0, scale):
        def run():
            acc = _dot(xb_scr[slot], wq_ref[:, w0:w0 + PROJ_TILE])
            dst[slot, :, c0:c0 + PROJ_TILE] = acc * scale if scale is not None else acc
        return run

    def gate():
        gl = _dot(xb_scr[slot], wgl_ref[...])
        z = _dot(gl.astype(bf16), wgu_ref[...]) + bgate_ref[...]
        g_scr[slot] = _log_sigmoid(z) / TAU

    tiles = []
    for dst, w_base, width, scale in ((q_scr, 0, DK, Q_SCALE), (k_scr, DK, DK, None),
                                      (v_scr, 2 * DK, DV, None), (r_scr, 2 * DK + DV, DV, None)):
        for c0 in range(0, width, PROJ_TILE):
            tiles.append(proj(dst, c0, w_base + c0, scale))
    return cast_x, gate, tiles


def _split3(a):
    a1 = a.astype(bf16)
    r1 = a - a1.astype(f32)
    a2 = r1.astype(bf16)
    a3 = (r1 - a2.astype(f32)).astype(bf16)
    return a1, a2, a3


def _gla_stage2_tasks(xres_ref, y_ref, r0, slot, scr, s_scr, ng_ref, wo_ref, lg_ref, lb_ref, n_chunks):
    _, q_scr, k_scr, v_scr, r_scr, g_scr, b_scr, o_scr, gated_scr = scr
    tb = n_chunks * KCHUNK
    mid = KCHUNK // 2

    def cumsum():
        row = lax.broadcasted_iota(jnp.int32, (tb, tb), 0)
        col = lax.broadcasted_iota(jnp.int32, (tb, tb), 1)
        tri = ((row >= col) & ((row // KCHUNK) == (col // KCHUNK))).astype(bf16)
        g1, g2, g3 = _split3(g_scr[slot])
        b_scr[...] = _dot(tri, g1) + _dot(tri, g2) + _dot(tri, g3)

    def chunk_prep(c, cell):
        rs = slice(c * KCHUNK, (c + 1) * KCHUNK)

        def run():
            b = b_scr[rs, :]
            b_mid = b[mid - 1:mid, :]
            b_last = b[KCHUNK - 1:KCHUNK, :]
            qc = q_scr[slot, rs, :]
            kc = k_scr[slot, rs, :]
            cell["qs"] = (qc * jnp.exp(b - b_mid)).astype(bf16)
            cell["ks"] = (kc * jnp.exp(b_mid - b)).astype(bf16)
            cell["qd"] = (qc * jnp.exp(b)).astype(bf16)
            cell["kk"] = (kc * jnp.exp(b_last - b)).astype(bf16)
            cell["vb"] = v_scr[slot, rs, :].astype(bf16)
            cell["dec_t"] = jnp.exp(jnp.broadcast_to(b_last, (KCHUNK, DK)).T)
        return run

    def head_scores(h, cell):
        ks = slice(h * DKH, (h + 1) * DKH)

        def run():
            ri = lax.broadcasted_iota(jnp.int32, (KCHUNK, KCHUNK), 0)
            ci = lax.broadcasted_iota(jnp.int32, (KCHUNK, KCHUNK), 1)
            sc = _dg(cell["qs"][:, ks], cell["ks"][:, ks], _NT)
            cell["sc", h] = jnp.where(ri >= ci, sc, 0.0).astype(bf16)
        return run

    def head_update(c, h, cell):
        rs = slice(c * KCHUNK, (c + 1) * KCHUNK)
        ks = slice(h * DKH, (h + 1) * DKH)
        vs = slice(h * DVH, (h + 1) * DVH)

        def run():
            st = s_scr[h]
            vh = cell["vb"][:, vs]
            lhs = jnp.concatenate([cell["sc", h], cell["qd"][:, ks]], axis=1)
            rhs = jnp.concatenate([vh, st.astype(bf16)], axis=0)
            o_scr[rs, vs] = _dot(lhs, rhs)
            dec_h = cell["dec_t"][ks, :]
            dec_m = jnp.concatenate([dec_h] * (DVH // KCHUNK), axis=1)
            s_scr[h] = dec_m * st + _dg(cell["kk"][:, ks], vh, _TN)
        return run

    def norm_gate(h):
        vs = slice(h * DVH, (h + 1) * DVH)

        def run():
            oh = o_scr[:, vs]
            rh = r_scr[slot, :, vs]
            ms = jnp.mean(oh * oh, axis=-1, keepdims=True)
            on = oh * lax.rsqrt(ms + RMS_EPS) * ng_ref[:, vs]
            gated_scr[:, vs] = (on * (rh * jax.nn.sigmoid(rh))).astype(bf16)
        return run

    def out_rows(m0, m1):
        def run():
            y = _dot(gated_scr[m0:m1, :], wo_ref[...])
            x = xres_ref[r0 + m0:r0 + m1, :]
            y_ref[r0 + m0:r0 + m1, :] = _layer_norm(ALPHA * x + y, lg_ref[...], lb_ref[...])
        return run

    tasks = [(cumsum, W_VPU)]
    for c in range(n_chunks):
        cell = {}
        tasks.append((chunk_prep(c, cell), W_VPU))
        tasks += [(head_scores(h, cell), W_MXU) for h in range(H)]
        tasks += [(head_update(c, h, cell), W_MXU) for h in range(H)]
    tasks += [(norm_gate(h), W_NORM) for h in range(H)]
    half = tb // 2
    tasks += [(out_rows(0, half), W_OUT), (out_rows(half, tb), W_OUT)]
    return tasks


def _gla_prompt_kernel(x0_ref, xa_ref, xb_ref, xres_ref, wq_ref, wgl_ref, wgu_ref, bgate_ref, ng_ref, wo_ref,
                       lg_ref, lb_ref, y_ref, st_ref, xb_scr, q_scr, k_scr, v_scr, r_scr, g_scr, b_scr, o_scr,
                       gated_scr, s_scr, *, n_chunks, steps_per_seq):
    i = pl.program_id(0)
    tb = n_chunks * KCHUNK
    w_refs = (wq_ref, wgl_ref, wgu_ref, bgate_ref)
    scr = (xb_scr, q_scr, k_scr, v_scr, r_scr, g_scr, b_scr, o_scr, gated_scr)
    post = (ng_ref, wo_ref, lg_ref, lb_ref)

    @pl.when(i == 0)
    def _():
        cast_x, gate, tiles = _gla_stage1_tasks(x0_ref, 0, w_refs, scr)
        for task in [cast_x, gate] + tiles:
            task()

    @pl.when(i % steps_per_seq == 0)
    def _():
        s_scr[...] = jnp.zeros_like(s_scr)

    for r0, slot, x_next in ((0, 0, xa_ref), (tb, 1, xb_ref)):
        cast_x, gate, tiles = _gla_stage1_tasks(x_next, 1 - slot, w_refs, scr)
        stage2 = _gla_stage2_tasks(xres_ref, y_ref, r0, slot, scr, s_scr, *post, n_chunks)
        cast_x()
        _interleave([stage2[0], (gate, W_VPU)] + stage2[1:], tiles)

    @pl.when(i % steps_per_seq == steps_per_seq - 1)
    def _():
        for h in range(H):
            st_ref[0, h] = s_scr[h]


def _gla_sample_kernel(x_ref, st_ref, wq_ref, wgl_ref, wgu_ref, bgate_ref, ng_ref, wo_ref, lg_ref, lb_ref,
                       y_ref, sto_ref, *, n_seq, t_valid):
    rows = n_seq * SUB
    x = x_ref[...]
    q, k, v, r, g = _gla_project(x.astype(bf16), wq_ref, wgl_ref, wgu_ref, bgate_ref)

    row = lax.broadcasted_iota(jnp.int32, (rows, rows), 0)
    col = lax.broadcasted_iota(jnp.int32, (rows, rows), 1)
    same_seq = (row // SUB) == (col // SUB)
    tri = (same_seq & (row >= col)).astype(f32)
    tri_last = (same_seq & ((col % SUB) < t_valid)).astype(f32)
    b = jnp.dot(tri, g, precision=lax.Precision.HIGHEST, preferred_element_type=f32)
    b_last = jnp.dot(tri_last, g, precision=lax.Precision.HIGHEST, preferred_element_type=f32)

    valid = (lax.broadcasted_iota(jnp.int32, (rows, DK), 0) % SUB) < t_valid
    qd = (q * jnp.exp(b)).astype(bf16)
    kd = jnp.where(valid, k * jnp.exp(-b), 0.0).astype(bf16)
    kk = jnp.where(valid, k * jnp.exp(b_last - b), 0.0).astype(bf16)
    vb = v.astype(bf16)
    dec_t = jnp.exp(b_last.T)

    r8 = lax.broadcasted_iota(jnp.int32, (SUB, SUB), 0)
    c8 = lax.broadcasted_iota(jnp.int32, (SUB, SUB), 1)
    causal = r8 >= c8

    o_seqs = []
    for n in range(n_seq):
        rs = slice(n * SUB, (n + 1) * SUB)
        o_heads = []
        for h in range(H):
            ks = slice(h * DKH, (h + 1) * DKH)
            vs = slice(h * DVH, (h + 1) * DVH)
            s0 = st_ref[n, h]
            sc = jnp.where(causal, _dg(qd[rs, ks], kd[rs, ks], _NT), 0.0).astype(bf16)
            o_heads.append(_dot(sc, vb[rs, vs]) + _dot(qd[rs, ks], s0.astype(bf16)))
            dcol = dec_t[ks, n * SUB:n * SUB + 1]
            sto_ref[n, h] = dcol * s0 + _dg(kk[rs, ks], vb[rs, vs], _TN)
        o_seqs.append(jnp.concatenate(o_heads, axis=1))
    o = jnp.concatenate(o_seqs, axis=0)

    y_ref[...] = _gla_post(o, r, x, ng_ref[...], wo_ref, lg_ref[...], lb_ref[...])


def _conv_prompt_kernel(x_ref, win_ref, wc_ref, wout_ref, lg_ref, lb_ref, y_ref, st_ref,
                        xb_scr, bg_scr, cg_scr, u_scr, gated_scr, c_scr, *, n_sub, sub_rows):
    @pl.when(pl.program_id(1) == 0)
    def _():
        c_scr[...] = jnp.zeros_like(c_scr)

    def sub_tasks(s):
        rs = slice(s * sub_rows, (s + 1) * sub_rows)
        buf = s % 2

        def cast():
            xb_scr[buf] = x_ref[0, rs, :].astype(bf16)

        def proj_bg():
            bg_scr[buf] = _dot(xb_scr[buf], win_ref[:, 0:D])

        def proj_cg():
            cg_scr[...] = _dot(xb_scr[buf], win_ref[:, D:2 * D])

        def proj_u():
            u_scr[buf] = cg_scr[...] * _dot(xb_scr[buf], win_ref[:, 2 * D:3 * D])

        def conv():
            u = u_scr[buf]
            t = lax.broadcasted_iota(jnp.int32, (sub_rows, D), 0)
            c0 = c_scr[0:1, :]
            c1 = c_scr[1:2, :]
            p1 = jnp.where(t == 0, c1, pltpu.roll(u, 1, 0))
            p2 = jnp.where(t == 0, c0, jnp.where(t == 1, c1, pltpu.roll(u, 2, 0)))
            cv = p2 * wc_ref[0:1, :] + p1 * wc_ref[1:2, :] + u * wc_ref[2:3, :]
            gated_scr[buf] = (bg_scr[buf] * cv).astype(bf16)
            c_scr[0:2, :] = u[sub_rows - 2:sub_rows, :]

        def out():
            y = _dot(gated_scr[buf], wout_ref[...])
            y_ref[0, rs, :] = _layer_norm(ALPHA * x_ref[0, rs, :] + y, lg_ref[...], lb_ref[...])

        return (cast, proj_bg, proj_cg, proj_u), (conv, out)

    pending = None
    for s in range(n_sub):
        (cast, proj_bg, proj_cg, proj_u), tail = sub_tasks(s)
        cast()
        proj_bg()
        if pending is not None:
            pending[0]()
        proj_cg()
        if pending is not None:
            pending[1]()
        proj_u()
        pending = tail
    pending[0]()
    pending[1]()
    st_ref[0] = c_scr[0:2, :]


def _conv_sample_kernel(x_ref, e1_ref, e2_ref, win_ref, wc_ref, wout_ref, lg_ref, lb_ref, y_ref, u_ref, *, t_seq):
    x = x_ref[...]
    rows = x.shape[0]
    xb = x.astype(bf16)
    bg = _dot(xb, win_ref[:, 0:D])
    cg = _dot(xb, win_ref[:, D:2 * D])
    hh = _dot(xb, win_ref[:, 2 * D:3 * D])
    u = cg * hh
    t = lax.broadcasted_iota(jnp.int32, (rows, D), 0) % t_seq
    p1 = jnp.where(t >= 1, pltpu.roll(u, 1, 0), e1_ref[...])
    p2 = jnp.where(t >= 2, pltpu.roll(u, 2, 0), e2_ref[...])
    conv = p2 * wc_ref[0:1, :] + p1 * wc_ref[1:2, :] + u * wc_ref[2:3, :]
    y = _dot((bg * conv).astype(bf16), wout_ref[...])
    y_ref[...] = _layer_norm(ALPHA * x + y, lg_ref[...], lb_ref[...])
    u_ref[...] = u


def _mlp_kernel(x_ref, wup_ref, wdn_ref, lg_ref, lb_ref, y_ref, xb_scr, h_scr, *, n_sub, sub_rows, ff_chunk):
    def sub_tasks(s):
        rs = slice(s * sub_rows, (s + 1) * sub_rows)
        buf = s % 2

        def cast():
            xb_scr[buf] = x_ref[rs, :].astype(bf16)

        def up(c):
            cs = slice(c * ff_chunk, (c + 1) * ff_chunk)

            def run():
                hcol = _dot(xb_scr[buf], wup_ref[0, :, cs])
                h_scr[buf, :, cs] = jnp.square(jnp.maximum(hcol, 0.0)).astype(bf16)
            return run

        def down():
            y = _dot(h_scr[buf], wdn_ref[0])
            y_ref[rs, :] = _layer_norm(ALPHA * x_ref[rs, :] + y, lg_ref[0], lb_ref[0])

        return [cast] + [up(c) for c in range(D_FF // ff_chunk)], down

    pending = None
    for s in range(n_sub):
        ups, down = sub_tasks(s)
        ups[0]()
        ups[1]()
        if pending is not None:
            pending()
        for task in ups[2:]:
            task()
        pending = down
    pending()


def _const_spec(shape):
    nd = len(shape)
    return pl.BlockSpec(shape, lambda *_: (0,) * nd, pipeline_mode=pl.Buffered(1))


def _layer_spec(shape, layer):
    nd = len(shape)
    return pl.BlockSpec((1,) + tuple(shape[1:]), lambda *_: (layer,) + (0,) * (nd - 1),
                        pipeline_mode=pl.Buffered(1))


def _mlp(x2d, wup, wdn, lg, lb, layer, n_sub, sub_rows):
    m = x2d.shape[0]
    tm = n_sub * sub_rows
    return pl.pallas_call(
        functools.partial(_mlp_kernel, n_sub=n_sub, sub_rows=sub_rows, ff_chunk=1024),
        grid=(m // tm,),
        in_specs=[pl.BlockSpec((tm, D), lambda i: (i, 0)),
                  _layer_spec(wup.shape, layer), _layer_spec(wdn.shape, layer),
                  _layer_spec(lg.shape, layer), _layer_spec(lb.shape, layer)],
        out_specs=pl.BlockSpec((tm, D), lambda i: (i, 0)),
        out_shape=jax.ShapeDtypeStruct((m, D), f32),
        scratch_shapes=[pltpu.VMEM((2, sub_rows, D), bf16), pltpu.VMEM((2, sub_rows, D_FF), bf16)],
        compiler_params=pltpu.CompilerParams(dimension_semantics=("arbitrary",), vmem_limit_bytes=VMEM_LIMIT),
        name="mlp",
    )(x2d, wup, wdn, lg, lb)


def _gla_prompt(x, wq, wgl, wgu, bgate, ng, wo, lg, lb, tb):
    bsz, t, _ = x.shape
    n_chunks = tb // KCHUNK
    n_blocks = bsz * t // tb
    steps_per_seq = t // (2 * tb)
    x2d = x.reshape(bsz * t, D)
    y, st = pl.pallas_call(
        functools.partial(_gla_prompt_kernel, n_chunks=n_chunks, steps_per_seq=steps_per_seq),
        grid=(n_blocks // 2,),
        in_specs=[pl.BlockSpec((tb, D), lambda i: (0, 0), pipeline_mode=pl.Buffered(1)),
                  pl.BlockSpec((tb, D), lambda i: (2 * i + 1, 0)),
                  pl.BlockSpec((tb, D), lambda i: (jnp.minimum(2 * i + 2, n_blocks - 1), 0)),
                  pl.BlockSpec((2 * tb, D), lambda i: (i, 0)),
                  _const_spec(wq.shape), _const_spec(wgl.shape), _const_spec(wgu.shape),
                  _const_spec(bgate.shape), _const_spec(ng.shape), _const_spec(wo.shape),
                  _const_spec(lg.shape), _const_spec(lb.shape)],
        out_specs=[pl.BlockSpec((2 * tb, D), lambda i: (i, 0)),
                   pl.BlockSpec((1, H, DKH, DVH), lambda i: (i // steps_per_seq, 0, 0, 0))],
        out_shape=[jax.ShapeDtypeStruct((bsz * t, D), f32),
                   jax.ShapeDtypeStruct((bsz, H, DKH, DVH), f32)],
        scratch_shapes=[pltpu.VMEM((2, tb, D), bf16),
                        pltpu.VMEM((2, tb, DK), f32), pltpu.VMEM((2, tb, DK), f32),
                        pltpu.VMEM((2, tb, DV), f32), pltpu.VMEM((2, tb, DV), f32),
                        pltpu.VMEM((2, tb, DK), f32),
                        pltpu.VMEM((tb, DK), f32), pltpu.VMEM((tb, DV), f32), pltpu.VMEM((tb, DV), bf16),
                        pltpu.VMEM((H, DKH, DVH), f32)],
        compiler_params=pltpu.CompilerParams(dimension_semantics=("arbitrary",),
                                             vmem_limit_bytes=VMEM_LIMIT),
        name="gla_prompt",
    )(x2d, x2d, x2d, x2d, wq, wgl, wgu, bgate, ng, wo, lg, lb)
    return y.reshape(bsz, t, D), st


def _gla_sample(x_pad, state, wq, wgl, wgu, bgate, ng, wo, lg, lb, n_seq, t_valid):
    s_total = state.shape[0]
    rows = n_seq * SUB
    return pl.pallas_call(
        functools.partial(_gla_sample_kernel, n_seq=n_seq, t_valid=t_valid),
        grid=(s_total // n_seq,),
        in_specs=[pl.BlockSpec((rows, D), lambda i: (i, 0)),
                  pl.BlockSpec((n_seq, H, DKH, DVH), lambda i: (i, 0, 0, 0)),
                  _const_spec(wq.shape), _const_spec(wgl.shape), _const_spec(wgu.shape),
                  _const_spec(bgate.shape), _const_spec(ng.shape), _const_spec(wo.shape),
                  _const_spec(lg.shape), _const_spec(lb.shape)],
        out_specs=[pl.BlockSpec((rows, D), lambda i: (i, 0)),
                   pl.BlockSpec((n_seq, H, DKH, DVH), lambda i: (i, 0, 0, 0))],
        out_shape=[jax.ShapeDtypeStruct((s_total * SUB, D), f32),
                   jax.ShapeDtypeStruct(state.shape, f32)],
        compiler_params=pltpu.CompilerParams(dimension_semantics=("arbitrary",), vmem_limit_bytes=VMEM_LIMIT),
        name="gla_sample",
    )(x_pad, state, wq, wgl, wgu, bgate, ng, wo, lg, lb)


def _conv_prompt(x, win, wc, wout, lg, lb, n_sub, sub_rows):
    bsz, t, _ = x.shape
    tb = n_sub * sub_rows
    return pl.pallas_call(
        functools.partial(_conv_prompt_kernel, n_sub=n_sub, sub_rows=sub_rows),
        grid=(bsz, t // tb),
        in_specs=[pl.BlockSpec((1, tb, D), lambda b, j: (b, j, 0)),
                  _const_spec(win.shape), _const_spec(wc.shape), _const_spec(wout.shape),
                  _const_spec(lg.shape), _const_spec(lb.shape)],
        out_specs=[pl.BlockSpec((1, tb, D), lambda b, j: (b, j, 0)),
                   pl.BlockSpec((1, 2, D), lambda b, j: (b, 0, 0))],
        out_shape=[jax.ShapeDtypeStruct((bsz, t, D), f32),
                   jax.ShapeDtypeStruct((bsz, 2, D), f32)],
        scratch_shapes=[pltpu.VMEM((2, sub_rows, D), bf16), pltpu.VMEM((2, sub_rows, D), f32),
                        pltpu.VMEM((sub_rows, D), f32), pltpu.VMEM((2, sub_rows, D), f32),
                        pltpu.VMEM((2, sub_rows, D), bf16), pltpu.VMEM((SUB, D), f32)],
        compiler_params=pltpu.CompilerParams(dimension_semantics=("arbitrary", "arbitrary"),
                                             vmem_limit_bytes=VMEM_LIMIT),
        name="conv_prompt",
    )(x, win, wc, wout, lg, lb)


def _conv_sample(x2d, e1, e2, win, wc, wout, lg, lb, t_seq):
    m = x2d.shape[0]
    full = lambda shape: pl.BlockSpec(shape, lambda i: (0,) * len(shape))
    return pl.pallas_call(
        functools.partial(_conv_sample_kernel, t_seq=t_seq),
        grid=(1,),
        in_specs=[full((m, D)), full((m, D)), full((m, D)),
                  _const_spec(win.shape), _const_spec(wc.shape), _const_spec(wout.shape),
                  _const_spec(lg.shape), _const_spec(lb.shape)],
        out_specs=[full((m, D)), full((m, D))],
        out_shape=[jax.ShapeDtypeStruct((m, D), f32), jax.ShapeDtypeStruct((m, D), f32)],
        compiler_params=pltpu.CompilerParams(dimension_semantics=("arbitrary",), vmem_limit_bytes=VMEM_LIMIT),
        name="conv_sample",
    )(x2d, e1, e2, win, wc, wout, lg, lb)


def kernel(x_prompt, x_sample, state_gla, state_conv, gla_w_in, gla_w_gate_up, gla_b_gate, gla_norm_g, gla_w_o, conv_w_in, conv_w_conv, conv_w_out, mlp_w_up, mlp_w_down, ln1_g, ln1_b, ln2_g, ln2_b):
    bsz, t, _ = x_prompt.shape
    n_dec, t_dec, _ = x_sample.shape
    assert t % CHUNK == 0 and t_dec <= SUB and t_dec >= 2

    w_in = gla_w_in[0]
    wq = w_in.astype(bf16)
    wgl = jnp.pad(w_in[:, 2 * DK + 2 * DV:], ((0, 0), (0, RANK_PAD - RANK))).astype(bf16)
    wgu = jnp.pad(gla_w_gate_up[0], ((0, RANK_PAD - RANK), (0, 0))).astype(bf16)
    bgate = gla_b_gate[0].reshape(1, DK)
    ng = gla_norm_g[0].reshape(1, DV)
    wo = gla_w_o[0].astype(bf16)
    cwin = conv_w_in[0].astype(bf16)
    cwc = conv_w_conv[0]
    cwout = conv_w_out[0].astype(bf16)
    wup = mlp_w_up.astype(bf16)
    wdn = mlp_w_down.astype(bf16)
    ln2g = ln2_g.reshape(DEPTH, 1, D)
    ln2b = ln2_b.reshape(DEPTH, 1, D)
    row = lambda a, i: a[i].reshape(1, D)
    mlp_prompt = functools.partial(_mlp, n_sub=4, sub_rows=256)
    mlp_sample = functools.partial(_mlp, n_sub=2, sub_rows=n_dec * t_dec // 2)

    xp, gla_p = _gla_prompt(x_prompt, wq, wgl, wgu, bgate, ng, wo, row(ln1_g, 0), row(ln1_b, 0), tb=256)
    xs_pad = jnp.pad(x_sample, ((0, 0), (0, SUB - t_dec), (0, 0))).reshape(n_dec * SUB, D)
    xs_pad, gla_s = _gla_sample(xs_pad, state_gla[0], wq, wgl, wgu, bgate, ng, wo, row(ln1_g, 0), row(ln1_b, 0),
                                n_seq=16, t_valid=t_dec)
    xs = xs_pad.reshape(n_dec, SUB, D)[:, :t_dec].reshape(n_dec * t_dec, D)
    xp = mlp_prompt(xp.reshape(bsz * t, D), wup, wdn, ln2g, ln2b, layer=0)
    xs = mlp_sample(xs, wup, wdn, ln2g, ln2b, layer=0)

    xp, conv_p = _conv_prompt(xp.reshape(bsz, t, D), cwin, cwc, cwout, row(ln1_g, 1), row(ln1_b, 1),
                              n_sub=4, sub_rows=256)
    buf = state_conv[0]
    e1 = jnp.pad(buf[:, 1:2], ((0, 0), (0, t_dec - 1), (0, 0))).reshape(n_dec * t_dec, D)
    e2 = jnp.pad(buf, ((0, 0), (0, t_dec - 2), (0, 0))).reshape(n_dec * t_dec, D)
    xs, u_s = _conv_sample(xs, e1, e2, cwin, cwc, cwout, row(ln1_g, 1), row(ln1_b, 1), t_seq=t_dec)
    conv_s = u_s.reshape(n_dec, t_dec, D)[:, t_dec - 2:]
    xp = mlp_prompt(xp.reshape(bsz * t, D), wup, wdn, ln2g, ln2b, layer=1)
    xs = mlp_sample(xs, wup, wdn, ln2g, ln2b, layer=1)

    return (xp.reshape(bsz, t, D), xs.reshape(n_dec, t_dec, D), gla_p[None], gla_s[None],
            conv_p[None], conv_s[None])
```

```python
import functools

import jax
import jax.numpy as jnp
from jax import lax
from jax.experimental import pallas as pl
from jax.experimental.pallas import tpu as pltpu

bf16 = jnp.bfloat16
f32 = jnp.float32

D = 1024
H = 4
DK = 512
DV = 1024
DKH = DK // H
DVH = DV // H
RANK = 16
RANK_PAD = 128
TAU = 16.0
CHUNK = 64
KCHUNK = 128
D_FF = 4 * D
DEPTH = 2
ALPHA = (2 * DEPTH) ** 0.25
LN_EPS = 1e-5
RMS_EPS = 1e-6
Q_SCALE = DKH ** -0.5

SUB = 8
VMEM_LIMIT = 56 * 1024 * 1024

_NT = (((1,), (1,)), ((), ()))
_TN = (((0,), (0,)), ((), ()))


def _dot(a, b):
    return jnp.dot(a, b, preferred_element_type=f32)


def _dg(a, b, dims):
    return lax.dot_general(a, b, dims, preferred_element_type=f32)


def _layer_norm(y, g, b):
    mu = jnp.mean(y, axis=-1, keepdims=True)
    yc = y - mu
    var = jnp.mean(yc * yc, axis=-1, keepdims=True)
    return yc * lax.rsqrt(var + LN_EPS) * g + b


def _log_sigmoid(z):
    return -(jnp.maximum(-z, 0.0) + jnp.log(1.0 + jnp.exp(-jnp.abs(z))))


def _gla_project(xb, wq_ref, wgl_ref, wgu_ref, bgate_ref):
    q = _dot(xb, wq_ref[:, 0:DK]) * Q_SCALE
    k = _dot(xb, wq_ref[:, DK:2 * DK])
    v = _dot(xb, wq_ref[:, 2 * DK:2 * DK + DV])
    r = _dot(xb, wq_ref[:, 2 * DK + DV:2 * DK + 2 * DV])
    gl = _dot(xb, wgl_ref[...])
    z = _dot(gl.astype(bf16), wgu_ref[...]) + bgate_ref[...]
    g = _log_sigmoid(z) / TAU
    return q, k, v, r, g


def _gla_post(o, r, x, ng, wo_ref, lg, lb):
    parts = []
    for h in range(H):
        vs = slice(h * DVH, (h + 1) * DVH)
        oh = o[:, vs]
        ms = jnp.mean(oh * oh, axis=-1, keepdims=True)
        parts.append(oh * lax.rsqrt(ms + RMS_EPS) * ng[:, vs])
    on = jnp.concatenate(parts, axis=1)
    gated = on * (r * jax.nn.sigmoid(r))
    y = _dot(gated.astype(bf16), wo_ref[...])
    return _layer_norm(ALPHA * x + y, lg, lb)


PROJ_TILE = 512
W_VPU, W_NORM, W_OUT, W_MXU = 1.0, 1.0, 1.0, 1.0


def _interleave(main_tasks, filler_tasks):
    total = sum(w for _, w in main_tasks)
    n_fill = len(filler_tasks)
    done, acc = 0, 0.0
    for task, w in main_tasks:
        task()
        acc += w
        want = int(round(acc * n_fill / total))
        while done < want:
            filler_tasks[done]()
            done += 1


def _gla_stage1_tasks(x_ref, slot, w_refs, scr):
    wq_ref, wgl_ref, wgu_ref, bgate_ref = w_refs
    xb_scr, q_scr, k_scr, v_scr, r_scr, g_scr = scr[:6]

    def cast_x():
        xb_scr[slot] = x_ref[...].astype(bf16)

    def proj(dst, c0, w0, scale):
        def run():
            acc = _dot(xb_scr[slot], wq_ref[:, w0:w0 + PROJ_TILE])
            dst[slot, :, c0:c0 + PROJ_TILE] = acc * scale if scale is not None else acc
        return run

    def gate():
        gl = _dot(xb_scr[slot], wgl_ref[...])
        z = _dot(gl.astype(bf16), wgu_ref[...]) + bgate_ref[...]
        g_scr[slot] = _log_sigmoid(z) / TAU

    tiles = []
    for dst, w_base, width, scale in ((q_scr, 0, DK, Q_SCALE), (k_scr, DK, DK, None),
                                      (v_scr, 2 * DK, DV, None), (r_scr, 2 * DK + DV, DV, None)):
        for c0 in range(0, width, PROJ_TILE):
            tiles.append(proj(dst, c0, w_base + c0, scale))
    return cast_x, gate, tiles


def _split3(a):
    a1 = a.astype(bf16)
    r1 = a - a1.astype(f32)
    a2 = r1.astype(bf16)
    a3 = (r1 - a2.astype(f32)).astype(bf16)
    return a1, a2, a3


def _gla_stage2_tasks(xres_ref, y_ref, r0, slot, scr, s_scr, ng_ref, wo_ref, lg_ref, lb_ref, n_chunks):
    _, q_scr, k_scr, v_scr, r_scr, g_scr, b_scr, o_scr, gated_scr = scr
    tb = n_chunks * KCHUNK
    mid = KCHUNK // 2

    def cumsum():
        row = lax.broadcasted_iota(jnp.int32, (tb, tb), 0)
        col = lax.broadcasted_iota(jnp.int32, (tb, tb), 1)
        tri = ((row >= col) & ((row // KCHUNK) == (col // KCHUNK))).astype(bf16)
        g1, g2, g3 = _split3(g_scr[slot])
        b_scr[...] = _dot(tri, g1) + _dot(tri, g2) + _dot(tri, g3)

    def chunk_prep(c, cell):
        rs = slice(c * KCHUNK, (c + 1) * KCHUNK)

        def run():
            b = b_scr[rs, :]
            b_mid = b[mid - 1:mid, :]
            b_last = b[KCHUNK - 1:KCHUNK, :]
            qc = q_scr[slot, rs, :]
            kc = k_scr[slot, rs, :]
            cell["qs"] = (qc * jnp.exp(b - b_mid)).astype(bf16)
            cell["ks"] = (kc * jnp.exp(b_mid - b)).astype(bf16)
            cell["qd"] = (qc * jnp.exp(b)).astype(bf16)
            cell["kk"] = (kc * jnp.exp(b_last - b)).astype(bf16)
            cell["vb"] = v_scr[slot, rs, :].astype(bf16)
            cell["dec_t"] = jnp.exp(jnp.broadcast_to(b_last, (KCHUNK, DK)).T)
        return run

    def head_scores(h, cell):
        ks = slice(h * DKH, (h + 1) * DKH)

        def run():
            ri = lax.broadcasted_iota(jnp.int32, (KCHUNK, KCHUNK), 0)
            ci = lax.broadcasted_iota(jnp.int32, (KCHUNK, KCHUNK), 1)
            sc = _dg(cell["qs"][:, ks], cell["ks"][:, ks], _NT)
            cell["sc", h] = jnp.where(ri >= ci, sc, 0.0).astype(bf16)
        return run

    def head_update(c, h, cell):
        rs = slice(c * KCHUNK, (c + 1) * KCHUNK)
        ks = slice(h * DKH, (h + 1) * DKH)
        vs = slice(h * DVH, (h + 1) * DVH)

        def run():
            st = s_scr[h]
            vh = cell["vb"][:, vs]
            lhs = jnp.concatenate([cell["sc", h], cell["qd"][:, ks]], axis=1)
            rhs = jnp.concatenate([vh, st.astype(bf16)], axis=0)
            o_scr[rs, vs] = _dot(lhs, rhs)
            dec_h = cell["dec_t"][ks, :]
            dec_m = jnp.concatenate([dec_h] * (DVH // KCHUNK), axis=1)
            s_scr[h] = dec_m * st + _dg(cell["kk"][:, ks], vh, _TN)
        return run

    def norm_gate(h):
        vs = slice(h * DVH, (h + 1) * DVH)

        def run():
            oh = o_scr[:, vs]
            rh = r_scr[slot, :, vs]
            ms = jnp.mean(oh * oh, axis=-1, keepdims=True)
            on = oh * lax.rsqrt(ms + RMS_EPS) * ng_ref[:, vs]
            gated_scr[:, vs] = (on * (rh * jax.nn.sigmoid(rh))).astype(bf16)
        return run

    def out_rows(m0, m1):
        def run():
            y = _dot(gated_scr[m0:m1, :], wo_ref[...])
            x = xres_ref[r0 + m0:r0 + m1, :]
            y_ref[r0 + m0:r0 + m1, :] = _layer_norm(ALPHA * x + y, lg_ref[...], lb_ref[...])
        return run

    tasks = [(cumsum, W_VPU)]
    for c in range(n_chunks):
        cell = {}
        tasks.append((chunk_prep(c, cell), W_VPU))
        tasks += [(head_scores(h, cell), W_MXU) for h in range(H)]
        tasks += [(head_update(c, h, cell), W_MXU) for h in range(H)]
    tasks += [(norm_gate(h), W_NORM) for h in range(H)]
    half = tb // 2
    tasks += [(out_rows(0, half), W_OUT), (out_rows(half, tb), W_OUT)]
    return tasks


def _gla_prompt_kernel(x0_ref, xa_ref, xb_ref, xres_ref, wq_ref, wgl_ref, wgu_ref, bgate_ref, ng_ref, wo_ref,
                       lg_ref, lb_ref, *refs, n_chunks, steps_per_seq, n_side):
    side_in = refs[:n_side]
    y_ref, st_ref = refs[n_side:n_side + 2]
    side_out = refs[n_side + 2:2 * n_side + 2]
    (xb_scr, q_scr, k_scr, v_scr, r_scr, g_scr, b_scr, o_scr, gated_scr, s_scr) = refs[2 * n_side + 2:]
    i = pl.program_id(0)
    tb = n_chunks * KCHUNK
    w_refs = (wq_ref, wgl_ref, wgu_ref, bgate_ref)
    scr = (xb_scr, q_scr, k_scr, v_scr, r_scr, g_scr, b_scr, o_scr, gated_scr)
    post = (ng_ref, wo_ref, lg_ref, lb_ref)

    @pl.when(i == 0)
    def _():
        cast_x, gate, tiles = _gla_stage1_tasks(x0_ref, 0, w_refs, scr)
        for task in [cast_x, gate] + tiles:
            task()

    @pl.when(i % steps_per_seq == 0)
    def _():
        s_scr[...] = jnp.zeros_like(s_scr)

    def side_cast(src_ref, dst_ref):
        def run():
            dst_ref[...] = src_ref[...].astype(bf16)
        return run

    side_tasks = [side_cast(s, d) for s, d in zip(side_in, side_out)]
    for r0, slot, x_next in ((0, 0, xa_ref), (tb, 1, xb_ref)):
        cast_x, gate, tiles = _gla_stage1_tasks(x_next, 1 - slot, w_refs, scr)
        stage2 = _gla_stage2_tasks(xres_ref, y_ref, r0, slot, scr, s_scr, *post, n_chunks)
        cast_x()
        fillers = tiles + side_tasks[slot::2]
        _interleave([stage2[0], (gate, W_VPU)] + stage2[1:], fillers)

    @pl.when(i % steps_per_seq == steps_per_seq - 1)
    def _():
        for h in range(H):
            st_ref[0, h] = s_scr[h]


def _gla_sample_kernel(x_ref, st_ref, wq_ref, wgl_ref, wgu_ref, bgate_ref, ng_ref, wo_ref, lg_ref, lb_ref,
                       y_ref, sto_ref, *, n_seq, t_valid):
    rows = n_seq * SUB
    x = x_ref[...]
    q, k, v, r, g = _gla_project(x.astype(bf16), wq_ref, wgl_ref, wgu_ref, bgate_ref)

    row = lax.broadcasted_iota(jnp.int32, (rows, rows), 0)
    col = lax.broadcasted_iota(jnp.int32, (rows, rows), 1)
    same_seq = (row // SUB) == (col // SUB)
    tri = (same_seq & (row >= col)).astype(f32)
    tri_last = (same_seq & ((col % SUB) < t_valid)).astype(f32)
    b = jnp.dot(tri, g, precision=lax.Precision.HIGHEST, preferred_element_type=f32)
    b_last = jnp.dot(tri_last, g, precision=lax.Precision.HIGHEST, preferred_element_type=f32)

    valid = (lax.broadcasted_iota(jnp.int32, (rows, DK), 0) % SUB) < t_valid
    qd = (q * jnp.exp(b)).astype(bf16)
    kd = jnp.where(valid, k * jnp.exp(-b), 0.0).astype(bf16)
    kk = jnp.where(valid, k * jnp.exp(b_last - b), 0.0).astype(bf16)
    vb = v.astype(bf16)
    dec_t = jnp.exp(b_last.T)

    r8 = lax.broadcasted_iota(jnp.int32, (SUB, SUB), 0)
    c8 = lax.broadcasted_iota(jnp.int32, (SUB, SUB), 1)
    causal = r8 >= c8

    o_seqs = []
    for n in range(n_seq):
        rs = slice(n * SUB, (n + 1) * SUB)
        o_heads = []
        for h in range(H):
            ks = slice(h * DKH, (h + 1) * DKH)
            vs = slice(h * DVH, (h + 1) * DVH)
            s0 = st_ref[n, h]
            sc = jnp.where(causal, _dg(qd[rs, ks], kd[rs, ks], _NT), 0.0).astype(bf16)
            o_heads.append(_dot(sc, vb[rs, vs]) + _dot(qd[rs, ks], s0.astype(bf16)))
            dcol = dec_t[ks, n * SUB:n * SUB + 1]
            sto_ref[n, h] = dcol * s0 + _dg(kk[rs, ks], vb[rs, vs], _TN)
        o_seqs.append(jnp.concatenate(o_heads, axis=1))
    o = jnp.concatenate(o_seqs, axis=0)

    y_ref[...] = _gla_post(o, r, x, ng_ref[...], wo_ref, lg_ref[...], lb_ref[...])


def _conv_prompt_kernel(x_ref, win_ref, wc_ref, wout_ref, lg_ref, lb_ref, y_ref, st_ref,
                        xb_scr, bg_scr, cg_scr, u_scr, gated_scr, c_scr, *, n_sub, sub_rows):
    @pl.when(pl.program_id(1) == 0)
    def _():
        c_scr[...] = jnp.zeros_like(c_scr)

    def sub_tasks(s):
        rs = slice(s * sub_rows, (s + 1) * sub_rows)
        buf = s % 2

        def cast():
            xb_scr[buf] = x_ref[0, rs, :].astype(bf16)

        def proj_bg():
            bg_scr[buf] = _dot(xb_scr[buf], win_ref[:, 0:D])

        def proj_cg():
            cg_scr[...] = _dot(xb_scr[buf], win_ref[:, D:2 * D])

        def proj_u():
            u_scr[buf] = cg_scr[...] * _dot(xb_scr[buf], win_ref[:, 2 * D:3 * D])

        def conv():
            u = u_scr[buf]
            t = lax.broadcasted_iota(jnp.int32, (sub_rows, D), 0)
            c0 = c_scr[0:1, :]
            c1 = c_scr[1:2, :]
            p1 = jnp.where(t == 0, c1, pltpu.roll(u, 1, 0))
            p2 = jnp.where(t == 0, c0, jnp.where(t == 1, c1, pltpu.roll(u, 2, 0)))
            cv = p2 * wc_ref[0:1, :] + p1 * wc_ref[1:2, :] + u * wc_ref[2:3, :]
            gated_scr[buf] = (bg_scr[buf] * cv).astype(bf16)
            c_scr[0:2, :] = u[sub_rows - 2:sub_rows, :]

        def out():
            y = _dot(gated_scr[buf], wout_ref[...])
            y_ref[0, rs, :] = _layer_norm(ALPHA * x_ref[0, rs, :] + y, lg_ref[...], lb_ref[...])

        return (cast, proj_bg, proj_cg, proj_u), (conv, out)

    pending = None
    for s in range(n_sub):
        (cast, proj_bg, proj_cg, proj_u), tail = sub_tasks(s)
        cast()
        proj_bg()
        if pending is not None:
            pending[0]()
        proj_cg()
        if pending is not None:
            pending[1]()
        proj_u()
        pending = tail
    pending[0]()
    pending[1]()
    st_ref[0] = c_scr[0:2, :]


def _conv_sample_kernel(x_ref, e1_ref, e2_ref, win_ref, wc_ref, wout_ref, lg_ref, lb_ref, y_ref, u_ref, *, t_seq):
    x = x_ref[...]
    rows = x.shape[0]
    xb = x.astype(bf16)
    bg = _dot(xb, win_ref[:, 0:D])
    cg = _dot(xb, win_ref[:, D:2 * D])
    hh = _dot(xb, win_ref[:, 2 * D:3 * D])
    u = cg * hh
    t = lax.broadcasted_iota(jnp.int32, (rows, D), 0) % t_seq
    p1 = jnp.where(t >= 1, pltpu.roll(u, 1, 0), e1_ref[...])
    p2 = jnp.where(t >= 2, pltpu.roll(u, 2, 0), e2_ref[...])
    conv = p2 * wc_ref[0:1, :] + p1 * wc_ref[1:2, :] + u * wc_ref[2:3, :]
    y = _dot((bg * conv).astype(bf16), wout_ref[...])
    y_ref[...] = _layer_norm(ALPHA * x + y, lg_ref[...], lb_ref[...])
    u_ref[...] = u


def _mlp_kernel(x_ref, wup_ref, wdn_ref, lg_ref, lb_ref, y_ref, xb_scr, h_scr, *, n_sub, sub_rows, ff_chunk):
    def sub_tasks(s):
        rs = slice(s * sub_rows, (s + 1) * sub_rows)
        buf = s % 2

        def cast():
            xb_scr[buf] = x_ref[rs, :].astype(bf16)

        def up(c):
            cs = slice(c * ff_chunk, (c + 1) * ff_chunk)

            def run():
                hcol = _dot(xb_scr[buf], wup_ref[0, :, cs])
                h_scr[buf, :, cs] = jnp.square(jnp.maximum(hcol, 0.0)).astype(bf16)
            return run

        def down():
            y = _dot(h_scr[buf], wdn_ref[0])
            y_ref[rs, :] = _layer_norm(ALPHA * x_ref[rs, :] + y, lg_ref[0], lb_ref[0])

        return [cast] + [up(c) for c in range(D_FF // ff_chunk)], down

    pending = None
    for s in range(n_sub):
        ups, down = sub_tasks(s)
        ups[0]()
        ups[1]()
        if pending is not None:
            pending()
        for task in ups[2:]:
            task()
        pending = down
    pending()


def _const_spec(shape):
    nd = len(shape)
    return pl.BlockSpec(shape, lambda *_: (0,) * nd, pipeline_mode=pl.Buffered(1))


def _layer_spec(shape, layer):
    nd = len(shape)
    return pl.BlockSpec((1,) + tuple(shape[1:]), lambda *_: (layer,) + (0,) * (nd - 1),
                        pipeline_mode=pl.Buffered(1))


def _mlp(x2d, wup, wdn, lg, lb, layer, n_sub, sub_rows):
    m = x2d.shape[0]
    tm = n_sub * sub_rows
    return pl.pallas_call(
        functools.partial(_mlp_kernel, n_sub=n_sub, sub_rows=sub_rows, ff_chunk=1024),
        grid=(m // tm,),
        in_specs=[pl.BlockSpec((tm, D), lambda i: (i, 0)),
                  _layer_spec(wup.shape, layer), _layer_spec(wdn.shape, layer),
                  _layer_spec(lg.shape, layer), _layer_spec(lb.shape, layer)],
        out_specs=pl.BlockSpec((tm, D), lambda i: (i, 0)),
        out_shape=jax.ShapeDtypeStruct((m, D), f32),
        scratch_shapes=[pltpu.VMEM((2, sub_rows, D), bf16), pltpu.VMEM((2, sub_rows, D_FF), bf16)],
        compiler_params=pltpu.CompilerParams(dimension_semantics=("arbitrary",), vmem_limit_bytes=VMEM_LIMIT),
        name="mlp",
    )(x2d, wup, wdn, lg, lb)


def _gla_prompt(x, wq, wgl, wgu, bgate, ng, wo, lg, lb, tb, side):
    bsz, t, _ = x.shape
    n_chunks = tb // KCHUNK
    n_blocks = bsz * t // tb
    n_steps = n_blocks // 2
    steps_per_seq = t // (2 * tb)
    x2d = x.reshape(bsz * t, D)
    side_specs = [pl.BlockSpec((a.shape[0] // n_steps, a.shape[1]), lambda i: (i, 0)) for a in side]
    outs = pl.pallas_call(
        functools.partial(_gla_prompt_kernel, n_chunks=n_chunks, steps_per_seq=steps_per_seq, n_side=len(side)),
        grid=(n_steps,),
        in_specs=[pl.BlockSpec((tb, D), lambda i: (0, 0), pipeline_mode=pl.Buffered(1)),
                  pl.BlockSpec((tb, D), lambda i: (2 * i + 1, 0)),
                  pl.BlockSpec((tb, D), lambda i: (jnp.minimum(2 * i + 2, n_blocks - 1), 0)),
                  pl.BlockSpec((2 * tb, D), lambda i: (i, 0)),
                  _const_spec(wq.shape), _const_spec(wgl.shape), _const_spec(wgu.shape),
                  _const_spec(bgate.shape), _const_spec(ng.shape), _const_spec(wo.shape),
                  _const_spec(lg.shape), _const_spec(lb.shape)] + side_specs,
        out_specs=[pl.BlockSpec((2 * tb, D), lambda i: (i, 0)),
                   pl.BlockSpec((1, H, DKH, DVH), lambda i: (i // steps_per_seq, 0, 0, 0))] + side_specs,
        out_shape=[jax.ShapeDtypeStruct((bsz * t, D), f32),
                   jax.ShapeDtypeStruct((bsz, H, DKH, DVH), f32)]
                  + [jax.ShapeDtypeStruct(a.shape, bf16) for a in side],
        scratch_shapes=[pltpu.VMEM((2, tb, D), bf16),
                        pltpu.VMEM((2, tb, DK), f32), pltpu.VMEM((2, tb, DK), f32),
                        pltpu.VMEM((2, tb, DV), f32), pltpu.VMEM((2, tb, DV), f32),
                        pltpu.VMEM((2, tb, DK), f32),
                        pltpu.VMEM((tb, DK), f32), pltpu.VMEM((tb, DV), f32), pltpu.VMEM((tb, DV), bf16),
                        pltpu.VMEM((H, DKH, DVH), f32)],
        compiler_params=pltpu.CompilerParams(dimension_semantics=("arbitrary",),
                                             vmem_limit_bytes=VMEM_LIMIT),
        name="gla_prompt",
    )(x2d, x2d, x2d, x2d, wq, wgl, wgu, bgate, ng, wo, lg, lb, *side)
    return outs[0].reshape(bsz, t, D), outs[1], outs[2:]


def _gla_sample(x_pad, state, wq, wgl, wgu, bgate, ng, wo, lg, lb, n_seq, t_valid):
    s_total = state.shape[0]
    rows = n_seq * SUB
    return pl.pallas_call(
        functools.partial(_gla_sample_kernel, n_seq=n_seq, t_valid=t_valid),
        grid=(s_total // n_seq,),
        in_specs=[pl.BlockSpec((rows, D), lambda i: (i, 0)),
                  pl.BlockSpec((n_seq, H, DKH, DVH), lambda i: (i, 0, 0, 0)),
                  _const_spec(wq.shape), _const_spec(wgl.shape), _const_spec(wgu.shape),
                  _const_spec(bgate.shape), _const_spec(ng.shape), _const_spec(wo.shape),
                  _const_spec(lg.shape), _const_spec(lb.shape)],
        out_specs=[pl.BlockSpec((rows, D), lambda i: (i, 0)),
                   pl.BlockSpec((n_seq, H, DKH, DVH), lambda i: (i, 0, 0, 0))],
        out_shape=[jax.ShapeDtypeStruct((s_total * SUB, D), f32),
                   jax.ShapeDtypeStruct(state.shape, f32)],
        compiler_params=pltpu.CompilerParams(dimension_semantics=("arbitrary",), vmem_limit_bytes=VMEM_LIMIT),
        name="gla_sample",
    )(x_pad, state, wq, wgl, wgu, bgate, ng, wo, lg, lb)


def _conv_prompt(x, win, wc, wout, lg, lb, n_sub, sub_rows):
    bsz, t, _ = x.shape
    tb = n_sub * sub_rows
    return pl.pallas_call(
        functools.partial(_conv_prompt_kernel, n_sub=n_sub, sub_rows=sub_rows),
        grid=(bsz, t // tb),
        in_specs=[pl.BlockSpec((1, tb, D), lambda b, j: (b, j, 0)),
                  _const_spec(win.shape), _const_spec(wc.shape), _const_spec(wout.shape),
                  _const_spec(lg.shape), _const_spec(lb.shape)],
        out_specs=[pl.BlockSpec((1, tb, D), lambda b, j: (b, j, 0)),
                   pl.BlockSpec((1, 2, D), lambda b, j: (b, 0, 0))],
        out_shape=[jax.ShapeDtypeStruct((bsz, t, D), f32),
                   jax.ShapeDtypeStruct((bsz, 2, D), f32)],
        scratch_shapes=[pltpu.VMEM((2, sub_rows, D), bf16), pltpu.VMEM((2, sub_rows, D), f32),
                        pltpu.VMEM((sub_rows, D), f32), pltpu.VMEM((2, sub_rows, D), f32),
                        pltpu.VMEM((2, sub_rows, D), bf16), pltpu.VMEM((SUB, D), f32)],
        compiler_params=pltpu.CompilerParams(dimension_semantics=("arbitrary", "arbitrary"),
                                             vmem_limit_bytes=VMEM_LIMIT),
        name="conv_prompt",
    )(x, win, wc, wout, lg, lb)


def _conv_sample(x2d, e1, e2, win, wc, wout, lg, lb, t_seq):
    m = x2d.shape[0]
    full = lambda shape: pl.BlockSpec(shape, lambda i: (0,) * len(shape))
    return pl.pallas_call(
        functools.partial(_conv_sample_kernel, t_seq=t_seq),
        grid=(1,),
        in_specs=[full((m, D)), full((m, D)), full((m, D)),
                  _const_spec(win.shape), _const_spec(wc.shape), _const_spec(wout.shape),
                  _const_spec(lg.shape), _const_spec(lb.shape)],
        out_specs=[full((m, D)), full((m, D))],
        out_shape=[jax.ShapeDtypeStruct((m, D), f32), jax.ShapeDtypeStruct((m, D), f32)],
        compiler_params=pltpu.CompilerParams(dimension_semantics=("arbitrary",), vmem_limit_bytes=VMEM_LIMIT),
        name="conv_sample",
    )(x2d, e1, e2, win, wc, wout, lg, lb)


def kernel(x_prompt, x_sample, state_gla, state_conv, gla_w_in, gla_w_gate_up, gla_b_gate, gla_norm_g, gla_w_o, conv_w_in, conv_w_conv, conv_w_out, mlp_w_up, mlp_w_down, ln1_g, ln1_b, ln2_g, ln2_b):
    bsz, t, _ = x_prompt.shape
    n_dec, t_dec, _ = x_sample.shape
    assert t % CHUNK == 0 and t_dec <= SUB and t_dec >= 2

    w_in = gla_w_in[0]
    wq = w_in.astype(bf16)
    wgl = jnp.pad(w_in[:, 2 * DK + 2 * DV:], ((0, 0), (0, RANK_PAD - RANK))).astype(bf16)
    wgu = jnp.pad(gla_w_gate_up[0], ((0, RANK_PAD - RANK), (0, 0))).astype(bf16)
    bgate = gla_b_gate[0].reshape(1, DK)
    ng = gla_norm_g[0].reshape(1, DV)
    wo = gla_w_o[0].astype(bf16)
    cwc = conv_w_conv[0]
    ln2g = ln2_g.reshape(DEPTH, 1, D)
    ln2b = ln2_b.reshape(DEPTH, 1, D)
    row = lambda a, i: a[i].reshape(1, D)
    mlp_prompt = functools.partial(_mlp, n_sub=4, sub_rows=256)
    mlp_sample = functools.partial(_mlp, n_sub=2, sub_rows=n_dec * t_dec // 2)

    side = (mlp_w_up.reshape(DEPTH * D, D_FF), mlp_w_down.reshape(DEPTH * D_FF, D), conv_w_in[0], conv_w_out[0])
    xp, gla_p, (wup, wdn, cwin, cwout) = _gla_prompt(x_prompt, wq, wgl, wgu, bgate, ng, wo, row(ln1_g, 0),
                                                     row(ln1_b, 0), tb=256, side=side)
    wup = wup.reshape(DEPTH, D, D_FF)
    wdn = wdn.reshape(DEPTH, D_FF, D)
    xs_pad = jnp.pad(x_sample, ((0, 0), (0, SUB - t_dec), (0, 0))).reshape(n_dec * SUB, D)
    xs_pad, gla_s = _gla_sample(xs_pad, state_gla[0], wq, wgl, wgu, bgate, ng, wo, row(ln1_g, 0), row(ln1_b, 0),
                                n_seq=16, t_valid=t_dec)
    xs = xs_pad.reshape(n_dec, SUB, D)[:, :t_dec].reshape(n_dec * t_dec, D)
    xp = mlp_prompt(xp.reshape(bsz * t, D), wup, wdn, ln2g, ln2b, layer=0)
    xs = mlp_sample(xs, wup, wdn, ln2g, ln2b, layer=0)

    xp, conv_p = _conv_prompt(xp.reshape(bsz, t, D), cwin, cwc, cwout, row(ln1_g, 1), row(ln1_b, 1),
                              n_sub=4, sub_rows=256)
    buf = state_conv[0]
    e1 = jnp.pad(buf[:, 1:2], ((0, 0), (0, t_dec - 1), (0, 0))).reshape(n_dec * t_dec, D)
    e2 = jnp.pad(buf, ((0, 0), (0, t_dec - 2), (0, 0))).reshape(n_dec * t_dec, D)
    xs, u_s = _conv_sample(xs, e1, e2, cwin, cwc, cwout, row(ln1_g, 1), row(ln1_b, 1), t_seq=t_dec)
    conv_s = u_s.reshape(n_dec, t_dec, D)[:, t_dec - 2:]
    xp = mlp_prompt(xp.reshape(bsz * t, D), wup, wdn, ln2g, ln2b, layer=1)
    xs = mlp_sample(xs, wup, wdn, ln2g, ln2b, layer=1)

    return (xp.reshape(bsz, t, D), xs.reshape(n_dec, t_dec, D), gla_p[None], gla_s[None],
            conv_p[None], conv_s[None])
```

```python
import functools

import jax
import jax.numpy as jnp
from jax import lax
from jax.experimental import pallas as pl
from jax.experimental.pallas import tpu as pltpu

bf16 = jnp.bfloat16
f32 = jnp.float32

D = 1024
H = 4
DK = 512
DV = 1024
DKH = DK // H
DVH = DV // H
RANK = 16
RANK_PAD = 128
TAU = 16.0
CHUNK = 64
KCHUNK = 128
D_FF = 4 * D
DEPTH = 2
ALPHA = (2 * DEPTH) ** 0.25
LN_EPS = 1e-5
RMS_EPS = 1e-6
Q_SCALE = DKH ** -0.5

SUB = 8
VMEM_LIMIT = 56 * 1024 * 1024

_NT = (((1,), (1,)), ((), ()))
_TN = (((0,), (0,)), ((), ()))


def _dot(a, b):
    return jnp.dot(a, b, preferred_element_type=f32)


def _dg(a, b, dims):
    return lax.dot_general(a, b, dims, preferred_element_type=f32)


def _layer_norm(y, g, b):
    mu = jnp.mean(y, axis=-1, keepdims=True)
    yc = y - mu
    var = jnp.mean(yc * yc, axis=-1, keepdims=True)
    return yc * lax.rsqrt(var + LN_EPS) * g + b


def _log_sigmoid(z):
    return -(jnp.maximum(-z, 0.0) + jnp.log(1.0 + jnp.exp(-jnp.abs(z))))


def _gla_project(xb, wq_ref, wgl_ref, wgu_ref, bgate_ref):
    q = _dot(xb, wq_ref[:, 0:DK]) * Q_SCALE
    k = _dot(xb, wq_ref[:, DK:2 * DK])
    v = _dot(xb, wq_ref[:, 2 * DK:2 * DK + DV])
    r = _dot(xb, wq_ref[:, 2 * DK + DV:2 * DK + 2 * DV])
    gl = _dot(xb, wgl_ref[...])
    z = _dot(gl.astype(bf16), wgu_ref[...]) + bgate_ref[...]
    g = _log_sigmoid(z) / TAU
    return q, k, v, r, g


def _gla_post(o, r, x, ng, wo_ref, lg, lb):
    parts = []
    for h in range(H):
        vs = slice(h * DVH, (h + 1) * DVH)
        oh = o[:, vs]
        ms = jnp.mean(oh * oh, axis=-1, keepdims=True)
        parts.append(oh * lax.rsqrt(ms + RMS_EPS) * ng[:, vs])
    on = jnp.concatenate(parts, axis=1)
    gated = on * (r * jax.nn.sigmoid(r))
    y = _dot(gated.astype(bf16), wo_ref[...])
    return _layer_norm(ALPHA * x + y, lg, lb)


PROJ_TILE = 512
W_VPU, W_NORM, W_OUT, W_MXU = 1.0, 1.0, 1.0, 1.0


def _interleave(main_tasks, filler_tasks):
    total = sum(w for _, w in main_tasks)
    n_fill = len(filler_tasks)
    done, acc = 0, 0.0
    for task, w in main_tasks:
        task()
        acc += w
        want = int(round(acc * n_fill / total))
        while done < want:
            filler_tasks[done]()
            done += 1


def _gla_stage1_tasks(x_ref, slot, w_refs, scr):
    wq_ref, wgl_ref, wgu_ref, bgate_ref = w_refs
    xb_scr, q_scr, k_scr, v_scr, r_scr, g_scr = scr[:6]

    def cast_x():
        xb_scr[slot] = x_ref[...].astype(bf16)

    def proj(dst, c0, w0, scale):
        def run():
            acc = _dot(xb_scr[slot], wq_ref[:, w0:w0 + PROJ_TILE])
            dst[slot, :, c0:c0 + PROJ_TILE] = acc * scale if scale is not None else acc
        return run

    def gate():
        gl = _dot(xb_scr[slot], wgl_ref[...])
        z = _dot(gl.astype(bf16), wgu_ref[...]) + bgate_ref[...]
        g_scr[slot] = _log_sigmoid(z) / TAU

    tiles = []
    for dst, w_base, width, scale in ((q_scr, 0, DK, Q_SCALE), (k_scr, DK, DK, None),
                                      (v_scr, 2 * DK, DV, None), (r_scr, 2 * DK + DV, DV, None)):
        for c0 in range(0, width, PROJ_TILE):
            tiles.append(proj(dst, c0, w_base + c0, scale))
    return cast_x, gate, tiles


def _split3(a):
    a1 = a.astype(bf16)
    r1 = a - a1.astype(f32)
    a2 = r1.astype(bf16)
    a3 = (r1 - a2.astype(f32)).astype(bf16)
    return a1, a2, a3


def _gla_stage2_tasks(xres_ref, y_ref, r0, slot, scr, s_scr, ng_ref, wo_ref, lg_ref, lb_ref, n_chunks):
    _, q_scr, k_scr, v_scr, r_scr, g_scr, b_scr, o_scr, gated_scr = scr
    tb = n_chunks * KCHUNK
    mid = KCHUNK // 2

    def cumsum():
        row = lax.broadcasted_iota(jnp.int32, (tb, tb), 0)
        col = lax.broadcasted_iota(jnp.int32, (tb, tb), 1)
        tri = ((row >= col) & ((row // KCHUNK) == (col // KCHUNK))).astype(bf16)
        g1, g2, g3 = _split3(g_scr[slot])
        b_scr[...] = _dot(tri, g1) + _dot(tri, g2) + _dot(tri, g3)

    def chunk_prep(c, cell):
        rs = slice(c * KCHUNK, (c + 1) * KCHUNK)

        def run():
            b = b_scr[rs, :]
            b_mid = b[mid - 1:mid, :]
            b_last = b[KCHUNK - 1:KCHUNK, :]
            qc = q_scr[slot, rs, :]
            kc = k_scr[slot, rs, :]
            cell["qs"] = (qc * jnp.exp(b - b_mid)).astype(bf16)
            cell["ks"] = (kc * jnp.exp(b_mid - b)).astype(bf16)
            cell["qd"] = (qc * jnp.exp(b)).astype(bf16)
            cell["kk"] = (kc * jnp.exp(b_last - b)).astype(bf16)
            cell["vb"] = v_scr[slot, rs, :].astype(bf16)
            cell["dec_t"] = jnp.exp(jnp.broadcast_to(b_last, (KCHUNK, DK)).T)
        return run

    def head_scores(h, cell):
        ks = slice(h * DKH, (h + 1) * DKH)

        def run():
            ri = lax.broadcasted_iota(jnp.int32, (KCHUNK, KCHUNK), 0)
            ci = lax.broadcasted_iota(jnp.int32, (KCHUNK, KCHUNK), 1)
            sc = _dg(cell["qs"][:, ks], cell["ks"][:, ks], _NT)
            cell["sc", h] = jnp.where(ri >= ci, sc, 0.0).astype(bf16)
        return run

    def head_update(c, h, cell):
        rs = slice(c * KCHUNK, (c + 1) * KCHUNK)
        ks = slice(h * DKH, (h + 1) * DKH)
        vs = slice(h * DVH, (h + 1) * DVH)

        def run():
            st = s_scr[h]
            vh = cell["vb"][:, vs]
            lhs = jnp.concatenate([cell["sc", h], cell["qd"][:, ks]], axis=1)
            rhs = jnp.concatenate([vh, st.astype(bf16)], axis=0)
            o_scr[rs, vs] = _dot(lhs, rhs)
            dec_h = cell["dec_t"][ks, :]
            dec_m = jnp.concatenate([dec_h] * (DVH // KCHUNK), axis=1)
            s_scr[h] = dec_m * st + _dg(cell["kk"][:, ks], vh, _TN)
        return run

    def norm_gate(h):
        vs = slice(h * DVH, (h + 1) * DVH)

        def run():
            oh = o_scr[:, vs]
            rh = r_scr[slot, :, vs]
            ms = jnp.mean(oh * oh, axis=-1, keepdims=True)
            on = oh * lax.rsqrt(ms + RMS_EPS) * ng_ref[:, vs]
            gated_scr[:, vs] = (on * (rh * jax.nn.sigmoid(rh))).astype(bf16)
        return run

    def out_rows(m0, m1):
        def run():
            y = _dot(gated_scr[m0:m1, :], wo_ref[...])
            x = xres_ref[r0 + m0:r0 + m1, :]
            y_ref[r0 + m0:r0 + m1, :] = _layer_norm(ALPHA * x + y, lg_ref[...], lb_ref[...])
        return run

    tasks = [(cumsum, W_VPU)]
    for c in range(n_chunks):
        cell = {}
        tasks.append((chunk_prep(c, cell), W_VPU))
        tasks += [(head_scores(h, cell), W_MXU) for h in range(H)]
        tasks += [(head_update(c, h, cell), W_MXU) for h in range(H)]
    tasks += [(norm_gate(h), W_NORM) for h in range(H)]
    half = tb // 2
    tasks += [(out_rows(0, half), W_OUT), (out_rows(half, tb), W_OUT)]
    return tasks


def _gla_prompt_kernel(x0_ref, xa_ref, xb_ref, xres_ref, wq_ref, wgl_ref, wgu_ref, bgate_ref, ng_ref, wo_ref,
                       lg_ref, lb_ref, *refs, n_chunks, steps_per_seq, n_side):
    side_in = refs[:n_side]
    y_ref, st_ref = refs[n_side:n_side + 2]
    side_out = refs[n_side + 2:2 * n_side + 2]
    (xb_scr, q_scr, k_scr, v_scr, r_scr, g_scr, b_scr, o_scr, gated_scr, s_scr) = refs[2 * n_side + 2:]
    i = pl.program_id(0)
    tb = n_chunks * KCHUNK
    w_refs = (wq_ref, wgl_ref, wgu_ref, bgate_ref)
    scr = (xb_scr, q_scr, k_scr, v_scr, r_scr, g_scr, b_scr, o_scr, gated_scr)
    post = (ng_ref, wo_ref, lg_ref, lb_ref)

    @pl.when(i == 0)
    def _():
        cast_x, gate, tiles = _gla_stage1_tasks(x0_ref, 0, w_refs, scr)
        for task in [cast_x, gate] + tiles:
            task()

    @pl.when(i % steps_per_seq == 0)
    def _():
        s_scr[...] = jnp.zeros_like(s_scr)

    def side_cast(src_ref, dst_ref):
        def run():
            dst_ref[...] = src_ref[...].astype(bf16)
        return run

    side_tasks = [side_cast(s, d) for s, d in zip(side_in, side_out)]
    for r0, slot, x_next in ((0, 0, xa_ref), (tb, 1, xb_ref)):
        cast_x, gate, tiles = _gla_stage1_tasks(x_next, 1 - slot, w_refs, scr)
        stage2 = _gla_stage2_tasks(xres_ref, y_ref, r0, slot, scr, s_scr, *post, n_chunks)
        cast_x()
        fillers = tiles + side_tasks[slot::2]
        _interleave([stage2[0], (gate, W_VPU)] + stage2[1:], fillers)

    @pl.when(i % steps_per_seq == steps_per_seq - 1)
    def _():
        for h in range(H):
            st_ref[0, h] = s_scr[h]


def _gla_sample_kernel(x_ref, st_ref, wq_ref, wgl_ref, wgu_ref, bgate_ref, ng_ref, wo_ref, lg_ref, lb_ref,
                       y_ref, sto_ref, o_scr, *, n_seq, t_valid):
    rows = n_seq * SUB
    x = x_ref[...]
    q, k, v, r, g = _gla_project(x.astype(bf16), wq_ref, wgl_ref, wgu_ref, bgate_ref)

    row = lax.broadcasted_iota(jnp.int32, (rows, rows), 0)
    col = lax.broadcasted_iota(jnp.int32, (rows, rows), 1)
    same_seq = (row // SUB) == (col // SUB)
    tri = (same_seq & (row >= col)).astype(f32)
    tri_last = (same_seq & ((col % SUB) < t_valid)).astype(f32)
    b = jnp.dot(tri, g, precision=lax.Precision.HIGHEST, preferred_element_type=f32)
    b_last = jnp.dot(tri_last, g, precision=lax.Precision.HIGHEST, preferred_element_type=f32)

    valid = (lax.broadcasted_iota(jnp.int32, (rows, DK), 0) % SUB) < t_valid
    qd = (q * jnp.exp(b)).astype(bf16)
    kd = jnp.where(valid, k * jnp.exp(-b), 0.0).astype(bf16)
    kk_t = jnp.where(valid, k * jnp.exp(b_last - b), 0.0).T.astype(bf16)
    vb = v.astype(bf16)
    dec_t = jnp.exp(b_last.T)

    intra_mask = same_seq & (row >= col)
    for h in range(H):
        ks = slice(h * DKH, (h + 1) * DKH)
        vs = slice(h * DVH, (h + 1) * DVH)
        sc = jnp.where(intra_mask, _dg(qd[:, ks], kd[:, ks], _NT), 0.0).astype(bf16)
        o_scr[:, vs] = _dot(sc, vb[:, vs])

    seq_of_col = lax.broadcasted_iota(jnp.int32, (DKH, rows), 1) // SUB

    def seq_dots(n):
        rs = slice(n * SUB, (n + 1) * SUB)
        res = []
        for h in range(H):
            ks = slice(h * DKH, (h + 1) * DKH)
            vs = slice(h * DVH, (h + 1) * DVH)
            s0 = st_ref[n, h]
            o_inter = _dot(qd[rs, ks], s0.astype(bf16))
            upd = _dot(jnp.where(seq_of_col == n, kk_t[ks, :], jnp.zeros((), bf16)), vb[:, vs])
            res.append((s0, o_inter, upd))
        return res

    def seq_combine(n, res):
        rs = slice(n * SUB, (n + 1) * SUB)
        for h, (s0, o_inter, upd) in enumerate(res):
            ks = slice(h * DKH, (h + 1) * DKH)
            vs = slice(h * DVH, (h + 1) * DVH)
            o_scr[rs, vs] = o_scr[rs, vs] + o_inter
            sto_ref[n, h] = dec_t[ks, n * SUB:n * SUB + 1] * s0 + upd

    pending = None
    for n in range(n_seq):
        res = seq_dots(n)
        if pending is not None:
            seq_combine(*pending)
        pending = (n, res)
    seq_combine(*pending)

    y_ref[...] = _gla_post(o_scr[...], r, x, ng_ref[...], wo_ref, lg_ref[...], lb_ref[...])


def _conv_prompt_kernel(x_ref, win_ref, wc_ref, wout_ref, lg_ref, lb_ref, y_ref, st_ref,
                        xb_scr, bg_scr, cg_scr, u_scr, gated_scr, c_scr, *, n_sub, sub_rows):
    @pl.when(pl.program_id(1) == 0)
    def _():
        c_scr[...] = jnp.zeros_like(c_scr)

    def sub_tasks(s):
        rs = slice(s * sub_rows, (s + 1) * sub_rows)
        buf = s % 2

        def cast():
            xb_scr[buf] = x_ref[0, rs, :].astype(bf16)

        def proj_bg():
            bg_scr[buf] = _dot(xb_scr[buf], win_ref[:, 0:D])

        def proj_cg():
            cg_scr[...] = _dot(xb_scr[buf], win_ref[:, D:2 * D])

        def proj_u():
            u_scr[buf] = cg_scr[...] * _dot(xb_scr[buf], win_ref[:, 2 * D:3 * D])

        def conv():
            u = u_scr[buf]
            t = lax.broadcasted_iota(jnp.int32, (sub_rows, D), 0)
            c0 = c_scr[0:1, :]
            c1 = c_scr[1:2, :]
            p1 = jnp.where(t == 0, c1, pltpu.roll(u, 1, 0))
            p2 = jnp.where(t == 0, c0, jnp.where(t == 1, c1, pltpu.roll(u, 2, 0)))
            cv = p2 * wc_ref[0:1, :] + p1 * wc_ref[1:2, :] + u * wc_ref[2:3, :]
            gated_scr[buf] = (bg_scr[buf] * cv).astype(bf16)
            c_scr[0:2, :] = u[sub_rows - 2:sub_rows, :]

        def out():
            y = _dot(gated_scr[buf], wout_ref[...])
            y_ref[0, rs, :] = _layer_norm(ALPHA * x_ref[0, rs, :] + y, lg_ref[...], lb_ref[...])

        return (cast, proj_bg, proj_cg, proj_u), (conv, out)

    pending = None
    for s in range(n_sub):
        (cast, proj_bg, proj_cg, proj_u), tail = sub_tasks(s)
        cast()
        proj_bg()
        if pending is not None:
            pending[0]()
        proj_cg()
        if pending is not None:
            pending[1]()
        proj_u()
        pending = tail
    pending[0]()
    pending[1]()
    st_ref[0] = c_scr[0:2, :]


def _conv_sample_kernel(x_ref, e1_ref, e2_ref, win_ref, wc_ref, wout_ref, lg_ref, lb_ref, y_ref, u_ref, *, t_seq):
    x = x_ref[...]
    rows = x.shape[0]
    xb = x.astype(bf16)
    bg = _dot(xb, win_ref[:, 0:D])
    cg = _dot(xb, win_ref[:, D:2 * D])
    hh = _dot(xb, win_ref[:, 2 * D:3 * D])
    u = cg * hh
    t = lax.broadcasted_iota(jnp.int32, (rows, D), 0) % t_seq
    p1 = jnp.where(t >= 1, pltpu.roll(u, 1, 0), e1_ref[...])
    p2 = jnp.where(t >= 2, pltpu.roll(u, 2, 0), e2_ref[...])
    conv = p2 * wc_ref[0:1, :] + p1 * wc_ref[1:2, :] + u * wc_ref[2:3, :]
    y = _dot((bg * conv).astype(bf16), wout_ref[...])
    y_ref[...] = _layer_norm(ALPHA * x + y, lg_ref[...], lb_ref[...])
    u_ref[...] = u


def _mlp_kernel(x_ref, wup_ref, wdn_ref, lg_ref, lb_ref, y_ref, xb_scr, h_scr, *, n_sub, sub_rows, ff_chunk):
    def sub_tasks(s):
        rs = slice(s * sub_rows, (s + 1) * sub_rows)
        buf = s % 2

        def cast():
            xb_scr[buf] = x_ref[rs, :].astype(bf16)

        def up(c):
            cs = slice(c * ff_chunk, (c + 1) * ff_chunk)

            def run():
                hcol = _dot(xb_scr[buf], wup_ref[0, :, cs])
                h_scr[buf, :, cs] = jnp.square(jnp.maximum(hcol, 0.0)).astype(bf16)
            return run

        def down():
            y = _dot(h_scr[buf], wdn_ref[0])
            y_ref[rs, :] = _layer_norm(ALPHA * x_ref[rs, :] + y, lg_ref[0], lb_ref[0])

        return [cast] + [up(c) for c in range(D_FF // ff_chunk)], down

    pending = None
    for s in range(n_sub):
        ups, down = sub_tasks(s)
        ups[0]()
        ups[1]()
        if pending is not None:
            pending()
        for task in ups[2:]:
            task()
        pending = down
    pending()


def _const_spec(shape):
    nd = len(shape)
    return pl.BlockSpec(shape, lambda *_: (0,) * nd, pipeline_mode=pl.Buffered(1))


def _layer_spec(shape, layer):
    nd = len(shape)
    return pl.BlockSpec((1,) + tuple(shape[1:]), lambda *_: (layer,) + (0,) * (nd - 1),
                        pipeline_mode=pl.Buffered(1))


def _mlp(x2d, wup, wdn, lg, lb, layer, n_sub, sub_rows):
    m = x2d.shape[0]
    tm = n_sub * sub_rows
    return pl.pallas_call(
        functools.partial(_mlp_kernel, n_sub=n_sub, sub_rows=sub_rows, ff_chunk=1024),
        grid=(m // tm,),
        in_specs=[pl.BlockSpec((tm, D), lambda i: (i, 0)),
                  _layer_spec(wup.shape, layer), _layer_spec(wdn.shape, layer),
                  _layer_spec(lg.shape, layer), _layer_spec(lb.shape, layer)],
        out_specs=pl.BlockSpec((tm, D), lambda i: (i, 0)),
        out_shape=jax.ShapeDtypeStruct((m, D), f32),
        scratch_shapes=[pltpu.VMEM((2, sub_rows, D), bf16), pltpu.VMEM((2, sub_rows, D_FF), bf16)],
        compiler_params=pltpu.CompilerParams(dimension_semantics=("arbitrary",), vmem_limit_bytes=VMEM_LIMIT),
        name="mlp",
    )(x2d, wup, wdn, lg, lb)


def _gla_prompt(x, wq, wgl, wgu, bgate, ng, wo, lg, lb, tb, side):
    bsz, t, _ = x.shape
    n_chunks = tb // KCHUNK
    n_blocks = bsz * t // tb
    n_steps = n_blocks // 2
    steps_per_seq = t // (2 * tb)
    x2d = x.reshape(bsz * t, D)
    side_specs = [pl.BlockSpec((a.shape[0] // n_steps, a.shape[1]), lambda i: (i, 0)) for a in side]
    outs = pl.pallas_call(
        functools.partial(_gla_prompt_kernel, n_chunks=n_chunks, steps_per_seq=steps_per_seq, n_side=len(side)),
        grid=(n_steps,),
        in_specs=[pl.BlockSpec((tb, D), lambda i: (0, 0), pipeline_mode=pl.Buffered(1)),
                  pl.BlockSpec((tb, D), lambda i: (2 * i + 1, 0)),
                  pl.BlockSpec((tb, D), lambda i: (jnp.minimum(2 * i + 2, n_blocks - 1), 0)),
                  pl.BlockSpec((2 * tb, D), lambda i: (i, 0)),
                  _const_spec(wq.shape), _const_spec(wgl.shape), _const_spec(wgu.shape),
                  _const_spec(bgate.shape), _const_spec(ng.shape), _const_spec(wo.shape),
                  _const_spec(lg.shape), _const_spec(lb.shape)] + side_specs,
        out_specs=[pl.BlockSpec((2 * tb, D), lambda i: (i, 0)),
                   pl.BlockSpec((1, H, DKH, DVH), lambda i: (i // steps_per_seq, 0, 0, 0))] + side_specs,
        out_shape=[jax.ShapeDtypeStruct((bsz * t, D), f32),
                   jax.ShapeDtypeStruct((bsz, H, DKH, DVH), f32)]
                  + [jax.ShapeDtypeStruct(a.shape, bf16) for a in side],
        scratch_shapes=[pltpu.VMEM((2, tb, D), bf16),
                        pltpu.VMEM((2, tb, DK), f32), pltpu.VMEM((2, tb, DK), f32),
                        pltpu.VMEM((2, tb, DV), f32), pltpu.VMEM((2, tb, DV), f32),
                        pltpu.VMEM((2, tb, DK), f32),
                        pltpu.VMEM((tb, DK), f32), pltpu.VMEM((tb, DV), f32), pltpu.VMEM((tb, DV), bf16),
                        pltpu.VMEM((H, DKH, DVH), f32)],
        compiler_params=pltpu.CompilerParams(dimension_semantics=("arbitrary",),
                                             vmem_limit_bytes=VMEM_LIMIT),
        name="gla_prompt",
    )(x2d, x2d, x2d, x2d, wq, wgl, wgu, bgate, ng, wo, lg, lb, *side)
    return outs[0].reshape(bsz, t, D), outs[1], outs[2:]


def _gla_sample(x_pad, state, wq, wgl, wgu, bgate, ng, wo, lg, lb, n_seq, t_valid):
    s_total = state.shape[0]
    rows = n_seq * SUB
    return pl.pallas_call(
        functools.partial(_gla_sample_kernel, n_seq=n_seq, t_valid=t_valid),
        grid=(s_total // n_seq,),
        in_specs=[pl.BlockSpec((rows, D), lambda i: (i, 0)),
                  pl.BlockSpec((n_seq, H, DKH, DVH), lambda i: (i, 0, 0, 0)),
                  _const_spec(wq.shape), _const_spec(wgl.shape), _const_spec(wgu.shape),
                  _const_spec(bgate.shape), _const_spec(ng.shape), _const_spec(wo.shape),
                  _const_spec(lg.shape), _const_spec(lb.shape)],
        out_specs=[pl.BlockSpec((rows, D), lambda i: (i, 0)),
                   pl.BlockSpec((n_seq, H, DKH, DVH), lambda i: (i, 0, 0, 0))],
        out_shape=[jax.ShapeDtypeStruct((s_total * SUB, D), f32),
                   jax.ShapeDtypeStruct(state.shape, f32)],
        scratch_shapes=[pltpu.VMEM((rows, DV), f32)],
        compiler_params=pltpu.CompilerParams(dimension_semantics=("arbitrary",), vmem_limit_bytes=VMEM_LIMIT),
        name="gla_sample",
    )(x_pad, state, wq, wgl, wgu, bgate, ng, wo, lg, lb)


def _conv_prompt(x, win, wc, wout, lg, lb, n_sub, sub_rows):
    bsz, t, _ = x.shape
    tb = n_sub * sub_rows
    return pl.pallas_call(
        functools.partial(_conv_prompt_kernel, n_sub=n_sub, sub_rows=sub_rows),
        grid=(bsz, t // tb),
        in_specs=[pl.BlockSpec((1, tb, D), lambda b, j: (b, j, 0)),
                  _const_spec(win.shape), _const_spec(wc.shape), _const_spec(wout.shape),
                  _const_spec(lg.shape), _const_spec(lb.shape)],
        out_specs=[pl.BlockSpec((1, tb, D), lambda b, j: (b, j, 0)),
                   pl.BlockSpec((1, 2, D), lambda b, j: (b, 0, 0))],
        out_shape=[jax.ShapeDtypeStruct((bsz, t, D), f32),
                   jax.ShapeDtypeStruct((bsz, 2, D), f32)],
        scratch_shapes=[pltpu.VMEM((2, sub_rows, D), bf16), pltpu.VMEM((2, sub_rows, D), f32),
                        pltpu.VMEM((sub_rows, D), f32), pltpu.VMEM((2, sub_rows, D), f32),
                        pltpu.VMEM((2, sub_rows, D), bf16), pltpu.VMEM((SUB, D), f32)],
        compiler_params=pltpu.CompilerParams(dimension_semantics=("arbitrary", "arbitrary"),
                                             vmem_limit_bytes=VMEM_LIMIT),
        name="conv_prompt",
    )(x, win, wc, wout, lg, lb)


def _conv_sample(x2d, e1, e2, win, wc, wout, lg, lb, t_seq):
    m = x2d.shape[0]
    full = lambda shape: pl.BlockSpec(shape, lambda i: (0,) * len(shape))
    return pl.pallas_call(
        functools.partial(_conv_sample_kernel, t_seq=t_seq),
        grid=(1,),
        in_specs=[full((m, D)), full((m, D)), full((m, D)),
                  _const_spec(win.shape), _const_spec(wc.shape), _const_spec(wout.shape),
                  _const_spec(lg.shape), _const_spec(lb.shape)],
        out_specs=[full((m, D)), full((m, D))],
        out_shape=[jax.ShapeDtypeStruct((m, D), f32), jax.ShapeDtypeStruct((m, D), f32)],
        compiler_params=pltpu.CompilerParams(dimension_semantics=("arbitrary",), vmem_limit_bytes=VMEM_LIMIT),
        name="conv_sample",
    )(x2d, e1, e2, win, wc, wout, lg, lb)


def kernel(x_prompt, x_sample, state_gla, state_conv, gla_w_in, gla_w_gate_up, gla_b_gate, gla_norm_g, gla_w_o, conv_w_in, conv_w_conv, conv_w_out, mlp_w_up, mlp_w_down, ln1_g, ln1_b, ln2_g, ln2_b):
    bsz, t, _ = x_prompt.shape
    n_dec, t_dec, _ = x_sample.shape
    assert t % CHUNK == 0 and t_dec <= SUB and t_dec >= 2

    w_in = gla_w_in[0]
    wq = w_in.astype(bf16)
    wgl = jnp.pad(w_in[:, 2 * DK + 2 * DV:], ((0, 0), (0, RANK_PAD - RANK))).astype(bf16)
    wgu = jnp.pad(gla_w_gate_up[0], ((0, RANK_PAD - RANK), (0, 0))).astype(bf16)
    bgate = gla_b_gate[0].reshape(1, DK)
    ng = gla_norm_g[0].reshape(1, DV)
    wo = gla_w_o[0].astype(bf16)
    cwc = conv_w_conv[0]
    ln2g = ln2_g.reshape(DEPTH, 1, D)
    ln2b = ln2_b.reshape(DEPTH, 1, D)
    row = lambda a, i: a[i].reshape(1, D)
    mlp_prompt = functools.partial(_mlp, n_sub=4, sub_rows=256)
    mlp_sample = functools.partial(_mlp, n_sub=2, sub_rows=n_dec * t_dec // 2)

    side = (mlp_w_up.reshape(DEPTH * D, D_FF), mlp_w_down.reshape(DEPTH * D_FF, D), conv_w_in[0], conv_w_out[0])
    xp, gla_p, (wup, wdn, cwin, cwout) = _gla_prompt(x_prompt, wq, wgl, wgu, bgate, ng, wo, row(ln1_g, 0),
                                                     row(ln1_b, 0), tb=256, side=side)
    wup = wup.reshape(DEPTH, D, D_FF)
    wdn = wdn.reshape(DEPTH, D_FF, D)
    xs_pad = jnp.pad(x_sample, ((0, 0), (0, SUB - t_dec), (0, 0))).reshape(n_dec * SUB, D)
    xs_pad, gla_s = _gla_sample(xs_pad, state_gla[0], wq, wgl, wgu, bgate, ng, wo, row(ln1_g, 0), row(ln1_b, 0),
                                n_seq=16, t_valid=t_dec)
    xs = xs_pad.reshape(n_dec, SUB, D)[:, :t_dec].reshape(n_dec * t_dec, D)
    xp = mlp_prompt(xp.reshape(bsz * t, D), wup, wdn, ln2g, ln2b, layer=0)
    xs = mlp_sample(xs, wup, wdn, ln2g, ln2b, layer=0)

    xp, conv_p = _conv_prompt(xp.reshape(bsz, t, D), cwin, cwc, cwout, row(ln1_g, 1), row(ln1_b, 1),
                              n_sub=4, sub_rows=256)
    buf = state_conv[0]
    e1 = jnp.pad(buf[:, 1:2], ((0, 0), (0, t_dec - 1), (0, 0))).reshape(n_dec * t_dec, D)
    e2 = jnp.pad(buf, ((0, 0), (0, t_dec - 2), (0, 0))).reshape(n_dec * t_dec, D)
    xs, u_s = _conv_sample(xs, e1, e2, cwin, cwc, cwout, row(ln1_g, 1), row(ln1_b, 1), t_seq=t_dec)
    conv_s = u_s.reshape(n_dec, t_dec, D)[:, t_dec - 2:]
    xp = mlp_prompt(xp.reshape(bsz * t, D), wup, wdn, ln2g, ln2b, layer=1)
    xs = mlp_sample(xs, wup, wdn, ln2g, ln2b, layer=1)

    return (xp.reshape(bsz, t, D), xs.reshape(n_dec, t_dec, D), gla_p[None], gla_s[None],
            conv_p[None], conv_s[None])
```

```python
import functools

import jax
import jax.numpy as jnp
from jax import lax
from jax.experimental import pallas as pl
from jax.experimental.pallas import tpu as pltpu

bf16 = jnp.bfloat16
f32 = jnp.float32

D = 1024
H = 4
DK = 512
DV = 1024
DKH = DK // H
DVH = DV // H
RANK = 16
RANK_PAD = 128
TAU = 16.0
CHUNK = 64
KCHUNK = 128
D_FF = 4 * D
DEPTH = 2
ALPHA = (2 * DEPTH) ** 0.25
LN_EPS = 1e-5
RMS_EPS = 1e-6
Q_SCALE = DKH ** -0.5

SUB = 8
VMEM_LIMIT = 56 * 1024 * 1024

_NT = (((1,), (1,)), ((), ()))
_TN = (((0,), (0,)), ((), ()))


def _dot(a, b):
    return jnp.dot(a, b, preferred_element_type=f32)


def _dg(a, b, dims):
    return lax.dot_general(a, b, dims, preferred_element_type=f32)


def _layer_norm(y, g, b):
    mu = jnp.mean(y, axis=-1, keepdims=True)
    yc = y - mu
    var = jnp.mean(yc * yc, axis=-1, keepdims=True)
    return yc * lax.rsqrt(var + LN_EPS) * g + b


def _log_sigmoid(z):
    return -(jnp.maximum(-z, 0.0) + jnp.log(1.0 + jnp.exp(-jnp.abs(z))))


def _gla_project(xb, wq_ref, wgl_ref, wgu_ref, bgate_ref):
    q = _dot(xb, wq_ref[:, 0:DK]) * Q_SCALE
    k = _dot(xb, wq_ref[:, DK:2 * DK])
    v = _dot(xb, wq_ref[:, 2 * DK:2 * DK + DV])
    r = _dot(xb, wq_ref[:, 2 * DK + DV:2 * DK + 2 * DV])
    gl = _dot(xb, wgl_ref[...])
    z = _dot(gl.astype(bf16), wgu_ref[...]) + bgate_ref[...]
    g = _log_sigmoid(z) / TAU
    return q, k, v, r, g


def _gla_post(o, r, x, ng, wo_ref, lg, lb):
    parts = []
    for h in range(H):
        vs = slice(h * DVH, (h + 1) * DVH)
        oh = o[:, vs]
        ms = jnp.mean(oh * oh, axis=-1, keepdims=True)
        parts.append(oh * lax.rsqrt(ms + RMS_EPS) * ng[:, vs])
    on = jnp.concatenate(parts, axis=1)
    gated = on * (r * jax.nn.sigmoid(r))
    y = _dot(gated.astype(bf16), wo_ref[...])
    return _layer_norm(ALPHA * x + y, lg, lb)


PROJ_TILE = 512
W_VPU, W_NORM, W_OUT, W_MXU = 1.0, 1.0, 1.0, 1.0


def _interleave(main_tasks, filler_tasks):
    total = sum(w for _, w in main_tasks)
    n_fill = len(filler_tasks)
    done, acc = 0, 0.0
    for task, w in main_tasks:
        task()
        acc += w
        want = int(round(acc * n_fill / total))
        while done < want:
            filler_tasks[done]()
            done += 1


def _gla_stage1_tasks(x_ref, slot, w_refs, scr):
    wq_ref, wgl_ref, wgu_ref, bgate_ref = w_refs
    xb_scr, q_scr, k_scr, v_scr, r_scr, g_scr = scr[:6]

    def cast_x():
        xb_scr[slot] = x_ref[...].astype(bf16)

    def proj(dst, c0, w0, scale):
        def run():
            acc = _dot(xb_scr[slot], wq_ref[:, w0:w0 + PROJ_TILE])
            dst[slot, :, c0:c0 + PROJ_TILE] = acc * scale if scale is not None else acc
        return run

    def gate():
        gl = _dot(xb_scr[slot], wgl_ref[...])
        z = _dot(gl.astype(bf16), wgu_ref[...]) + bgate_ref[...]
        g_scr[slot] = _log_sigmoid(z) / TAU

    tiles = []
    for dst, w_base, width, scale in ((q_scr, 0, DK, Q_SCALE), (k_scr, DK, DK, None),
                                      (v_scr, 2 * DK, DV, None), (r_scr, 2 * DK + DV, DV, None)):
        for c0 in range(0, width, PROJ_TILE):
            tiles.append(proj(dst, c0, w_base + c0, scale))
    return cast_x, gate, tiles


def _split3(a):
    a1 = a.astype(bf16)
    r1 = a - a1.astype(f32)
    a2 = r1.astype(bf16)
    a3 = (r1 - a2.astype(f32)).astype(bf16)
    return a1, a2, a3


def _gla_stage2_tasks(xres_ref, y_ref, r0, slot, scr, s_scr, ng_ref, wo_ref, lg_ref, lb_ref, n_chunks):
    _, q_scr, k_scr, v_scr, r_scr, g_scr, b_scr, o_scr, gated_scr = scr
    tb = n_chunks * KCHUNK
    mid = KCHUNK // 2

    def cumsum():
        row = lax.broadcasted_iota(jnp.int32, (tb, tb), 0)
        col = lax.broadcasted_iota(jnp.int32, (tb, tb), 1)
        tri = ((row >= col) & ((row // KCHUNK) == (col // KCHUNK))).astype(bf16)
        g1, g2, g3 = _split3(g_scr[slot])
        b_scr[...] = _dot(tri, g1) + _dot(tri, g2) + _dot(tri, g3)

    def chunk_prep(c, cell):
        rs = slice(c * KCHUNK, (c + 1) * KCHUNK)

        def run():
            b = b_scr[rs, :]
            b_mid = b[mid - 1:mid, :]
            b_last = b[KCHUNK - 1:KCHUNK, :]
            qc = q_scr[slot, rs, :]
            kc = k_scr[slot, rs, :]
            cell["qs"] = (qc * jnp.exp(b - b_mid)).astype(bf16)
            cell["ks"] = (kc * jnp.exp(b_mid - b)).astype(bf16)
            cell["qd"] = (qc * jnp.exp(b)).astype(bf16)
            cell["kk"] = (kc * jnp.exp(b_last - b)).astype(bf16)
            cell["vb"] = v_scr[slot, rs, :].astype(bf16)
            cell["dec_t"] = jnp.exp(jnp.broadcast_to(b_last, (KCHUNK, DK)).T)
        return run

    def head_scores(h, cell):
        ks = slice(h * DKH, (h + 1) * DKH)

        def run():
            ri = lax.broadcasted_iota(jnp.int32, (KCHUNK, KCHUNK), 0)
            ci = lax.broadcasted_iota(jnp.int32, (KCHUNK, KCHUNK), 1)
            sc = _dg(cell["qs"][:, ks], cell["ks"][:, ks], _NT)
            cell["sc", h] = jnp.where(ri >= ci, sc, 0.0).astype(bf16)
        return run

    def head_update(c, h, cell):
        rs = slice(c * KCHUNK, (c + 1) * KCHUNK)
        ks = slice(h * DKH, (h + 1) * DKH)
        vs = slice(h * DVH, (h + 1) * DVH)

        def run():
            st = s_scr[h]
            vh = cell["vb"][:, vs]
            lhs = jnp.concatenate([cell["sc", h], cell["qd"][:, ks]], axis=1)
            rhs = jnp.concatenate([vh, st.astype(bf16)], axis=0)
            o_scr[rs, vs] = _dot(lhs, rhs)
            dec_h = cell["dec_t"][ks, :]
            dec_m = jnp.concatenate([dec_h] * (DVH // KCHUNK), axis=1)
            s_scr[h] = dec_m * st + _dg(cell["kk"][:, ks], vh, _TN)
        return run

    def norm_gate(h):
        vs = slice(h * DVH, (h + 1) * DVH)

        def run():
            oh = o_scr[:, vs]
            rh = r_scr[slot, :, vs]
            ms = jnp.mean(oh * oh, axis=-1, keepdims=True)
            on = oh * lax.rsqrt(ms + RMS_EPS) * ng_ref[:, vs]
            gated_scr[:, vs] = (on * (rh * jax.nn.sigmoid(rh))).astype(bf16)
        return run

    def out_rows(m0, m1):
        def run():
            y = _dot(gated_scr[m0:m1, :], wo_ref[...])
            x = xres_ref[r0 + m0:r0 + m1, :]
            y_ref[r0 + m0:r0 + m1, :] = _layer_norm(ALPHA * x + y, lg_ref[...], lb_ref[...])
        return run

    tasks = [(cumsum, W_VPU)]
    for c in range(n_chunks):
        cell = {}
        tasks.append((chunk_prep(c, cell), W_VPU))
        tasks += [(head_scores(h, cell), W_MXU) for h in range(H)]
        tasks += [(head_update(c, h, cell), W_MXU) for h in range(H)]
    tasks += [(norm_gate(h), W_NORM) for h in range(H)]
    half = tb // 2
    tasks += [(out_rows(0, half), W_OUT), (out_rows(half, tb), W_OUT)]
    return tasks


def _gla_prompt_kernel(x0_ref, xa_ref, xb_ref, xres_ref, wq_ref, wgl_ref, wgu_ref, bgate_ref, ng_ref, wo_ref,
                       lg_ref, lb_ref, *refs, n_chunks, steps_per_seq, n_side):
    side_in = refs[:n_side]
    y_ref, st_ref = refs[n_side:n_side + 2]
    side_out = refs[n_side + 2:2 * n_side + 2]
    (xb_scr, q_scr, k_scr, v_scr, r_scr, g_scr, b_scr, o_scr, gated_scr, s_scr) = refs[2 * n_side + 2:]
    i = pl.program_id(0)
    tb = n_chunks * KCHUNK
    w_refs = (wq_ref, wgl_ref, wgu_ref, bgate_ref)
    scr = (xb_scr, q_scr, k_scr, v_scr, r_scr, g_scr, b_scr, o_scr, gated_scr)
    post = (ng_ref, wo_ref, lg_ref, lb_ref)

    @pl.when(i == 0)
    def _():
        cast_x, gate, tiles = _gla_stage1_tasks(x0_ref, 0, w_refs, scr)
        for task in [cast_x, gate] + tiles:
            task()

    @pl.when(i % steps_per_seq == 0)
    def _():
        s_scr[...] = jnp.zeros_like(s_scr)

    def side_cast(src_ref, dst_ref):
        def run():
            dst_ref[...] = src_ref[...].astype(bf16)
        return run

    side_tasks = [side_cast(s, d) for s, d in zip(side_in, side_out)]
    for r0, slot, x_next in ((0, 0, xa_ref), (tb, 1, xb_ref)):
        cast_x, gate, tiles = _gla_stage1_tasks(x_next, 1 - slot, w_refs, scr)
        stage2 = _gla_stage2_tasks(xres_ref, y_ref, r0, slot, scr, s_scr, *post, n_chunks)
        cast_x()
        fillers = tiles + side_tasks[slot::2]
        _interleave([stage2[0], (gate, W_VPU)] + stage2[1:], fillers)

    @pl.when(i % steps_per_seq == steps_per_seq - 1)
    def _():
        for h in range(H):
            st_ref[0, h] = s_scr[h]


def _gla_sample_kernel(x_ref, st_ref, wq_ref, wgl_ref, wgu_ref, bgate_ref, ng_ref, wo_ref, lg_ref, lb_ref,
                       y_ref, sto_ref, o_scr, *, n_seq, t_valid):
    rows = n_seq * SUB
    x = x_ref[...]
    q, k, v, r, g = _gla_project(x.astype(bf16), wq_ref, wgl_ref, wgu_ref, bgate_ref)

    row = lax.broadcasted_iota(jnp.int32, (rows, rows), 0)
    col = lax.broadcasted_iota(jnp.int32, (rows, rows), 1)
    same_seq = (row // SUB) == (col // SUB)
    tri = (same_seq & (row >= col)).astype(f32)
    tri_last = (same_seq & ((col % SUB) < t_valid)).astype(f32)
    b = jnp.dot(tri, g, precision=lax.Precision.HIGHEST, preferred_element_type=f32)
    b_last = jnp.dot(tri_last, g, precision=lax.Precision.HIGHEST, preferred_element_type=f32)

    valid = (lax.broadcasted_iota(jnp.int32, (rows, DK), 0) % SUB) < t_valid
    qd = (q * jnp.exp(b)).astype(bf16)
    kd = jnp.where(valid, k * jnp.exp(-b), 0.0).astype(bf16)
    kk_t = jnp.where(valid, k * jnp.exp(b_last - b), 0.0).T.astype(bf16)
    vb = v.astype(bf16)
    dec_t = jnp.exp(b_last.T)

    intra_mask = same_seq & (row >= col)
    for h in range(H):
        ks = slice(h * DKH, (h + 1) * DKH)
        vs = slice(h * DVH, (h + 1) * DVH)
        sc = jnp.where(intra_mask, _dg(qd[:, ks], kd[:, ks], _NT), 0.0).astype(bf16)
        o_scr[:, vs] = _dot(sc, vb[:, vs])

    seq_of_col = lax.broadcasted_iota(jnp.int32, (DKH, rows), 1) // SUB

    def seq_dots(n):
        rs = slice(n * SUB, (n + 1) * SUB)
        res = []
        for h in range(H):
            ks = slice(h * DKH, (h + 1) * DKH)
            vs = slice(h * DVH, (h + 1) * DVH)
            s0 = st_ref[n, h]
            o_inter = _dot(qd[rs, ks], s0.astype(bf16))
            upd = _dot(jnp.where(seq_of_col == n, kk_t[ks, :], jnp.zeros((), bf16)), vb[:, vs])
            res.append((s0, o_inter, upd))
        return res

    def seq_combine(n, res):
        rs = slice(n * SUB, (n + 1) * SUB)
        for h, (s0, o_inter, upd) in enumerate(res):
            ks = slice(h * DKH, (h + 1) * DKH)
            vs = slice(h * DVH, (h + 1) * DVH)
            o_scr[rs, vs] = o_scr[rs, vs] + o_inter
            sto_ref[n, h] = dec_t[ks, n * SUB:n * SUB + 1] * s0 + upd

    pending = None
    for n in range(n_seq):
        res = seq_dots(n)
        if pending is not None:
            seq_combine(*pending)
        pending = (n, res)
    seq_combine(*pending)

    y_ref[...] = _gla_post(o_scr[...], r, x, ng_ref[...], wo_ref, lg_ref[...], lb_ref[...])


def _conv_prompt_kernel(x0_ref, xa_ref, xb_ref, xres_ref, win_ref, wc_ref, wout_ref, lg_ref, lb_ref,
                        y_ref, st_ref, xb_scr, bg_scr, cg_scr, u_scr, gated_scr, c_scr,
                        *, tb, steps_per_seq):
    i = pl.program_id(0)
    half = tb // 2
    col_tile = D // 2

    def stage1_tasks(x_ref, slot):
        def cast():
            xb_scr[slot] = x_ref[...].astype(bf16)

        def proj_bg(c0):
            def run():
                bg_scr[slot, :, c0:c0 + col_tile] = _dot(xb_scr[slot], win_ref[:, c0:c0 + col_tile])
            return run

        def proj_cg(c0):
            def run():
                cg_scr[:, c0:c0 + col_tile] = _dot(xb_scr[slot], win_ref[:, D + c0:D + c0 + col_tile])
            return run

        def proj_u(c0):
            def run():
                hh = _dot(xb_scr[slot], win_ref[:, 2 * D + c0:2 * D + c0 + col_tile])
                u_scr[slot, :, c0:c0 + col_tile] = cg_scr[:, c0:c0 + col_tile] * hh
            return run

        cols = range(0, D, col_tile)
        return cast, ([proj_bg(c) for c in cols] + [proj_cg(c) for c in cols] + [proj_u(c) for c in cols])

    def stage2_tasks(r0, slot):
        def conv(m0):
            def run():
                u = u_scr[slot, m0:m0 + half, :]
                t = lax.broadcasted_iota(jnp.int32, (half, D), 0)
                c0 = c_scr[0:1, :]
                c1 = c_scr[1:2, :]
                p1 = jnp.where(t == 0, c1, pltpu.roll(u, 1, 0))
                p2 = jnp.where(t == 0, c0, jnp.where(t == 1, c1, pltpu.roll(u, 2, 0)))
                cv = p2 * wc_ref[0:1, :] + p1 * wc_ref[1:2, :] + u * wc_ref[2:3, :]
                gated_scr[m0:m0 + half, :] = (bg_scr[slot, m0:m0 + half, :] * cv).astype(bf16)
                c_scr[0:2, :] = u[half - 2:half, :]
            return run

        def out(m0):
            def run():
                y = _dot(gated_scr[m0:m0 + half, :], wout_ref[...])
                x = xres_ref[r0 + m0:r0 + m0 + half, :]
                y_ref[r0 + m0:r0 + m0 + half, :] = _layer_norm(ALPHA * x + y, lg_ref[...], lb_ref[...])
            return run

        return [(conv(0), 1.0), (conv(half), 1.0), (out(0), 1.0), (out(half), 1.0)]

    @pl.when(i == 0)
    def _():
        cast, tiles = stage1_tasks(x0_ref, 0)
        cast()
        for task in tiles:
            task()

    @pl.when(i % steps_per_seq == 0)
    def _():
        c_scr[...] = jnp.zeros_like(c_scr)

    for r0, slot, x_next in ((0, 0, xa_ref), (tb, 1, xb_ref)):
        cast, tiles = stage1_tasks(x_next, 1 - slot)
        cast()
        _interleave(stage2_tasks(r0, slot), tiles)

    st_ref[0] = c_scr[0:2, :]


def _conv_sample_kernel(x_ref, e1_ref, e2_ref, win_ref, wc_ref, wout_ref, lg_ref, lb_ref, y_ref, u_ref, *, t_seq):
    x = x_ref[...]
    rows = x.shape[0]
    xb = x.astype(bf16)
    bg = _dot(xb, win_ref[:, 0:D])
    cg = _dot(xb, win_ref[:, D:2 * D])
    hh = _dot(xb, win_ref[:, 2 * D:3 * D])
    u = cg * hh
    t = lax.broadcasted_iota(jnp.int32, (rows, D), 0) % t_seq
    p1 = jnp.where(t >= 1, pltpu.roll(u, 1, 0), e1_ref[...])
    p2 = jnp.where(t >= 2, pltpu.roll(u, 2, 0), e2_ref[...])
    conv = p2 * wc_ref[0:1, :] + p1 * wc_ref[1:2, :] + u * wc_ref[2:3, :]
    y = _dot((bg * conv).astype(bf16), wout_ref[...])
    y_ref[...] = _layer_norm(ALPHA * x + y, lg_ref[...], lb_ref[...])
    u_ref[...] = u


def _mlp_kernel(x_ref, wup_ref, wdn_ref, lg_ref, lb_ref, y_ref, xb_scr, h_scr, *, n_sub, sub_rows, ff_chunk):
    def sub_tasks(s):
        rs = slice(s * sub_rows, (s + 1) * sub_rows)
        buf = s % 2

        def cast():
            xb_scr[buf] = x_ref[rs, :].astype(bf16)

        def up(c):
            cs = slice(c * ff_chunk, (c + 1) * ff_chunk)

            def run():
                hcol = _dot(xb_scr[buf], wup_ref[0, :, cs])
                h_scr[buf, :, cs] = jnp.square(jnp.maximum(hcol, 0.0)).astype(bf16)
            return run

        def down():
            y = _dot(h_scr[buf], wdn_ref[0])
            y_ref[rs, :] = _layer_norm(ALPHA * x_ref[rs, :] + y, lg_ref[0], lb_ref[0])

        return [cast] + [up(c) for c in range(D_FF // ff_chunk)], down

    pending = None
    for s in range(n_sub):
        ups, down = sub_tasks(s)
        ups[0]()
        ups[1]()
        if pending is not None:
            pending()
        for task in ups[2:]:
            task()
        pending = down
    pending()


def _const_spec(shape):
    nd = len(shape)
    return pl.BlockSpec(shape, lambda *_: (0,) * nd, pipeline_mode=pl.Buffered(1))


def _layer_spec(shape, layer):
    nd = len(shape)
    return pl.BlockSpec((1,) + tuple(shape[1:]), lambda *_: (layer,) + (0,) * (nd - 1),
                        pipeline_mode=pl.Buffered(1))


def _mlp(x2d, wup, wdn, lg, lb, layer, n_sub, sub_rows):
    m = x2d.shape[0]
    tm = n_sub * sub_rows
    return pl.pallas_call(
        functools.partial(_mlp_kernel, n_sub=n_sub, sub_rows=sub_rows, ff_chunk=1024),
        grid=(m // tm,),
        in_specs=[pl.BlockSpec((tm, D), lambda i: (i, 0)),
                  _layer_spec(wup.shape, layer), _layer_spec(wdn.shape, layer),
                  _layer_spec(lg.shape, layer), _layer_spec(lb.shape, layer)],
        out_specs=pl.BlockSpec((tm, D), lambda i: (i, 0)),
        out_shape=jax.ShapeDtypeStruct((m, D), f32),
        scratch_shapes=[pltpu.VMEM((2, sub_rows, D), bf16), pltpu.VMEM((2, sub_rows, D_FF), bf16)],
        compiler_params=pltpu.CompilerParams(dimension_semantics=("arbitrary",), vmem_limit_bytes=VMEM_LIMIT),
        name="mlp",
    )(x2d, wup, wdn, lg, lb)


def _gla_prompt(x, wq, wgl, wgu, bgate, ng, wo, lg, lb, tb, side):
    bsz, t, _ = x.shape
    n_chunks = tb // KCHUNK
    n_blocks = bsz * t // tb
    n_steps = n_blocks // 2
    steps_per_seq = t // (2 * tb)
    x2d = x.reshape(bsz * t, D)
    side_specs = [pl.BlockSpec((a.shape[0] // n_steps, a.shape[1]), lambda i: (i, 0)) for a in side]
    outs = pl.pallas_call(
        functools.partial(_gla_prompt_kernel, n_chunks=n_chunks, steps_per_seq=steps_per_seq, n_side=len(side)),
        grid=(n_steps,),
        in_specs=[pl.BlockSpec((tb, D), lambda i: (0, 0), pipeline_mode=pl.Buffered(1)),
                  pl.BlockSpec((tb, D), lambda i: (2 * i + 1, 0)),
                  pl.BlockSpec((tb, D), lambda i: (jnp.minimum(2 * i + 2, n_blocks - 1), 0)),
                  pl.BlockSpec((2 * tb, D), lambda i: (i, 0)),
                  _const_spec(wq.shape), _const_spec(wgl.shape), _const_spec(wgu.shape),
                  _const_spec(bgate.shape), _const_spec(ng.shape), _const_spec(wo.shape),
                  _const_spec(lg.shape), _const_spec(lb.shape)] + side_specs,
        out_specs=[pl.BlockSpec((2 * tb, D), lambda i: (i, 0)),
                   pl.BlockSpec((1, H, DKH, DVH), lambda i: (i // steps_per_seq, 0, 0, 0))] + side_specs,
        out_shape=[jax.ShapeDtypeStruct((bsz * t, D), f32),
                   jax.ShapeDtypeStruct((bsz, H, DKH, DVH), f32)]
                  + [jax.ShapeDtypeStruct(a.shape, bf16) for a in side],
        scratch_shapes=[pltpu.VMEM((2, tb, D), bf16),
                        pltpu.VMEM((2, tb, DK), f32), pltpu.VMEM((2, tb, DK), f32),
                        pltpu.VMEM((2, tb, DV), f32), pltpu.VMEM((2, tb, DV), f32),
                        pltpu.VMEM((2, tb, DK), f32),
                        pltpu.VMEM((tb, DK), f32), pltpu.VMEM((tb, DV), f32), pltpu.VMEM((tb, DV), bf16),
                        pltpu.VMEM((H, DKH, DVH), f32)],
        compiler_params=pltpu.CompilerParams(dimension_semantics=("arbitrary",),
                                             vmem_limit_bytes=VMEM_LIMIT),
        name="gla_prompt",
    )(x2d, x2d, x2d, x2d, wq, wgl, wgu, bgate, ng, wo, lg, lb, *side)
    return outs[0].reshape(bsz, t, D), outs[1], outs[2:]


def _gla_sample(x_pad, state, wq, wgl, wgu, bgate, ng, wo, lg, lb, n_seq, t_valid):
    s_total = state.shape[0]
    rows = n_seq * SUB
    return pl.pallas_call(
        functools.partial(_gla_sample_kernel, n_seq=n_seq, t_valid=t_valid),
        grid=(s_total // n_seq,),
        in_specs=[pl.BlockSpec((rows, D), lambda i: (i, 0)),
                  pl.BlockSpec((n_seq, H, DKH, DVH), lambda i: (i, 0, 0, 0)),
                  _const_spec(wq.shape), _const_spec(wgl.shape), _const_spec(wgu.shape),
                  _const_spec(bgate.shape), _const_spec(ng.shape), _const_spec(wo.shape),
                  _const_spec(lg.shape), _const_spec(lb.shape)],
        out_specs=[pl.BlockSpec((rows, D), lambda i: (i, 0)),
                   pl.BlockSpec((n_seq, H, DKH, DVH), lambda i: (i, 0, 0, 0))],
        out_shape=[jax.ShapeDtypeStruct((s_total * SUB, D), f32),
                   jax.ShapeDtypeStruct(state.shape, f32)],
        scratch_shapes=[pltpu.VMEM((rows, DV), f32)],
        compiler_params=pltpu.CompilerParams(dimension_semantics=("arbitrary",), vmem_limit_bytes=VMEM_LIMIT),
        name="gla_sample",
    )(x_pad, state, wq, wgl, wgu, bgate, ng, wo, lg, lb)


def _conv_prompt(x, win, wc, wout, lg, lb, tb):
    bsz, t, _ = x.shape
    n_blocks = bsz * t // tb
    steps_per_seq = t // (2 * tb)
    x2d = x.reshape(bsz * t, D)
    y, st = pl.pallas_call(
        functools.partial(_conv_prompt_kernel, tb=tb, steps_per_seq=steps_per_seq),
        grid=(n_blocks // 2,),
        in_specs=[pl.BlockSpec((tb, D), lambda i: (0, 0), pipeline_mode=pl.Buffered(1)),
                  pl.BlockSpec((tb, D), lambda i: (2 * i + 1, 0)),
                  pl.BlockSpec((tb, D), lambda i: (jnp.minimum(2 * i + 2, n_blocks - 1), 0)),
                  pl.BlockSpec((2 * tb, D), lambda i: (i, 0)),
                  _const_spec(win.shape), _const_spec(wc.shape), _const_spec(wout.shape),
                  _const_spec(lg.shape), _const_spec(lb.shape)],
        out_specs=[pl.BlockSpec((2 * tb, D), lambda i: (i, 0)),
                   pl.BlockSpec((1, 2, D), lambda i: (i // steps_per_seq, 0, 0))],
        out_shape=[jax.ShapeDtypeStruct((bsz * t, D), f32),
                   jax.ShapeDtypeStruct((bsz, 2, D), f32)],
        scratch_shapes=[pltpu.VMEM((2, tb, D), bf16), pltpu.VMEM((2, tb, D), f32),
                        pltpu.VMEM((tb, D), f32), pltpu.VMEM((2, tb, D), f32),
                        pltpu.VMEM((tb, D), bf16), pltpu.VMEM((SUB, D), f32)],
        compiler_params=pltpu.CompilerParams(dimension_semantics=("arbitrary",), vmem_limit_bytes=VMEM_LIMIT),
        name="conv_prompt",
    )(x2d, x2d, x2d, x2d, win, wc, wout, lg, lb)
    return y.reshape(bsz, t, D), st


def _conv_sample(x2d, e1, e2, win, wc, wout, lg, lb, t_seq):
    m = x2d.shape[0]
    full = lambda shape: pl.BlockSpec(shape, lambda i: (0,) * len(shape))
    return pl.pallas_call(
        functools.partial(_conv_sample_kernel, t_seq=t_seq),
        grid=(1,),
        in_specs=[full((m, D)), full((m, D)), full((m, D)),
                  _const_spec(win.shape), _const_spec(wc.shape), _const_spec(wout.shape),
                  _const_spec(lg.shape), _const_spec(lb.shape)],
        out_specs=[full((m, D)), full((m, D))],
        out_shape=[jax.ShapeDtypeStruct((m, D), f32), jax.ShapeDtypeStruct((m, D), f32)],
        compiler_params=pltpu.CompilerParams(dimension_semantics=("arbitrary",), vmem_limit_bytes=VMEM_LIMIT),
        name="conv_sample",
    )(x2d, e1, e2, win, wc, wout, lg, lb)


def kernel(x_prompt, x_sample, state_gla, state_conv, gla_w_in, gla_w_gate_up, gla_b_gate, gla_norm_g, gla_w_o, conv_w_in, conv_w_conv, conv_w_out, mlp_w_up, mlp_w_down, ln1_g, ln1_b, ln2_g, ln2_b):
    bsz, t, _ = x_prompt.shape
    n_dec, t_dec, _ = x_sample.shape
    assert t % CHUNK == 0 and t_dec <= SUB and t_dec >= 2

    w_in = gla_w_in[0]
    wq = w_in.astype(bf16)
    wgl = jnp.pad(w_in[:, 2 * DK + 2 * DV:], ((0, 0), (0, RANK_PAD - RANK))).astype(bf16)
    wgu = jnp.pad(gla_w_gate_up[0], ((0, RANK_PAD - RANK), (0, 0))).astype(bf16)
    bgate = gla_b_gate[0].reshape(1, DK)
    ng = gla_norm_g[0].reshape(1, DV)
    wo = gla_w_o[0].astype(bf16)
    cwc = conv_w_conv[0]
    ln2g = ln2_g.reshape(DEPTH, 1, D)
    ln2b = ln2_b.reshape(DEPTH, 1, D)
    row = lambda a, i: a[i].reshape(1, D)
    mlp_prompt = functools.partial(_mlp, n_sub=4, sub_rows=256)
    mlp_sample = functools.partial(_mlp, n_sub=2, sub_rows=n_dec * t_dec // 2)

    side = (mlp_w_up.reshape(DEPTH * D, D_FF), mlp_w_down.reshape(DEPTH * D_FF, D), conv_w_in[0], conv_w_out[0])
    xp, gla_p, (wup, wdn, cwin, cwout) = _gla_prompt(x_prompt, wq, wgl, wgu, bgate, ng, wo, row(ln1_g, 0),
                                                     row(ln1_b, 0), tb=256, side=side)
    wup = wup.reshape(DEPTH, D, D_FF)
    wdn = wdn.reshape(DEPTH, D_FF, D)
    xs_pad = jnp.pad(x_sample, ((0, 0), (0, SUB - t_dec), (0, 0))).reshape(n_dec * SUB, D)
    xs_pad, gla_s = _gla_sample(xs_pad, state_gla[0], wq, wgl, wgu, bgate, ng, wo, row(ln1_g, 0), row(ln1_b, 0),
                                n_seq=16, t_valid=t_dec)
    xs = xs_pad.reshape(n_dec, SUB, D)[:, :t_dec].reshape(n_dec * t_dec, D)
    xp = mlp_prompt(xp.reshape(bsz * t, D), wup, wdn, ln2g, ln2b, layer=0)
    xs = mlp_sample(xs, wup, wdn, ln2g, ln2b, layer=0)

    xp, conv_p = _conv_prompt(xp.reshape(bsz, t, D), cwin, cwc, cwout, row(ln1_g, 1), row(ln1_b, 1), tb=512)
    buf = state_conv[0]
    e1 = jnp.pad(buf[:, 1:2], ((0, 0), (0, t_dec - 1), (0, 0))).reshape(n_dec * t_dec, D)
    e2 = jnp.pad(buf, ((0, 0), (0, t_dec - 2), (0, 0))).reshape(n_dec * t_dec, D)
    xs, u_s = _conv_sample(xs, e1, e2, cwin, cwc, cwout, row(ln1_g, 1), row(ln1_b, 1), t_seq=t_dec)
    conv_s = u_s.reshape(n_dec, t_dec, D)[:, t_dec - 2:]
    xp = mlp_prompt(xp.reshape(bsz * t, D), wup, wdn, ln2g, ln2b, layer=1)
    xs = mlp_sample(xs, wup, wdn, ln2g, ln2b, layer=1)

    return (xp.reshape(bsz, t, D), xs.reshape(n_dec, t_dec, D), gla_p[None], gla_s[None],
            conv_p[None], conv_s[None])
```

```python
import functools

import jax
import jax.numpy as jnp
from jax import lax
from jax.experimental import pallas as pl
from jax.experimental.pallas import tpu as pltpu

bf16 = jnp.bfloat16
f32 = jnp.float32

D = 1024
H = 4
DK = 512
DV = 1024
DKH = DK // H
DVH = DV // H
RANK = 16
RANK_PAD = 128
TAU = 16.0
CHUNK = 64
KCHUNK = 128
D_FF = 4 * D
DEPTH = 2
ALPHA = (2 * DEPTH) ** 0.25
LN_EPS = 1e-5
RMS_EPS = 1e-6
Q_SCALE = DKH ** -0.5

SUB = 8
VMEM_LIMIT = 56 * 1024 * 1024

_NT = (((1,), (1,)), ((), ()))
_TN = (((0,), (0,)), ((), ()))


def _dot(a, b):
    return jnp.dot(a, b, preferred_element_type=f32)


def _dg(a, b, dims):
    return lax.dot_general(a, b, dims, preferred_element_type=f32)


def _layer_norm(y, g, b):
    mu = jnp.mean(y, axis=-1, keepdims=True)
    yc = y - mu
    var = jnp.mean(yc * yc, axis=-1, keepdims=True)
    return yc * lax.rsqrt(var + LN_EPS) * g + b


def _log_sigmoid(z):
    return -(jnp.maximum(-z, 0.0) + jnp.log(1.0 + jnp.exp(-jnp.abs(z))))


def _gla_project(xb, wq_ref, wgl_ref, wgu_ref, bgate_ref):
    q = _dot(xb, wq_ref[:, 0:DK]) * Q_SCALE
    k = _dot(xb, wq_ref[:, DK:2 * DK])
    v = _dot(xb, wq_ref[:, 2 * DK:2 * DK + DV])
    r = _dot(xb, wq_ref[:, 2 * DK + DV:2 * DK + 2 * DV])
    gl = _dot(xb, wgl_ref[...])
    z = _dot(gl.astype(bf16), wgu_ref[...]) + bgate_ref[...]
    g = _log_sigmoid(z) / TAU
    return q, k, v, r, g


def _gla_post(o, r, x, ng, wo_ref, lg, lb):
    parts = []
    for h in range(H):
        vs = slice(h * DVH, (h + 1) * DVH)
        oh = o[:, vs]
        ms = jnp.mean(oh * oh, axis=-1, keepdims=True)
        parts.append(oh * lax.rsqrt(ms + RMS_EPS) * ng[:, vs])
    on = jnp.concatenate(parts, axis=1)
    gated = on * (r * jax.nn.sigmoid(r))
    y = _dot(gated.astype(bf16), wo_ref[...])
    return _layer_norm(ALPHA * x + y, lg, lb)


PROJ_TILE = 512
W_VPU, W_NORM, W_OUT, W_MXU = 1.0, 1.0, 1.0, 1.0


def _interleave(main_tasks, filler_tasks):
    total = sum(w for _, w in main_tasks)
    n_fill = len(filler_tasks)
    done, acc = 0, 0.0
    for task, w in main_tasks:
        task()
        acc += w
        want = int(round(acc * n_fill / total))
        while done < want:
            filler_tasks[done]()
            done += 1


def _gla_stage1_tasks(x_ref, slot, w_refs, scr):
    wq_ref, wgl_ref, wgu_ref, bgate_ref = w_refs
    xb_scr, q_scr, k_scr, v_scr, r_scr, g_scr = scr[:6]

    def cast_x():
        xb_scr[slot] = x_ref[...].astype(bf16)

    def proj(dst, c0, w0, scale):
        def run():
            acc = _dot(xb_scr[slot], wq_ref[:, w0:w0 + PROJ_TILE])
            dst[slot, :, c0:c0 + PROJ_TILE] = acc * scale if scale is not None else acc
        return run

    def gate():
        gl = _dot(xb_scr[slot], wgl_ref[...])
        z = _dot(gl.astype(bf16), wgu_ref[...]) + bgate_ref[...]
        g_scr[slot] = _log_sigmoid(z) / TAU

    tiles = []
    for dst, w_base, width, scale in ((q_scr, 0, DK, Q_SCALE), (k_scr, DK, DK, None),
                                      (v_scr, 2 * DK, DV, None), (r_scr, 2 * DK + DV, DV, None)):
        for c0 in range(0, width, PROJ_TILE):
            tiles.append(proj(dst, c0, w_base + c0, scale))
    return cast_x, gate, tiles


def _split3(a):
    a1 = a.astype(bf16)
    r1 = a - a1.astype(f32)
    a2 = r1.astype(bf16)
    a3 = (r1 - a2.astype(f32)).astype(bf16)
    return a1, a2, a3


def _gla_stage2_tasks(xres_ref, y_ref, r0, slot, scr, s_scr, ng_ref, wo_ref, lg_ref, lb_ref, n_chunks):
    _, q_scr, k_scr, v_scr, r_scr, g_scr, b_scr, o_scr, gated_scr = scr
    tb = n_chunks * KCHUNK
    mid = KCHUNK // 2

    def cumsum():
        row = lax.broadcasted_iota(jnp.int32, (tb, tb), 0)
        col = lax.broadcasted_iota(jnp.int32, (tb, tb), 1)
        tri = ((row >= col) & ((row // KCHUNK) == (col // KCHUNK))).astype(bf16)
        g1, g2, g3 = _split3(g_scr[slot])
        b_scr[...] = _dot(tri, g1) + _dot(tri, g2) + _dot(tri, g3)

    def chunk_prep(c, cell):
        rs = slice(c * KCHUNK, (c + 1) * KCHUNK)

        def run():
            b = b_scr[rs, :]
            b_mid = b[mid - 1:mid, :]
            b_last = b[KCHUNK - 1:KCHUNK, :]
            qc = q_scr[slot, rs, :]
            kc = k_scr[slot, rs, :]
            cell["qs"] = (qc * jnp.exp(b - b_mid)).astype(bf16)
            cell["ks"] = (kc * jnp.exp(b_mid - b)).astype(bf16)
            cell["qd"] = (qc * jnp.exp(b)).astype(bf16)
            cell["kk"] = (kc * jnp.exp(b_last - b)).astype(bf16)
            cell["vb"] = v_scr[slot, rs, :].astype(bf16)
            cell["dec_t"] = jnp.exp(jnp.broadcast_to(b_last, (KCHUNK, DK)).T)
        return run

    def head_scores(h, cell):
        ks = slice(h * DKH, (h + 1) * DKH)

        def run():
            ri = lax.broadcasted_iota(jnp.int32, (KCHUNK, KCHUNK), 0)
            ci = lax.broadcasted_iota(jnp.int32, (KCHUNK, KCHUNK), 1)
            sc = _dg(cell["qs"][:, ks], cell["ks"][:, ks], _NT)
            cell["sc", h] = jnp.where(ri >= ci, sc, 0.0).astype(bf16)
        return run

    def head_update(c, h, cell):
        rs = slice(c * KCHUNK, (c + 1) * KCHUNK)
        ks = slice(h * DKH, (h + 1) * DKH)
        vs = slice(h * DVH, (h + 1) * DVH)

        def run():
            st = s_scr[h]
            vh = cell["vb"][:, vs]
            lhs = jnp.concatenate([cell["sc", h], cell["qd"][:, ks]], axis=1)
            rhs = jnp.concatenate([vh, st.astype(bf16)], axis=0)
            o_scr[rs, vs] = _dot(lhs, rhs)
            dec_h = cell["dec_t"][ks, :]
            dec_m = jnp.concatenate([dec_h] * (DVH // KCHUNK), axis=1)
            s_scr[h] = dec_m * st + _dg(cell["kk"][:, ks], vh, _TN)
        return run

    def norm_gate(h):
        vs = slice(h * DVH, (h + 1) * DVH)

        def run():
            oh = o_scr[:, vs]
            rh = r_scr[slot, :, vs]
            ms = jnp.mean(oh * oh, axis=-1, keepdims=True)
            on = oh * lax.rsqrt(ms + RMS_EPS) * ng_ref[:, vs]
            gated_scr[:, vs] = (on * (rh * jax.nn.sigmoid(rh))).astype(bf16)
        return run

    def out_rows(m0, m1):
        def run():
            y = _dot(gated_scr[m0:m1, :], wo_ref[...])
            x = xres_ref[r0 + m0:r0 + m1, :]
            y_ref[r0 + m0:r0 + m1, :] = _layer_norm(ALPHA * x + y, lg_ref[...], lb_ref[...])
        return run

    tasks = [(cumsum, W_VPU)]
    for c in range(n_chunks):
        cell = {}
        tasks.append((chunk_prep(c, cell), W_VPU))
        tasks += [(head_scores(h, cell), W_MXU) for h in range(H)]
        tasks += [(head_update(c, h, cell), W_MXU) for h in range(H)]
    tasks += [(norm_gate(h), W_NORM) for h in range(H)]
    half = tb // 2
    tasks += [(out_rows(0, half), W_OUT), (out_rows(half, tb), W_OUT)]
    return tasks


def _gla_prompt_kernel(x0_ref, xa_ref, xb_ref, xres_ref, wq_ref, wgl_ref, wgu_ref, bgate_ref, ng_ref, wo_ref,
                       lg_ref, lb_ref, *refs, n_chunks, steps_per_seq, n_side):
    side_in = refs[:n_side]
    y_ref, st_ref = refs[n_side:n_side + 2]
    side_out = refs[n_side + 2:2 * n_side + 2]
    (xb_scr, q_scr, k_scr, v_scr, r_scr, g_scr, b_scr, o_scr, gated_scr, s_scr) = refs[2 * n_side + 2:]
    i = pl.program_id(0)
    tb = n_chunks * KCHUNK
    w_refs = (wq_ref, wgl_ref, wgu_ref, bgate_ref)
    scr = (xb_scr, q_scr, k_scr, v_scr, r_scr, g_scr, b_scr, o_scr, gated_scr)
    post = (ng_ref, wo_ref, lg_ref, lb_ref)

    @pl.when(i == 0)
    def _():
        cast_x, gate, tiles = _gla_stage1_tasks(x0_ref, 0, w_refs, scr)
        for task in [cast_x, gate] + tiles:
            task()

    @pl.when(i % steps_per_seq == 0)
    def _():
        s_scr[...] = jnp.zeros_like(s_scr)

    def side_cast(src_ref, dst_ref):
        def run():
            dst_ref[...] = src_ref[...].astype(bf16)
        return run

    side_tasks = [side_cast(s, d) for s, d in zip(side_in, side_out)]
    for r0, slot, x_next in ((0, 0, xa_ref), (tb, 1, xb_ref)):
        cast_x, gate, tiles = _gla_stage1_tasks(x_next, 1 - slot, w_refs, scr)
        stage2 = _gla_stage2_tasks(xres_ref, y_ref, r0, slot, scr, s_scr, *post, n_chunks)
        cast_x()
        fillers = tiles + side_tasks[slot::2]
        _interleave([stage2[0], (gate, W_VPU)] + stage2[1:], fillers)

    @pl.when(i % steps_per_seq == steps_per_seq - 1)
    def _():
        for h in range(H):
            st_ref[0, h] = s_scr[h]


def _gla_sample_kernel(x_ref, st_ref, wq_ref, wgl_ref, wgu_ref, bgate_ref, ng_ref, wo_ref, lg_ref, lb_ref,
                       y_ref, sto_ref, o_scr, *, n_seq, t_valid):
    rows = n_seq * SUB
    x = x_ref[...]
    q, k, v, r, g = _gla_project(x.astype(bf16), wq_ref, wgl_ref, wgu_ref, bgate_ref)

    row = lax.broadcasted_iota(jnp.int32, (rows, rows), 0)
    col = lax.broadcasted_iota(jnp.int32, (rows, rows), 1)
    same_seq = (row // SUB) == (col // SUB)
    tri = (same_seq & (row >= col)).astype(f32)
    tri_last = (same_seq & ((col % SUB) < t_valid)).astype(f32)
    b = jnp.dot(tri, g, precision=lax.Precision.HIGHEST, preferred_element_type=f32)
    b_last = jnp.dot(tri_last, g, precision=lax.Precision.HIGHEST, preferred_element_type=f32)

    valid = (lax.broadcasted_iota(jnp.int32, (rows, DK), 0) % SUB) < t_valid
    qd = (q * jnp.exp(b)).astype(bf16)
    kd = jnp.where(valid, k * jnp.exp(-b), 0.0).astype(bf16)
    kk_t = jnp.where(valid, k * jnp.exp(b_last - b), 0.0).T.astype(bf16)
    vb = v.astype(bf16)
    dec_t = jnp.exp(b_last.T)

    intra_mask = same_seq & (row >= col)
    for h in range(H):
        ks = slice(h * DKH, (h + 1) * DKH)
        vs = slice(h * DVH, (h + 1) * DVH)
        sc = jnp.where(intra_mask, _dg(qd[:, ks], kd[:, ks], _NT), 0.0).astype(bf16)
        o_scr[:, vs] = _dot(sc, vb[:, vs])

    seq_of_col = lax.broadcasted_iota(jnp.int32, (DKH, rows), 1) // SUB

    def seq_dots(n):
        rs = slice(n * SUB, (n + 1) * SUB)
        res = []
        for h in range(H):
            ks = slice(h * DKH, (h + 1) * DKH)
            vs = slice(h * DVH, (h + 1) * DVH)
            s0 = st_ref[n, h]
            o_inter = _dot(qd[rs, ks], s0.astype(bf16))
            upd = _dot(jnp.where(seq_of_col == n, kk_t[ks, :], jnp.zeros((), bf16)), vb[:, vs])
            res.append((s0, o_inter, upd))
        return res

    def seq_combine(n, res):
        rs = slice(n * SUB, (n + 1) * SUB)
        for h, (s0, o_inter, upd) in enumerate(res):
            ks = slice(h * DKH, (h + 1) * DKH)
            vs = slice(h * DVH, (h + 1) * DVH)
            o_scr[rs, vs] = o_scr[rs, vs] + o_inter
            sto_ref[n, h] = dec_t[ks, n * SUB:n * SUB + 1] * s0 + upd

    pending = None
    for n in range(n_seq):
        res = seq_dots(n)
        if pending is not None:
            seq_combine(*pending)
        pending = (n, res)
    seq_combine(*pending)

    y_ref[...] = _gla_post(o_scr[...], r, x, ng_ref[...], wo_ref, lg_ref[...], lb_ref[...])


def _conv_prompt_kernel(x0_ref, xa_ref, xb_ref, xres_ref, win_ref, wc_ref, wout_ref, lg_ref, lb_ref,
                        y_ref, st_ref, xb_scr, bg_scr, cg_scr, u_scr, gated_scr, c_scr,
                        *, tb, steps_per_seq):
    i = pl.program_id(0)
    half = tb // 2
    col_tile = D // 2

    def stage1_tasks(x_ref, slot):
        def cast():
            xb_scr[slot] = x_ref[...].astype(bf16)

        def proj_bg(c0):
            def run():
                bg_scr[slot, :, c0:c0 + col_tile] = _dot(xb_scr[slot], win_ref[:, c0:c0 + col_tile])
            return run

        def proj_cg(c0):
            def run():
                cg_scr[:, c0:c0 + col_tile] = _dot(xb_scr[slot], win_ref[:, D + c0:D + c0 + col_tile])
            return run

        def proj_u(c0):
            def run():
                hh = _dot(xb_scr[slot], win_ref[:, 2 * D + c0:2 * D + c0 + col_tile])
                u_scr[slot, :, c0:c0 + col_tile] = cg_scr[:, c0:c0 + col_tile] * hh
            return run

        cols = range(0, D, col_tile)
        return cast, ([proj_bg(c) for c in cols] + [proj_cg(c) for c in cols] + [proj_u(c) for c in cols])

    def stage2_tasks(r0, slot):
        def conv(m0):
            def run():
                u = u_scr[slot, m0:m0 + half, :]
                t = lax.broadcasted_iota(jnp.int32, (half, D), 0)
                c0 = c_scr[0:1, :]
                c1 = c_scr[1:2, :]
                p1 = jnp.where(t == 0, c1, pltpu.roll(u, 1, 0))
                p2 = jnp.where(t == 0, c0, jnp.where(t == 1, c1, pltpu.roll(u, 2, 0)))
                cv = p2 * wc_ref[0:1, :] + p1 * wc_ref[1:2, :] + u * wc_ref[2:3, :]
                gated_scr[m0:m0 + half, :] = (bg_scr[slot, m0:m0 + half, :] * cv).astype(bf16)
                c_scr[0:2, :] = u[half - 2:half, :]
            return run

        def out(m0):
            def run():
                y = _dot(gated_scr[m0:m0 + half, :], wout_ref[...])
                x = xres_ref[r0 + m0:r0 + m0 + half, :]
                y_ref[r0 + m0:r0 + m0 + half, :] = _layer_norm(ALPHA * x + y, lg_ref[...], lb_ref[...])
            return run

        return [(conv(0), 1.0), (conv(half), 1.0), (out(0), 1.0), (out(half), 1.0)]

    @pl.when(i == 0)
    def _():
        cast, tiles = stage1_tasks(x0_ref, 0)
        cast()
        for task in tiles:
            task()

    @pl.when(i % steps_per_seq == 0)
    def _():
        c_scr[...] = jnp.zeros_like(c_scr)

    for r0, slot, x_next in ((0, 0, xa_ref), (tb, 1, xb_ref)):
        cast, tiles = stage1_tasks(x_next, 1 - slot)
        cast()
        _interleave(stage2_tasks(r0, slot), tiles)

    st_ref[0] = c_scr[0:2, :]


def _conv_sample_kernel(x_ref, e1_ref, e2_ref, win_ref, wc_ref, wout_ref, lg_ref, lb_ref, y_ref, u_ref,
                        xb_scr, p_scr, *, t_seq):
    c = pl.program_id(0)
    n_proj = p_scr.shape[0]

    @pl.when(c == 0)
    def _():
        xb_scr[...] = x_ref[...].astype(bf16)

    @pl.when(c < n_proj)
    def _():
        p_scr[c] = _dot(xb_scr[...], win_ref[...])

    @pl.when(c == n_proj)
    def _():
        x = x_ref[...]
        u = p_scr[1] * p_scr[2]
        t = lax.broadcasted_iota(jnp.int32, u.shape, 0) % t_seq
        p1 = jnp.where(t >= 1, pltpu.roll(u, 1, 0), e1_ref[...])
        p2 = jnp.where(t >= 2, pltpu.roll(u, 2, 0), e2_ref[...])
        conv = p2 * wc_ref[0:1, :] + p1 * wc_ref[1:2, :] + u * wc_ref[2:3, :]
        y = _dot((p_scr[0] * conv).astype(bf16), wout_ref[...])
        y_ref[...] = _layer_norm(ALPHA * x + y, lg_ref[...], lb_ref[...])
        u_ref[...] = u


def _mlp_kernel(x_ref, wup_ref, wdn_ref, lg_ref, lb_ref, y_ref, xb_scr, h_scr, *, n_sub, sub_rows, ff_chunk):
    def sub_tasks(s):
        rs = slice(s * sub_rows, (s + 1) * sub_rows)
        buf = s % 2

        def cast():
            xb_scr[buf] = x_ref[rs, :].astype(bf16)

        def up(c):
            cs = slice(c * ff_chunk, (c + 1) * ff_chunk)

            def run():
                hcol = _dot(xb_scr[buf], wup_ref[0, :, cs])
                h_scr[buf, :, cs] = jnp.square(jnp.maximum(hcol, 0.0)).astype(bf16)
            return run

        def down():
            y = _dot(h_scr[buf], wdn_ref[0])
            y_ref[rs, :] = _layer_norm(ALPHA * x_ref[rs, :] + y, lg_ref[0], lb_ref[0])

        return [cast] + [up(c) for c in range(D_FF // ff_chunk)], down

    pending = None
    for s in range(n_sub):
        ups, down = sub_tasks(s)
        ups[0]()
        ups[1]()
        if pending is not None:
            pending()
        for task in ups[2:]:
            task()
        pending = down
    pending()


def _const_spec(shape):
    nd = len(shape)
    return pl.BlockSpec(shape, lambda *_: (0,) * nd, pipeline_mode=pl.Buffered(1))


def _layer_spec(shape, layer):
    nd = len(shape)
    return pl.BlockSpec((1,) + tuple(shape[1:]), lambda *_: (layer,) + (0,) * (nd - 1),
                        pipeline_mode=pl.Buffered(1))


def _mlp_stream_kernel(x_ref, wup_ref, wdn_ref, lg_ref, lb_ref, y_ref, xb_scr, acc_scr):
    c = pl.program_id(0)

    @pl.when(c == 0)
    def _():
        xb_scr[...] = x_ref[...].astype(bf16)

    hcol = _dot(xb_scr[...], wup_ref[0])
    part = _dot(jnp.square(jnp.maximum(hcol, 0.0)).astype(bf16), wdn_ref[0])

    @pl.when(c == 0)
    def _():
        acc_scr[...] = part

    @pl.when(c > 0)
    def _():
        acc_scr[...] += part

    @pl.when(c == pl.num_programs(0) - 1)
    def _():
        y_ref[...] = _layer_norm(ALPHA * x_ref[...] + acc_scr[...], lg_ref[0], lb_ref[0])


def _mlp_stream(x2d, wup, wdn, lg, lb, layer, ff_chunk):
    m = x2d.shape[0]
    return pl.pallas_call(
        _mlp_stream_kernel,
        grid=(D_FF // ff_chunk,),
        in_specs=[pl.BlockSpec((m, D), lambda c: (0, 0)),
                  pl.BlockSpec((1, D, ff_chunk), lambda c: (layer, 0, c)),
                  pl.BlockSpec((1, ff_chunk, D), lambda c: (layer, c, 0)),
                  _layer_spec(lg.shape, layer), _layer_spec(lb.shape, layer)],
        out_specs=pl.BlockSpec((m, D), lambda c: (0, 0)),
        out_shape=jax.ShapeDtypeStruct((m, D), f32),
        scratch_shapes=[pltpu.VMEM((m, D), bf16), pltpu.VMEM((m, D), f32)],
        compiler_params=pltpu.CompilerParams(dimension_semantics=("arbitrary",), vmem_limit_bytes=VMEM_LIMIT),
        name="mlp_stream",
    )(x2d, wup, wdn, lg, lb)


def _mlp(x2d, wup, wdn, lg, lb, layer, n_sub, sub_rows):
    m = x2d.shape[0]
    tm = n_sub * sub_rows
    return pl.pallas_call(
        functools.partial(_mlp_kernel, n_sub=n_sub, sub_rows=sub_rows, ff_chunk=1024),
        grid=(m // tm,),
        in_specs=[pl.BlockSpec((tm, D), lambda i: (i, 0)),
                  _layer_spec(wup.shape, layer), _layer_spec(wdn.shape, layer),
                  _layer_spec(lg.shape, layer), _layer_spec(lb.shape, layer)],
        out_specs=pl.BlockSpec((tm, D), lambda i: (i, 0)),
        out_shape=jax.ShapeDtypeStruct((m, D), f32),
        scratch_shapes=[pltpu.VMEM((2, sub_rows, D), bf16), pltpu.VMEM((2, sub_rows, D_FF), bf16)],
        compiler_params=pltpu.CompilerParams(dimension_semantics=("arbitrary",), vmem_limit_bytes=VMEM_LIMIT),
        name="mlp",
    )(x2d, wup, wdn, lg, lb)


def _gla_prompt(x, wq, wgl, wgu, bgate, ng, wo, lg, lb, tb, side):
    bsz, t, _ = x.shape
    n_chunks = tb // KCHUNK
    n_blocks = bsz * t // tb
    n_steps = n_blocks // 2
    steps_per_seq = t // (2 * tb)
    x2d = x.reshape(bsz * t, D)
    side_specs = [pl.BlockSpec((a.shape[0] // n_steps, a.shape[1]), lambda i: (i, 0)) for a in side]
    outs = pl.pallas_call(
        functools.partial(_gla_prompt_kernel, n_chunks=n_chunks, steps_per_seq=steps_per_seq, n_side=len(side)),
        grid=(n_steps,),
        in_specs=[pl.BlockSpec((tb, D), lambda i: (0, 0), pipeline_mode=pl.Buffered(1)),
                  pl.BlockSpec((tb, D), lambda i: (2 * i + 1, 0)),
                  pl.BlockSpec((tb, D), lambda i: (jnp.minimum(2 * i + 2, n_blocks - 1), 0)),
                  pl.BlockSpec((2 * tb, D), lambda i: (i, 0)),
                  _const_spec(wq.shape), _const_spec(wgl.shape), _const_spec(wgu.shape),
                  _const_spec(bgate.shape), _const_spec(ng.shape), _const_spec(wo.shape),
                  _const_spec(lg.shape), _const_spec(lb.shape)] + side_specs,
        out_specs=[pl.BlockSpec((2 * tb, D), lambda i: (i, 0)),
                   pl.BlockSpec((1, H, DKH, DVH), lambda i: (i // steps_per_seq, 0, 0, 0))] + side_specs,
        out_shape=[jax.ShapeDtypeStruct((bsz * t, D), f32),
                   jax.ShapeDtypeStruct((bsz, H, DKH, DVH), f32)]
                  + [jax.ShapeDtypeStruct(a.shape, bf16) for a in side],
        scratch_shapes=[pltpu.VMEM((2, tb, D), bf16),
                        pltpu.VMEM((2, tb, DK), f32), pltpu.VMEM((2, tb, DK), f32),
                        pltpu.VMEM((2, tb, DV), f32), pltpu.VMEM((2, tb, DV), f32),
                        pltpu.VMEM((2, tb, DK), f32),
                        pltpu.VMEM((tb, DK), f32), pltpu.VMEM((tb, DV), f32), pltpu.VMEM((tb, DV), bf16),
                        pltpu.VMEM((H, DKH, DVH), f32)],
        compiler_params=pltpu.CompilerParams(dimension_semantics=("arbitrary",),
                                             vmem_limit_bytes=VMEM_LIMIT),
        name="gla_prompt",
    )(x2d, x2d, x2d, x2d, wq, wgl, wgu, bgate, ng, wo, lg, lb, *side)
    return outs[0].reshape(bsz, t, D), outs[1], outs[2:]


def _gla_sample(x_pad, state, wq, wgl, wgu, bgate, ng, wo, lg, lb, n_seq, t_valid):
    s_total = state.shape[0]
    rows = n_seq * SUB
    return pl.pallas_call(
        functools.partial(_gla_sample_kernel, n_seq=n_seq, t_valid=t_valid),
        grid=(s_total // n_seq,),
        in_specs=[pl.BlockSpec((rows, D), lambda i: (i, 0)),
                  pl.BlockSpec((n_seq, H, DKH, DVH), lambda i: (i, 0, 0, 0)),
                  _const_spec(wq.shape), _const_spec(wgl.shape), _const_spec(wgu.shape),
                  _const_spec(bgate.shape), _const_spec(ng.shape), _const_spec(wo.shape),
                  _const_spec(lg.shape), _const_spec(lb.shape)],
        out_specs=[pl.BlockSpec((rows, D), lambda i: (i, 0)),
                   pl.BlockSpec((n_seq, H, DKH, DVH), lambda i: (i, 0, 0, 0))],
        out_shape=[jax.ShapeDtypeStruct((s_total * SUB, D), f32),
                   jax.ShapeDtypeStruct(state.shape, f32)],
        scratch_shapes=[pltpu.VMEM((rows, DV), f32)],
        compiler_params=pltpu.CompilerParams(dimension_semantics=("arbitrary",), vmem_limit_bytes=VMEM_LIMIT),
        name="gla_sample",
    )(x_pad, state, wq, wgl, wgu, bgate, ng, wo, lg, lb)


def _conv_prompt(x, win, wc, wout, lg, lb, tb):
    bsz, t, _ = x.shape
    n_blocks = bsz * t // tb
    steps_per_seq = t // (2 * tb)
    x2d = x.reshape(bsz * t, D)
    y, st = pl.pallas_call(
        functools.partial(_conv_prompt_kernel, tb=tb, steps_per_seq=steps_per_seq),
        grid=(n_blocks // 2,),
        in_specs=[pl.BlockSpec((tb, D), lambda i: (0, 0), pipeline_mode=pl.Buffered(1)),
                  pl.BlockSpec((tb, D), lambda i: (2 * i + 1, 0)),
                  pl.BlockSpec((tb, D), lambda i: (jnp.minimum(2 * i + 2, n_blocks - 1), 0)),
                  pl.BlockSpec((2 * tb, D), lambda i: (i, 0)),
                  _const_spec(win.shape), _const_spec(wc.shape), _const_spec(wout.shape),
                  _const_spec(lg.shape), _const_spec(lb.shape)],
        out_specs=[pl.BlockSpec((2 * tb, D), lambda i: (i, 0)),
                   pl.BlockSpec((1, 2, D), lambda i: (i // steps_per_seq, 0, 0))],
        out_shape=[jax.ShapeDtypeStruct((bsz * t, D), f32),
                   jax.ShapeDtypeStruct((bsz, 2, D), f32)],
        scratch_shapes=[pltpu.VMEM((2, tb, D), bf16), pltpu.VMEM((2, tb, D), f32),
                        pltpu.VMEM((tb, D), f32), pltpu.VMEM((2, tb, D), f32),
                        pltpu.VMEM((tb, D), bf16), pltpu.VMEM((SUB, D), f32)],
        compiler_params=pltpu.CompilerParams(dimension_semantics=("arbitrary",), vmem_limit_bytes=VMEM_LIMIT),
        name="conv_prompt",
    )(x2d, x2d, x2d, x2d, win, wc, wout, lg, lb)
    return y.reshape(bsz, t, D), st


def _conv_sample(x2d, e1, e2, win, wc, wout, lg, lb, t_seq):
    m = x2d.shape[0]
    n_proj = win.shape[1] // D
    full = lambda shape: pl.BlockSpec(shape, lambda i: (0,) * len(shape))
    return pl.pallas_call(
        functools.partial(_conv_sample_kernel, t_seq=t_seq),
        grid=(n_proj + 1,),
        in_specs=[full((m, D)), full((m, D)), full((m, D)),
                  pl.BlockSpec((D, D), lambda c: (0, jnp.minimum(c, n_proj - 1))),
                  _const_spec(wc.shape), _const_spec(wout.shape),
                  _const_spec(lg.shape), _const_spec(lb.shape)],
        out_specs=[full((m, D)), full((m, D))],
        out_shape=[jax.ShapeDtypeStruct((m, D), f32), jax.ShapeDtypeStruct((m, D), f32)],
        scratch_shapes=[pltpu.VMEM((m, D), bf16), pltpu.VMEM((n_proj, m, D), f32)],
        compiler_params=pltpu.CompilerParams(dimension_semantics=("arbitrary",), vmem_limit_bytes=VMEM_LIMIT),
        name="conv_sample",
    )(x2d, e1, e2, win, wc, wout, lg, lb)


def kernel(x_prompt, x_sample, state_gla, state_conv, gla_w_in, gla_w_gate_up, gla_b_gate, gla_norm_g, gla_w_o, conv_w_in, conv_w_conv, conv_w_out, mlp_w_up, mlp_w_down, ln1_g, ln1_b, ln2_g, ln2_b):
    bsz, t, _ = x_prompt.shape
    n_dec, t_dec, _ = x_sample.shape
    assert t % CHUNK == 0 and t_dec <= SUB and t_dec >= 2

    w_in = gla_w_in[0]
    wq = w_in.astype(bf16)
    wgl = jnp.pad(w_in[:, 2 * DK + 2 * DV:], ((0, 0), (0, RANK_PAD - RANK))).astype(bf16)
    wgu = jnp.pad(gla_w_gate_up[0], ((0, RANK_PAD - RANK), (0, 0))).astype(bf16)
    bgate = gla_b_gate[0].reshape(1, DK)
    ng = gla_norm_g[0].reshape(1, DV)
    wo = gla_w_o[0].astype(bf16)
    cwc = conv_w_conv[0]
    ln2g = ln2_g.reshape(DEPTH, 1, D)
    ln2b = ln2_b.reshape(DEPTH, 1, D)
    row = lambda a, i: a[i].reshape(1, D)
    mlp_prompt = functools.partial(_mlp, n_sub=4, sub_rows=256)
    mlp_sample = functools.partial(_mlp_stream, ff_chunk=1024)

    side = (mlp_w_up.reshape(DEPTH * D, D_FF), mlp_w_down.reshape(DEPTH * D_FF, D), conv_w_in[0], conv_w_out[0])
    xp, gla_p, (wup, wdn, cwin, cwout) = _gla_prompt(x_prompt, wq, wgl, wgu, bgate, ng, wo, row(ln1_g, 0),
                                                     row(ln1_b, 0), tb=256, side=side)
    wup = wup.reshape(DEPTH, D, D_FF)
    wdn = wdn.reshape(DEPTH, D_FF, D)
    xs_pad = jnp.pad(x_sample, ((0, 0), (0, SUB - t_dec), (0, 0))).reshape(n_dec * SUB, D)
    xs_pad, gla_s = _gla_sample(xs_pad, state_gla[0], wq, wgl, wgu, bgate, ng, wo, row(ln1_g, 0), row(ln1_b, 0),
                                n_seq=16, t_valid=t_dec)
    xs = xs_pad.reshape(n_dec, SUB, D)[:, :t_dec].reshape(n_dec * t_dec, D)
    xp = mlp_prompt(xp.reshape(bsz * t, D), wup, wdn, ln2g, ln2b, layer=0)
    xs = mlp_sample(xs, wup, wdn, ln2g, ln2b, layer=0)

    xp, conv_p = _conv_prompt(xp.reshape(bsz, t, D), cwin, cwc, cwout, row(ln1_g, 1), row(ln1_b, 1), tb=512)
    buf = state_conv[0]
    e1 = jnp.pad(buf[:, 1:2], ((0, 0), (0, t_dec - 1), (0, 0))).reshape(n_dec * t_dec, D)
    e2 = jnp.pad(buf, ((0, 0), (0, t_dec - 2), (0, 0))).reshape(n_dec * t_dec, D)
    xs, u_s = _conv_sample(xs, e1, e2, cwin, cwc, cwout, row(ln1_g, 1), row(ln1_b, 1), t_seq=t_dec)
    conv_s = u_s.reshape(n_dec, t_dec, D)[:, t_dec - 2:]
    xp = mlp_prompt(xp.reshape(bsz * t, D), wup, wdn, ln2g, ln2b, layer=1)
    xs = mlp_sample(xs, wup, wdn, ln2g, ln2b, layer=1)

    return (xp.reshape(bsz, t, D), xs.reshape(n_dec, t_dec, D), gla_p[None], gla_s[None],
            conv_p[None], conv_s[None])
```

```python
import functools

import jax
import jax.numpy as jnp
from jax import lax
from jax.experimental import pallas as pl
from jax.experimental.pallas import tpu as pltpu

bf16 = jnp.bfloat16
f32 = jnp.float32

D = 1024
H = 4
DK = 512
DV = 1024
DKH = DK // H
DVH = DV // H
RANK = 16
RANK_PAD = 128
TAU = 16.0
CHUNK = 64
KCHUNK = 128
D_FF = 4 * D
DEPTH = 2
ALPHA = (2 * DEPTH) ** 0.25
LN_EPS = 1e-5
RMS_EPS = 1e-6
Q_SCALE = DKH ** -0.5

SUB = 8
VMEM_LIMIT = 57 * 1024 * 1024

_NT = (((1,), (1,)), ((), ()))
_TN = (((0,), (0,)), ((), ()))


def _dot(a, b):
    return jnp.dot(a, b, preferred_element_type=f32)


def _dg(a, b, dims):
    return lax.dot_general(a, b, dims, preferred_element_type=f32)


def _layer_norm(y, g, b):
    mu = jnp.mean(y, axis=-1, keepdims=True)
    yc = y - mu
    var = jnp.mean(yc * yc, axis=-1, keepdims=True)
    return yc * lax.rsqrt(var + LN_EPS) * g + b


def _log_sigmoid(z):
    return -(jnp.maximum(-z, 0.0) + jnp.log(1.0 + jnp.exp(-jnp.abs(z))))


def _gla_project(xb, wq_ref, wgl_ref, wgu_ref, bgate_ref):
    q = _dot(xb, wq_ref[:, 0:DK]) * Q_SCALE
    k = _dot(xb, wq_ref[:, DK:2 * DK])
    v = _dot(xb, wq_ref[:, 2 * DK:2 * DK + DV])
    r = _dot(xb, wq_ref[:, 2 * DK + DV:2 * DK + 2 * DV])
    gl = _dot(xb, wgl_ref[...])
    z = _dot(gl.astype(bf16), wgu_ref[...]) + bgate_ref[...]
    g = _log_sigmoid(z) / TAU
    return q, k, v, r, g


def _gla_post(o, r, x, ng, wo_ref, lg, lb):
    parts = []
    for h in range(H):
        vs = slice(h * DVH, (h + 1) * DVH)
        oh = o[:, vs]
        ms = jnp.mean(oh * oh, axis=-1, keepdims=True)
        parts.append(oh * lax.rsqrt(ms + RMS_EPS) * ng[:, vs])
    on = jnp.concatenate(parts, axis=1)
    gated = on * (r * jax.nn.sigmoid(r))
    y = _dot(gated.astype(bf16), wo_ref[...])
    return _layer_norm(ALPHA * x + y, lg, lb)


PROJ_TILE = 512
W_VPU, W_NORM, W_OUT, W_MXU = 1.0, 1.0, 1.0, 1.0


def _interleave(main_tasks, filler_tasks):
    total = sum(w for _, w in main_tasks)
    n_fill = len(filler_tasks)
    done, acc = 0, 0.0
    for task, w in main_tasks:
        task()
        acc += w
        want = int(round(acc * n_fill / total))
        while done < want:
            filler_tasks[done]()
            done += 1


def _gla_stage1_tasks(x_ref, slot, w_refs, scr):
    wq_ref, wgl_ref, wgu_ref, bgate_ref = w_refs
    xb_scr, q_scr, k_scr, v_scr, r_scr, g_scr = scr[:6]

    def cast_x():
        xb_scr[slot] = x_ref[...].astype(bf16)

    def proj(dst, c0, w0, scale):
        def run():
            acc = _dot(xb_scr[slot], wq_ref[:, w0:w0 + PROJ_TILE])
            dst[slot, :, c0:c0 + PROJ_TILE] = acc * scale if scale is not None else acc
        return run

    def gate():
        gl = _dot(xb_scr[slot], wgl_ref[...])
        z = _dot(gl.astype(bf16), wgu_ref[...]) + bgate_ref[...]
        g_scr[slot] = _log_sigmoid(z) / TAU

    tiles = []
    for dst, w_base, width, scale in ((q_scr, 0, DK, Q_SCALE), (k_scr, DK, DK, None),
                                      (v_scr, 2 * DK, DV, None), (r_scr, 2 * DK + DV, DV, None)):
        for c0 in range(0, width, PROJ_TILE):
            tiles.append(proj(dst, c0, w_base + c0, scale))
    return cast_x, gate, tiles


def _split3(a):
    a1 = a.astype(bf16)
    r1 = a - a1.astype(f32)
    a2 = r1.astype(bf16)
    a3 = (r1 - a2.astype(f32)).astype(bf16)
    return a1, a2, a3


def _gla_stage2_tasks(xres_ref, y_ref, r0, slot, scr, s_scr, ng_ref, wo_ref, lg_ref, lb_ref, n_chunks):
    _, q_scr, k_scr, v_scr, r_scr, g_scr, b_scr, o_scr, gated_scr = scr
    tb = n_chunks * KCHUNK
    mid = KCHUNK // 2

    def cumsum():
        row = lax.broadcasted_iota(jnp.int32, (tb, tb), 0)
        col = lax.broadcasted_iota(jnp.int32, (tb, tb), 1)
        tri = ((row >= col) & ((row // KCHUNK) == (col // KCHUNK))).astype(bf16)
        g1, g2, g3 = _split3(g_scr[slot])
        b_scr[...] = _dot(tri, g1) + _dot(tri, g2) + _dot(tri, g3)

    def chunk_prep(c, cell):
        rs = slice(c * KCHUNK, (c + 1) * KCHUNK)

        def run():
            b = b_scr[rs, :]
            b_mid = b[mid - 1:mid, :]
            b_last = b[KCHUNK - 1:KCHUNK, :]
            qc = q_scr[slot, rs, :]
            kc = k_scr[slot, rs, :]
            cell["qs"] = (qc * jnp.exp(b - b_mid)).astype(bf16)
            cell["ks"] = (kc * jnp.exp(b_mid - b)).astype(bf16)
            cell["qd"] = (qc * jnp.exp(b)).astype(bf16)
            cell["kk"] = (kc * jnp.exp(b_last - b)).astype(bf16)
            cell["vb"] = v_scr[slot, rs, :].astype(bf16)
            cell["dec_t"] = jnp.exp(jnp.broadcast_to(b_last, (KCHUNK, DK)).T)
        return run

    def head_scores(h, cell):
        ks = slice(h * DKH, (h + 1) * DKH)

        def run():
            ri = lax.broadcasted_iota(jnp.int32, (KCHUNK, KCHUNK), 0)
            ci = lax.broadcasted_iota(jnp.int32, (KCHUNK, KCHUNK), 1)
            sc = _dg(cell["qs"][:, ks], cell["ks"][:, ks], _NT)
            cell["sc", h] = jnp.where(ri >= ci, sc, 0.0).astype(bf16)
        return run

    def head_update(c, h, cell):
        rs = slice(c * KCHUNK, (c + 1) * KCHUNK)
        ks = slice(h * DKH, (h + 1) * DKH)
        vs = slice(h * DVH, (h + 1) * DVH)

        def run():
            st = s_scr[h]
            vh = cell["vb"][:, vs]
            lhs = jnp.concatenate([cell["sc", h], cell["qd"][:, ks]], axis=1)
            rhs = jnp.concatenate([vh, st.astype(bf16)], axis=0)
            o_scr[rs, vs] = _dot(lhs, rhs)
            dec_h = cell["dec_t"][ks, :]
            dec_m = jnp.concatenate([dec_h] * (DVH // KCHUNK), axis=1)
            s_scr[h] = dec_m * st + _dg(cell["kk"][:, ks], vh, _TN)
        return run

    def norm_gate(h):
        vs = slice(h * DVH, (h + 1) * DVH)

        def run():
            oh = o_scr[:, vs]
            rh = r_scr[slot, :, vs]
            ms = jnp.mean(oh * oh, axis=-1, keepdims=True)
            on = oh * lax.rsqrt(ms + RMS_EPS) * ng_ref[:, vs]
            gated_scr[:, vs] = (on * (rh * jax.nn.sigmoid(rh))).astype(bf16)
        return run

    def out_rows(m0, m1):
        def run():
            y = _dot(gated_scr[m0:m1, :], wo_ref[...])
            x = xres_ref[r0 + m0:r0 + m1, :]
            y_ref[r0 + m0:r0 + m1, :] = _layer_norm(ALPHA * x + y, lg_ref[...], lb_ref[...])
        return run

    tasks = [(cumsum, W_VPU)]
    for c in range(n_chunks):
        cell = {}
        tasks.append((chunk_prep(c, cell), W_VPU))
        tasks += [(head_scores(h, cell), W_MXU) for h in range(H)]
        tasks += [(head_update(c, h, cell), W_MXU) for h in range(H)]
    tasks += [(norm_gate(h), W_NORM) for h in range(H)]
    half = tb // 2
    tasks += [(out_rows(0, half), W_OUT), (out_rows(half, tb), W_OUT)]
    return tasks


def _gla_prompt_kernel(x0_ref, xa_ref, xb_ref, xres_ref, wq_ref, wgl_ref, wgu_ref, bgate_ref, ng_ref, wo_ref,
                       lg_ref, lb_ref, *refs, n_chunks, steps_per_seq, n_side):
    side_in = refs[:n_side]
    y_ref, st_ref = refs[n_side:n_side + 2]
    side_out = refs[n_side + 2:2 * n_side + 2]
    (xb_scr, q_scr, k_scr, v_scr, r_scr, g_scr, b_scr, o_scr, gated_scr, s_scr) = refs[2 * n_side + 2:]
    i = pl.program_id(0)
    tb = n_chunks * KCHUNK
    w_refs = (wq_ref, wgl_ref, wgu_ref, bgate_ref)
    scr = (xb_scr, q_scr, k_scr, v_scr, r_scr, g_scr, b_scr, o_scr, gated_scr)
    post = (ng_ref, wo_ref, lg_ref, lb_ref)

    @pl.when(i == 0)
    def _():
        cast_x, gate, tiles = _gla_stage1_tasks(x0_ref, 0, w_refs, scr)
        for task in [cast_x, gate] + tiles:
            task()

    @pl.when(i % steps_per_seq == 0)
    def _():
        s_scr[...] = jnp.zeros_like(s_scr)

    def side_cast(src_ref, dst_ref):
        def run():
            dst_ref[...] = src_ref[...].astype(bf16)
        return run

    side_tasks = [side_cast(s, d) for s, d in zip(side_in, side_out)]
    for r0, slot, x_next in ((0, 0, xa_ref), (tb, 1, xb_ref)):
        cast_x, gate, tiles = _gla_stage1_tasks(x_next, 1 - slot, w_refs, scr)
        stage2 = _gla_stage2_tasks(xres_ref, y_ref, r0, slot, scr, s_scr, *post, n_chunks)
        cast_x()
        fillers = tiles + side_tasks[slot::2]
        _interleave([stage2[0], (gate, W_VPU)] + stage2[1:], fillers)

    @pl.when(i % steps_per_seq == steps_per_seq - 1)
    def _():
        for h in range(H):
            st_ref[0, h] = s_scr[h]


def _gla_sample_kernel(x_ref, st_ref, wq_ref, wgl_ref, wgu_ref, bgate_ref, ng_ref, wo_ref, lg_ref, lb_ref,
                       y_ref, sto_ref, o_scr, *, n_seq, t_valid):
    rows = n_seq * SUB
    x = x_ref[...]
    q, k, v, r, g = _gla_project(x.astype(bf16), wq_ref, wgl_ref, wgu_ref, bgate_ref)

    row = lax.broadcasted_iota(jnp.int32, (rows, rows), 0)
    col = lax.broadcasted_iota(jnp.int32, (rows, rows), 1)
    same_seq = (row // SUB) == (col // SUB)
    tri = (same_seq & (row >= col)).astype(f32)
    tri_last = (same_seq & ((col % SUB) < t_valid)).astype(f32)
    b = jnp.dot(tri, g, precision=lax.Precision.HIGHEST, preferred_element_type=f32)
    b_last = jnp.dot(tri_last, g, precision=lax.Precision.HIGHEST, preferred_element_type=f32)

    valid = (lax.broadcasted_iota(jnp.int32, (rows, DK), 0) % SUB) < t_valid
    qd = (q * jnp.exp(b)).astype(bf16)
    kd = jnp.where(valid, k * jnp.exp(-b), 0.0).astype(bf16)
    kk_t = jnp.where(valid, k * jnp.exp(b_last - b), 0.0).T.astype(bf16)
    vb = v.astype(bf16)
    dec_t = jnp.exp(b_last.T)

    intra_mask = same_seq & (row >= col)
    for h in range(H):
        ks = slice(h * DKH, (h + 1) * DKH)
        vs = slice(h * DVH, (h + 1) * DVH)
        sc = jnp.where(intra_mask, _dg(qd[:, ks], kd[:, ks], _NT), 0.0).astype(bf16)
        o_scr[:, vs] = _dot(sc, vb[:, vs])

    seq_of_col = lax.broadcasted_iota(jnp.int32, (DKH, rows), 1) // SUB

    def seq_dots(n):
        rs = slice(n * SUB, (n + 1) * SUB)
        res = []
        for h in range(H):
            ks = slice(h * DKH, (h + 1) * DKH)
            vs = slice(h * DVH, (h + 1) * DVH)
            s0 = st_ref[n, h]
            o_inter = _dot(qd[rs, ks], s0.astype(bf16))
            upd = _dot(jnp.where(seq_of_col == n, kk_t[ks, :], jnp.zeros((), bf16)), vb[:, vs])
            res.append((s0, o_inter, upd))
        return res

    def seq_combine(n, res):
        rs = slice(n * SUB, (n + 1) * SUB)
        for h, (s0, o_inter, upd) in enumerate(res):
            ks = slice(h * DKH, (h + 1) * DKH)
            vs = slice(h * DVH, (h + 1) * DVH)
            o_scr[rs, vs] = o_scr[rs, vs] + o_inter
            sto_ref[n, h] = dec_t[ks, n * SUB:n * SUB + 1] * s0 + upd

    pending = None
    for n in range(n_seq):
        res = seq_dots(n)
        if pending is not None:
            seq_combine(*pending)
        pending = (n, res)
    seq_combine(*pending)

    y_ref[...] = _gla_post(o_scr[...], r, x, ng_ref[...], wo_ref, lg_ref[...], lb_ref[...])


def _conv_prompt_kernel(x0_ref, xa_ref, xb_ref, xe_ref, ebuf_ref, win_ref, wc_ref, wout_ref,
                        lg_ref, lb_ref, y_ref, st_ref, ye_ref, ue_ref, x_scr, xb_scr, bg_scr, cg_scr, u_scr,
                        c_scr, *, tb, steps_per_seq, te_seq):
    i = pl.program_id(0)
    half = tb // 2
    col_tile = D // 2

    def stage1_tasks(x_ref, slot):
        def cast():
            x = x_ref[...]
            x_scr[slot] = x
            xb_scr[slot] = x.astype(bf16)

        def proj_bg(c0):
            def run():
                bg_scr[slot, :, c0:c0 + col_tile] = _dot(xb_scr[slot], win_ref[:, c0:c0 + col_tile])
            return run

        def proj_cg(c0):
            def run():
                cg_scr[:, c0:c0 + col_tile] = _dot(xb_scr[slot], win_ref[:, D + c0:D + c0 + col_tile])
            return run

        def proj_u(c0):
            def run():
                hh = _dot(xb_scr[slot], win_ref[:, 2 * D + c0:2 * D + c0 + col_tile])
                u_scr[slot, :, c0:c0 + col_tile] = cg_scr[:, c0:c0 + col_tile] * hh
            return run

        cols = range(0, D, col_tile)
        return cast, ([proj_bg(c) for c in cols] + [proj_cg(c) for c in cols] + [proj_u(c) for c in cols])

    def stage2_tasks(r0, slot):
        def conv(m0):
            def run():
                u = u_scr[slot, m0:m0 + half, :]
                t = lax.broadcasted_iota(jnp.int32, (half, D), 0)
                c0 = c_scr[0:1, :]
                c1 = c_scr[1:2, :]
                p1 = jnp.where(t == 0, c1, pltpu.roll(u, 1, 0))
                p2 = jnp.where(t == 0, c0, jnp.where(t == 1, c1, pltpu.roll(u, 2, 0)))
                cv = p2 * wc_ref[0:1, :] + p1 * wc_ref[1:2, :] + u * wc_ref[2:3, :]
                xb_scr[slot, m0:m0 + half, :] = (bg_scr[slot, m0:m0 + half, :] * cv).astype(bf16)
                c_scr[0:2, :] = u[half - 2:half, :]
            return run

        def out(m0):
            def run():
                y = _dot(xb_scr[slot, m0:m0 + half, :], wout_ref[...])
                x = x_scr[slot, m0:m0 + half, :]
                y_ref[r0 + m0:r0 + m0 + half, :] = _layer_norm(ALPHA * x + y, lg_ref[...], lb_ref[...])
            return run

        return [(conv(0), 1.0), (conv(half), 1.0), (out(0), 1.0), (out(half), 1.0)]

    @pl.when(i == 0)
    def _():
        cast, tiles = stage1_tasks(x0_ref, 0)
        cast()
        for task in tiles:
            task()

    @pl.when(i % steps_per_seq == 0)
    def _():
        c_scr[...] = jnp.zeros_like(c_scr)

    for r0, slot, x_next in ((0, 0, xa_ref), (tb, 1, xb_ref)):
        cast, tiles = stage1_tasks(x_next, 1 - slot)
        cast()
        _interleave(stage2_tasks(r0, slot), tiles)

    st_ref[0] = c_scr[0:2, :]

    @pl.when(i == pl.num_programs(0) - 1)
    def _():
        xb_scr[0] = xe_ref[...].astype(bf16)
        bg_scr[0] = _dot(xb_scr[0], win_ref[:, 0:D])
        cg_scr[...] = _dot(xb_scr[0], win_ref[:, D:2 * D])
        ue_ref[...] = cg_scr[...] * _dot(xb_scr[0], win_ref[:, 2 * D:3 * D])
        u = ue_ref[...]
        t = lax.broadcasted_iota(jnp.int32, u.shape, 0) % te_seq
        ebuf = ebuf_ref[...]
        p1 = jnp.where(t >= 1, pltpu.roll(u, 1, 0), pltpu.roll(ebuf, u.shape[0] - 1, 0))
        p2 = jnp.where(t >= 2, pltpu.roll(u, 2, 0), ebuf)
        cv = p2 * wc_ref[0:1, :] + p1 * wc_ref[1:2, :] + u * wc_ref[2:3, :]
        xb_scr[1] = (bg_scr[0] * cv).astype(bf16)
        y = _dot(xb_scr[1], wout_ref[...])
        ye_ref[...] = _layer_norm(ALPHA * xe_ref[...] + y, lg_ref[...], lb_ref[...])


def _mlp_kernel(x_ref, xe_ref, wup_ref, wdn_ref, lg_ref, lb_ref, y_ref, ye_ref, xb_scr, h_scr,
                *, n_sub, sub_rows, ff_chunk):
    def run_rows(src_ref, dst_ref, n_blocks):
        def sub_tasks(s):
            rs = slice(s * sub_rows, (s + 1) * sub_rows)
            buf = s % 2

            def cast():
                xb_scr[buf] = src_ref[rs, :].astype(bf16)

            def up(c):
                cs = slice(c * ff_chunk, (c + 1) * ff_chunk)

                def run():
                    hcol = _dot(xb_scr[buf], wup_ref[0, :, cs])
                    h_scr[buf, :, cs] = jnp.square(jnp.maximum(hcol, 0.0)).astype(bf16)
                return run

            def down():
                y = _dot(h_scr[buf], wdn_ref[0])
                dst_ref[rs, :] = _layer_norm(ALPHA * src_ref[rs, :] + y, lg_ref[0], lb_ref[0])

            return [cast] + [up(c) for c in range(D_FF // ff_chunk)], down

        pending = None
        for s in range(n_blocks):
            ups, down = sub_tasks(s)
            ups[0]()
            ups[1]()
            if pending is not None:
                pending()
            for task in ups[2:]:
                task()
            pending = down
        pending()

    run_rows(x_ref, y_ref, n_sub)

    @pl.when(pl.program_id(0) == pl.num_programs(0) - 1)
    def _():
        run_rows(xe_ref, ye_ref, xe_ref.shape[0] // sub_rows)


def _const_spec(shape):
    nd = len(shape)
    return pl.BlockSpec(shape, lambda *_: (0,) * nd, pipeline_mode=pl.Buffered(1))


def _layer_spec(shape, layer):
    nd = len(shape)
    return pl.BlockSpec((1,) + tuple(shape[1:]), lambda *_: (layer,) + (0,) * (nd - 1),
                        pipeline_mode=pl.Buffered(1))


def _mlp(x2d, xe2d, wup, wdn, lg, lb, layer, n_sub, sub_rows):
    m = x2d.shape[0]
    me = xe2d.shape[0]
    assert me % sub_rows == 0
    tm = n_sub * sub_rows
    return pl.pallas_call(
        functools.partial(_mlp_kernel, n_sub=n_sub, sub_rows=sub_rows, ff_chunk=1024),
        grid=(m // tm,),
        in_specs=[pl.BlockSpec((tm, D), lambda i: (i, 0)), _const_spec((me, D)),
                  _layer_spec(wup.shape, layer), _layer_spec(wdn.shape, layer),
                  _layer_spec(lg.shape, layer), _layer_spec(lb.shape, layer)],
        out_specs=[pl.BlockSpec((tm, D), lambda i: (i, 0)), pl.BlockSpec((me, D), lambda i: (0, 0))],
        out_shape=[jax.ShapeDtypeStruct((m, D), f32), jax.ShapeDtypeStruct((me, D), f32)],
        scratch_shapes=[pltpu.VMEM((2, sub_rows, D), bf16), pltpu.VMEM((2, sub_rows, D_FF), bf16)],
        compiler_params=pltpu.CompilerParams(dimension_semantics=("arbitrary",), vmem_limit_bytes=VMEM_LIMIT),
        name="mlp",
    )(x2d, xe2d, wup, wdn, lg, lb)


def _gla_prompt(x, wq, wgl, wgu, bgate, ng, wo, lg, lb, tb, side):
    bsz, t, _ = x.shape
    n_chunks = tb // KCHUNK
    n_blocks = bsz * t // tb
    n_steps = n_blocks // 2
    steps_per_seq = t // (2 * tb)
    x2d = x.reshape(bsz * t, D)
    side_specs = [pl.BlockSpec((a.shape[0] // n_steps, a.shape[1]), lambda i: (i, 0)) for a in side]
    outs = pl.pallas_call(
        functools.partial(_gla_prompt_kernel, n_chunks=n_chunks, steps_per_seq=steps_per_seq, n_side=len(side)),
        grid=(n_steps,),
        in_specs=[pl.BlockSpec((tb, D), lambda i: (0, 0), pipeline_mode=pl.Buffered(1)),
                  pl.BlockSpec((tb, D), lambda i: (2 * i + 1, 0)),
                  pl.BlockSpec((tb, D), lambda i: (jnp.minimum(2 * i + 2, n_blocks - 1), 0)),
                  pl.BlockSpec((2 * tb, D), lambda i: (i, 0)),
                  _const_spec(wq.shape), _const_spec(wgl.shape), _const_spec(wgu.shape),
                  _const_spec(bgate.shape), _const_spec(ng.shape), _const_spec(wo.shape),
                  _const_spec(lg.shape), _const_spec(lb.shape)] + side_specs,
        out_specs=[pl.BlockSpec((2 * tb, D), lambda i: (i, 0)),
                   pl.BlockSpec((1, H, DKH, DVH), lambda i: (i // steps_per_seq, 0, 0, 0))] + side_specs,
        out_shape=[jax.ShapeDtypeStruct((bsz * t, D), f32),
                   jax.ShapeDtypeStruct((bsz, H, DKH, DVH), f32)]
                  + [jax.ShapeDtypeStruct(a.shape, bf16) for a in side],
        scratch_shapes=[pltpu.VMEM((2, tb, D), bf16),
                        pltpu.VMEM((2, tb, DK), f32), pltpu.VMEM((2, tb, DK), f32),
                        pltpu.VMEM((2, tb, DV), f32), pltpu.VMEM((2, tb, DV), f32),
                        pltpu.VMEM((2, tb, DK), f32),
                        pltpu.VMEM((tb, DK), f32), pltpu.VMEM((tb, DV), f32), pltpu.VMEM((tb, DV), bf16),
                        pltpu.VMEM((H, DKH, DVH), f32)],
        compiler_params=pltpu.CompilerParams(dimension_semantics=("arbitrary",),
                                             vmem_limit_bytes=VMEM_LIMIT),
        name="gla_prompt",
    )(x2d, x2d, x2d, x2d, wq, wgl, wgu, bgate, ng, wo, lg, lb, *side)
    return outs[0].reshape(bsz, t, D), outs[1], outs[2:]


def _gla_sample(x_pad, state, wq, wgl, wgu, bgate, ng, wo, lg, lb, n_seq, t_valid):
    s_total = state.shape[0]
    rows = n_seq * SUB
    return pl.pallas_call(
        functools.partial(_gla_sample_kernel, n_seq=n_seq, t_valid=t_valid),
        grid=(s_total // n_seq,),
        in_specs=[pl.BlockSpec((rows, D), lambda i: (i, 0)),
                  pl.BlockSpec((n_seq, H, DKH, DVH), lambda i: (i, 0, 0, 0)),
                  _const_spec(wq.shape), _const_spec(wgl.shape), _const_spec(wgu.shape),
                  _const_spec(bgate.shape), _const_spec(ng.shape), _const_spec(wo.shape),
                  _const_spec(lg.shape), _const_spec(lb.shape)],
        out_specs=[pl.BlockSpec((rows, D), lambda i: (i, 0)),
                   pl.BlockSpec((n_seq, H, DKH, DVH), lambda i: (i, 0, 0, 0))],
        out_shape=[jax.ShapeDtypeStruct((s_total * SUB, D), f32),
                   jax.ShapeDtypeStruct(state.shape, f32)],
        scratch_shapes=[pltpu.VMEM((rows, DV), f32)],
        compiler_params=pltpu.CompilerParams(dimension_semantics=("arbitrary",), vmem_limit_bytes=VMEM_LIMIT),
        name="gla_sample",
    )(x_pad, state, wq, wgl, wgu, bgate, ng, wo, lg, lb)


def _conv_prompt(x, xe2d, ebuf, win, wc, wout, lg, lb, tb, te_seq):
    bsz, t, _ = x.shape
    me = xe2d.shape[0]
    n_blocks = bsz * t // tb
    steps_per_seq = t // (2 * tb)
    x2d = x.reshape(bsz * t, D)
    y, st, ye, ue = pl.pallas_call(
        functools.partial(_conv_prompt_kernel, tb=tb, steps_per_seq=steps_per_seq, te_seq=te_seq),
        grid=(n_blocks // 2,),
        in_specs=[pl.BlockSpec((tb, D), lambda i: (0, 0), pipeline_mode=pl.Buffered(1)),
                  pl.BlockSpec((tb, D), lambda i: (2 * i + 1, 0)),
                  pl.BlockSpec((tb, D), lambda i: (jnp.minimum(2 * i + 2, n_blocks - 1), 0)),
                  _const_spec((me, D)), _const_spec((me, D)),
                  _const_spec(win.shape), _const_spec(wc.shape), _const_spec(wout.shape),
                  _const_spec(lg.shape), _const_spec(lb.shape)],
        out_specs=[pl.BlockSpec((2 * tb, D), lambda i: (i, 0)),
                   pl.BlockSpec((1, 2, D), lambda i: (i // steps_per_seq, 0, 0)),
                   _const_spec((me, D)), _const_spec((me, D))],
        out_shape=[jax.ShapeDtypeStruct((bsz * t, D), f32),
                   jax.ShapeDtypeStruct((bsz, 2, D), f32),
                   jax.ShapeDtypeStruct((me, D), f32), jax.ShapeDtypeStruct((me, D), f32)],
        scratch_shapes=[pltpu.VMEM((2, tb, D), f32),
                        pltpu.VMEM((2, tb, D), bf16), pltpu.VMEM((2, tb, D), f32),
                        pltpu.VMEM((tb, D), f32), pltpu.VMEM((2, tb, D), f32),
                        pltpu.VMEM((SUB, D), f32)],
        compiler_params=pltpu.CompilerParams(dimension_semantics=("arbitrary",), vmem_limit_bytes=VMEM_LIMIT),
        name="conv_prompt",
    )(x2d, x2d, x2d, xe2d, ebuf, win, wc, wout, lg, lb)
    return y.reshape(bsz, t, D), st, ye, ue


def kernel(x_prompt, x_sample, state_gla, state_conv, gla_w_in, gla_w_gate_up, gla_b_gate, gla_norm_g, gla_w_o, conv_w_in, conv_w_conv, conv_w_out, mlp_w_up, mlp_w_down, ln1_g, ln1_b, ln2_g, ln2_b):
    bsz, t, _ = x_prompt.shape
    n_dec, t_dec, _ = x_sample.shape
    assert t % CHUNK == 0 and t_dec <= SUB and t_dec >= 2

    w_in = gla_w_in[0]
    wq = w_in.astype(bf16)
    wgl = jnp.pad(w_in[:, 2 * DK + 2 * DV:], ((0, 0), (0, RANK_PAD - RANK))).astype(bf16)
    wgu = jnp.pad(gla_w_gate_up[0], ((0, RANK_PAD - RANK), (0, 0))).astype(bf16)
    bgate = gla_b_gate[0].reshape(1, DK)
    ng = gla_norm_g[0].reshape(1, DV)
    wo = gla_w_o[0].astype(bf16)
    cwc = conv_w_conv[0]
    ln2g = ln2_g.reshape(DEPTH, 1, D)
    ln2b = ln2_b.reshape(DEPTH, 1, D)
    row = lambda a, i: a[i].reshape(1, D)
    mlp_both = functools.partial(_mlp, n_sub=4, sub_rows=256)

    side = (mlp_w_up.reshape(DEPTH * D, D_FF), mlp_w_down.reshape(DEPTH * D_FF, D), conv_w_in[0], conv_w_out[0])
    xp, gla_p, (wup, wdn, cwin, cwout) = _gla_prompt(x_prompt, wq, wgl, wgu, bgate, ng, wo, row(ln1_g, 0),
                                                     row(ln1_b, 0), tb=256, side=side)
    wup = wup.reshape(DEPTH, D, D_FF)
    wdn = wdn.reshape(DEPTH, D_FF, D)
    xs_pad = jnp.pad(x_sample, ((0, 0), (0, SUB - t_dec), (0, 0))).reshape(n_dec * SUB, D)
    xs_pad, gla_s = _gla_sample(xs_pad, state_gla[0], wq, wgl, wgu, bgate, ng, wo, row(ln1_g, 0), row(ln1_b, 0),
                                n_seq=16, t_valid=t_dec)
    xs = xs_pad.reshape(n_dec, SUB, D)[:, :t_dec].reshape(n_dec * t_dec, D)
    xp, xs = mlp_both(xp.reshape(bsz * t, D), xs, wup, wdn, ln2g, ln2b, layer=0)

    buf = state_conv[0]
    ebuf = jnp.pad(buf, ((0, 0), (0, t_dec - 2), (0, 0))).reshape(n_dec * t_dec, D)
    xp, conv_p, xs, u_s = _conv_prompt(xp.reshape(bsz, t, D), xs, ebuf, cwin, cwc, cwout, row(ln1_g, 1),
                                       row(ln1_b, 1), tb=512, te_seq=t_dec)
    conv_s = u_s.reshape(n_dec, t_dec, D)[:, t_dec - 2:]
    xp, xs = mlp_both(xp.reshape(bsz * t, D), xs, wup, wdn, ln2g, ln2b, layer=1)

    return (xp.reshape(bsz, t, D), xs.reshape(n_dec, t_dec, D), gla_p[None], gla_s[None],
            conv_p[None], conv_s[None])
```

```python
import functools

import jax
import jax.numpy as jnp
from jax import lax
from jax.experimental import pallas as pl
from jax.experimental.pallas import tpu as pltpu

bf16 = jnp.bfloat16
f32 = jnp.float32

D = 1024
H = 4
DK = 512
DV = 1024
DKH = DK // H
DVH = DV // H
RANK = 16
RANK_PAD = 128
TAU = 16.0
CONV_BUF = 2
CHUNK = 64
KCHUNK = 128
D_FF = 4 * D
DEPTH = 2
ALPHA = (2 * DEPTH) ** 0.25
LN_EPS = 1e-5
RMS_EPS = 1e-6
Q_SCALE = DKH ** -0.5

SUB = 8
VMEM_LIMIT = 56 * 1024 * 1024

_NT = (((1,), (1,)), ((), ()))
_TN = (((0,), (0,)), ((), ()))


def _dot(a, b):
    return jnp.dot(a, b, preferred_element_type=f32)


def _dg(a, b, dims):
    return lax.dot_general(a, b, dims, preferred_element_type=f32)


def _layer_norm(y, g, b):
    mu = jnp.mean(y, axis=-1, keepdims=True)
    yc = y - mu
    var = jnp.mean(yc * yc, axis=-1, keepdims=True)
    return yc * lax.rsqrt(var + LN_EPS) * g + b


def _log_sigmoid(z):
    return -(jnp.maximum(-z, 0.0) + jnp.log(1.0 + jnp.exp(-jnp.abs(z))))


def _gla_project(xb, wq_ref, wgl_ref, wgu_ref, bgate_ref):
    q = _dot(xb, wq_ref[:, 0:DK]) * Q_SCALE
    k = _dot(xb, wq_ref[:, DK:2 * DK])
    v = _dot(xb, wq_ref[:, 2 * DK:2 * DK + DV])
    r = _dot(xb, wq_ref[:, 2 * DK + DV:2 * DK + 2 * DV])
    gl = _dot(xb, wgl_ref[...])
    z = _dot(gl.astype(bf16), wgu_ref[...]) + bgate_ref[0:1, :]
    g = _log_sigmoid(z) / TAU
    return q, k, v, r, g


def _gla_post(o, r, x, ng, wo_ref, lg, lb):
    parts = []
    for h in range(H):
        vs = slice(h * DVH, (h + 1) * DVH)
        oh = o[:, vs]
        ms = jnp.mean(oh * oh, axis=-1, keepdims=True)
        parts.append(oh * lax.rsqrt(ms + RMS_EPS) * ng[:, vs])
    on = jnp.concatenate(parts, axis=1)
    gated = on * (r * jax.nn.sigmoid(r))
    y = _dot(gated.astype(bf16), wo_ref[...])
    return _layer_norm(ALPHA * x + y, lg, lb)


PROJ_TILE = 512
W_VPU, W_NORM, W_OUT, W_MXU = 1.0, 1.0, 1.0, 1.0


def _interleave(main_tasks, filler_tasks):
    total = sum(w for _, w in main_tasks)
    n_fill = len(filler_tasks)
    done, acc = 0, 0.0
    for task, w in main_tasks:
        task()
        acc += w
        want = int(round(acc * n_fill / total))
        while done < want:
            filler_tasks[done]()
            done += 1


def _gla_stage1_tasks(x_ref, slot, w_refs, scr):
    wq_ref, wgl_ref, wgu_ref, bgate_ref = w_refs
    xb_scr, q_scr, k_scr, v_scr, r_scr, g_scr = scr[:6]

    def cast_x():
        xb_scr[slot] = x_ref[...].astype(bf16)

    def proj(dst, c0, w0, scale):
        def run():
            acc = _dot(xb_scr[slot], wq_ref[:, w0:w0 + PROJ_TILE])
            dst[slot, :, c0:c0 + PROJ_TILE] = acc * scale if scale is not None else acc
        return run

    def gate():
        gl = _dot(xb_scr[slot], wgl_ref[...])
        z = _dot(gl.astype(bf16), wgu_ref[...]) + bgate_ref[0:1, :]
        g_scr[slot] = _log_sigmoid(z) / TAU

    tiles = []
    for dst, w_base, width, scale in ((q_scr, 0, DK, Q_SCALE), (k_scr, DK, DK, None),
                                      (v_scr, 2 * DK, DV, None), (r_scr, 2 * DK + DV, DV, None)):
        for c0 in range(0, width, PROJ_TILE):
            tiles.append(proj(dst, c0, w_base + c0, scale))
    return cast_x, gate, tiles


def _split3(a):
    a1 = a.astype(bf16)
    r1 = a - a1.astype(f32)
    a2 = r1.astype(bf16)
    a3 = (r1 - a2.astype(f32)).astype(bf16)
    return a1, a2, a3


def _gla_stage2_tasks(xres_ref, y_ref, r0, slot, scr, s_scr, ng_ref, wo_ref, lg_ref, lb_ref, n_chunks):
    _, q_scr, k_scr, v_scr, r_scr, g_scr, b_scr, o_scr, gated_scr = scr
    tb = n_chunks * KCHUNK
    mid = KCHUNK // 2

    def cumsum():
        row = lax.broadcasted_iota(jnp.int32, (tb, tb), 0)
        col = lax.broadcasted_iota(jnp.int32, (tb, tb), 1)
        tri = ((row >= col) & ((row // KCHUNK) == (col // KCHUNK))).astype(bf16)
        g1, g2, g3 = _split3(g_scr[slot])
        b_scr[...] = _dot(tri, g1) + _dot(tri, g2) + _dot(tri, g3)

    def chunk_prep(c, cell):
        rs = slice(c * KCHUNK, (c + 1) * KCHUNK)

        def run():
            b = b_scr[rs, :]
            b_mid = b[mid - 1:mid, :]
            b_last = b[KCHUNK - 1:KCHUNK, :]
            qc = q_scr[slot, rs, :]
            kc = k_scr[slot, rs, :]
            cell["qs"] = (qc * jnp.exp(b - b_mid)).astype(bf16)
            cell["ks"] = (kc * jnp.exp(b_mid - b)).astype(bf16)
            cell["qd"] = (qc * jnp.exp(b)).astype(bf16)
            cell["kk"] = (kc * jnp.exp(b_last - b)).astype(bf16)
            cell["vb"] = v_scr[slot, rs, :].astype(bf16)
            cell["dec_t"] = jnp.exp(jnp.broadcast_to(b_last, (KCHUNK, DK)).T)
        return run

    def head_scores(h, cell):
        ks = slice(h * DKH, (h + 1) * DKH)

        def run():
            ri = lax.broadcasted_iota(jnp.int32, (KCHUNK, KCHUNK), 0)
            ci = lax.broadcasted_iota(jnp.int32, (KCHUNK, KCHUNK), 1)
            sc = _dg(cell["qs"][:, ks], cell["ks"][:, ks], _NT)
            cell["sc", h] = jnp.where(ri >= ci, sc, 0.0).astype(bf16)
        return run

    def head_update(c, h, cell):
        rs = slice(c * KCHUNK, (c + 1) * KCHUNK)
        ks = slice(h * DKH, (h + 1) * DKH)
        vs = slice(h * DVH, (h + 1) * DVH)

        def run():
            st = s_scr[h]
            vh = cell["vb"][:, vs]
            lhs = jnp.concatenate([cell["sc", h], cell["qd"][:, ks]], axis=1)
            rhs = jnp.concatenate([vh, st.astype(bf16)], axis=0)
            o_scr[rs, vs] = _dot(lhs, rhs)
            dec_h = cell["dec_t"][ks, :]
            dec_m = jnp.concatenate([dec_h] * (DVH // KCHUNK), axis=1)
            s_scr[h] = dec_m * st + _dg(cell["kk"][:, ks], vh, _TN)
        return run

    def norm_gate(h):
        vs = slice(h * DVH, (h + 1) * DVH)

        def run():
            oh = o_scr[:, vs]
            rh = r_scr[slot, :, vs]
            ms = jnp.mean(oh * oh, axis=-1, keepdims=True)
            on = oh * lax.rsqrt(ms + RMS_EPS) * ng_ref[0:1, vs]
            gated_scr[:, vs] = (on * (rh * jax.nn.sigmoid(rh))).astype(bf16)
        return run

    def out_rows(m0, m1):
        def run():
            y = _dot(gated_scr[m0:m1, :], wo_ref[...])
            x = xres_ref[r0 + m0:r0 + m1, :]
            y_ref[r0 + m0:r0 + m1, :] = _layer_norm(ALPHA * x + y, lg_ref[0:1, :], lb_ref[0:1, :])
        return run

    tasks = [(cumsum, W_VPU)]
    for c in range(n_chunks):
        cell = {}
        tasks.append((chunk_prep(c, cell), W_VPU))
        tasks += [(head_scores(h, cell), W_MXU) for h in range(H)]
        tasks += [(head_update(c, h, cell), W_MXU) for h in range(H)]
    tasks += [(norm_gate(h), W_NORM) for h in range(H)]
    half = tb // 2
    tasks += [(out_rows(0, half), W_OUT), (out_rows(half, tb), W_OUT)]
    return tasks


def _gla_prompt_kernel(x0_ref, xa_ref, xb_ref, xres_ref, wq_ref, wgl_ref, wgu_ref, bgate_ref, ng_ref, wo_ref,
                       lg_ref, lb_ref, *refs, n_chunks, steps_per_seq, n_side):
    side_in = refs[:n_side]
    y_ref, st_ref = refs[n_side:n_side + 2]
    side_out = refs[n_side + 2:2 * n_side + 2]
    (xb_scr, q_scr, k_scr, v_scr, r_scr, g_scr, b_scr, o_scr, gated_scr, s_scr) = refs[2 * n_side + 2:]
    i = pl.program_id(0)
    tb = n_chunks * KCHUNK
    w_refs = (wq_ref, wgl_ref, wgu_ref, bgate_ref)
    scr = (xb_scr, q_scr, k_scr, v_scr, r_scr, g_scr, b_scr, o_scr, gated_scr)
    post = (ng_ref, wo_ref, lg_ref, lb_ref)

    @pl.when(i == 0)
    def _():
        cast_x, gate, tiles = _gla_stage1_tasks(x0_ref, 0, w_refs, scr)
        for task in [cast_x, gate] + tiles:
            task()

    @pl.when(i % steps_per_seq == 0)
    def _():
        s_scr[...] = jnp.zeros_like(s_scr)

    def side_cast(src_ref, dst_ref):
        def run():
            dst_ref[...] = src_ref[...].astype(bf16)
        return run

    side_tasks = [side_cast(s, d) for s, d in zip(side_in, side_out)]
    for r0, slot, x_next in ((0, 0, xa_ref), (tb, 1, xb_ref)):
        cast_x, gate, tiles = _gla_stage1_tasks(x_next, 1 - slot, w_refs, scr)
        stage2 = _gla_stage2_tasks(xres_ref, y_ref, r0, slot, scr, s_scr, *post, n_chunks)
        cast_x()
        fillers = tiles + side_tasks[slot::2]
        _interleave([stage2[0], (gate, W_VPU)] + stage2[1:], fillers)

    @pl.when(i % steps_per_seq == steps_per_seq - 1)
    def _():
        for h in range(H):
            st_ref[0, h] = s_scr[h]


def _gla_sample_kernel(x_ref, st_ref, wq_ref, wgl_ref, wgu_ref, bgate_ref, ng_ref, wo_ref, lg_ref, lb_ref,
                       y_ref, sto_ref, o_scr, *, n_seq, t_valid):
    rows = n_seq * SUB
    x = x_ref[...]
    q, k, v, r, g = _gla_project(x.astype(bf16), wq_ref, wgl_ref, wgu_ref, bgate_ref)

    row = lax.broadcasted_iota(jnp.int32, (rows, rows), 0)
    col = lax.broadcasted_iota(jnp.int32, (rows, rows), 1)
    same_seq = (row // SUB) == (col // SUB)
    tri = (same_seq & (row >= col)).astype(f32)
    tri_last = (same_seq & ((col % SUB) < t_valid)).astype(f32)
    b = jnp.dot(tri, g, precision=lax.Precision.HIGHEST, preferred_element_type=f32)
    b_last = jnp.dot(tri_last, g, precision=lax.Precision.HIGHEST, preferred_element_type=f32)

    valid = (lax.broadcasted_iota(jnp.int32, (rows, DK), 0) % SUB) < t_valid
    qd = (q * jnp.exp(b)).astype(bf16)
    kd = jnp.where(valid, k * jnp.exp(-b), 0.0).astype(bf16)
    kk_t = jnp.where(valid, k * jnp.exp(b_last - b), 0.0).T.astype(bf16)
    vb = v.astype(bf16)
    dec_t = jnp.exp(b_last.T)

    intra_mask = same_seq & (row >= col)
    for h in range(H):
        ks = slice(h * DKH, (h + 1) * DKH)
        vs = slice(h * DVH, (h + 1) * DVH)
        sc = jnp.where(intra_mask, _dg(qd[:, ks], kd[:, ks], _NT), 0.0).astype(bf16)
        o_scr[:, vs] = _dot(sc, vb[:, vs])

    seq_of_col = lax.broadcasted_iota(jnp.int32, (DKH, rows), 1) // SUB

    def seq_dots(n):
        rs = slice(n * SUB, (n + 1) * SUB)
        res = []
        for h in range(H):
            ks = slice(h * DKH, (h + 1) * DKH)
            vs = slice(h * DVH, (h + 1) * DVH)
            s0 = st_ref[n, h]
            o_inter = _dot(qd[rs, ks], s0.astype(bf16))
            upd = _dot(jnp.where(seq_of_col == n, kk_t[ks, :], jnp.zeros((), bf16)), vb[:, vs])
            res.append((s0, o_inter, upd))
        return res

    def seq_combine(n, res):
        rs = slice(n * SUB, (n + 1) * SUB)
        for h, (s0, o_inter, upd) in enumerate(res):
            ks = slice(h * DKH, (h + 1) * DKH)
            vs = slice(h * DVH, (h + 1) * DVH)
            o_scr[rs, vs] = o_scr[rs, vs] + o_inter
            sto_ref[n, h] = dec_t[ks, n * SUB:n * SUB + 1] * s0 + upd

    pending = None
    for n in range(n_seq):
        res = seq_dots(n)
        if pending is not None:
            seq_combine(*pending)
        pending = (n, res)
    seq_combine(*pending)

    y_ref[...] = _gla_post(o_scr[...], r, x, ng_ref[0:1, :], wo_ref, lg_ref[0:1, :], lb_ref[0:1, :])


def _conv_prompt_kernel(x0_ref, xa_ref, xb_ref, xres_ref, win_ref, wc_ref, wout_ref, lg_ref, lb_ref,
                        y_ref, st_ref, xb_scr, bg_scr, cg_scr, u_scr, gated_scr, c_scr,
                        *, tb, steps_per_seq):
    i = pl.program_id(0)
    half = tb // 2
    col_tile = D // 2

    def stage1_tasks(x_ref, slot):
        def cast():
            xb_scr[slot] = x_ref[...].astype(bf16)

        def proj_bg(c0):
            def run():
                bg_scr[slot, :, c0:c0 + col_tile] = _dot(xb_scr[slot], win_ref[:, c0:c0 + col_tile])
            return run

        def proj_cg(c0):
            def run():
                cg_scr[:, c0:c0 + col_tile] = _dot(xb_scr[slot], win_ref[:, D + c0:D + c0 + col_tile])
            return run

        def proj_u(c0):
            def run():
                hh = _dot(xb_scr[slot], win_ref[:, 2 * D + c0:2 * D + c0 + col_tile])
                u_scr[slot, :, c0:c0 + col_tile] = cg_scr[:, c0:c0 + col_tile] * hh
            return run

        cols = range(0, D, col_tile)
        return cast, ([proj_bg(c) for c in cols] + [proj_cg(c) for c in cols] + [proj_u(c) for c in cols])

    def stage2_tasks(r0, slot):
        def conv(m0):
            def run():
                u = u_scr[slot, m0:m0 + half, :]
                t = lax.broadcasted_iota(jnp.int32, (half, D), 0)
                c0 = c_scr[0:1, :]
                c1 = c_scr[1:2, :]
                p1 = jnp.where(t == 0, c1, pltpu.roll(u, 1, 0))
                p2 = jnp.where(t == 0, c0, jnp.where(t == 1, c1, pltpu.roll(u, 2, 0)))
                cv = p2 * wc_ref[0:1, :] + p1 * wc_ref[1:2, :] + u * wc_ref[2:3, :]
                gated_scr[m0:m0 + half, :] = (bg_scr[slot, m0:m0 + half, :] * cv).astype(bf16)
                c_scr[0:2, :] = u[half - 2:half, :]
            return run

        def out(m0):
            def run():
                y = _dot(gated_scr[m0:m0 + half, :], wout_ref[...])
                x = xres_ref[r0 + m0:r0 + m0 + half, :]
                y_ref[r0 + m0:r0 + m0 + half, :] = _layer_norm(ALPHA * x + y, lg_ref[0:1, :], lb_ref[0:1, :])
            return run

        return [(conv(0), 1.0), (conv(half), 1.0), (out(0), 1.0), (out(half), 1.0)]

    @pl.when(i == 0)
    def _():
        cast, tiles = stage1_tasks(x0_ref, 0)
        cast()
        for task in tiles:
            task()

    @pl.when(i % steps_per_seq == 0)
    def _():
        c_scr[...] = jnp.zeros_like(c_scr)

    for r0, slot, x_next in ((0, 0, xa_ref), (tb, 1, xb_ref)):
        cast, tiles = stage1_tasks(x_next, 1 - slot)
        cast()
        _interleave(stage2_tasks(r0, slot), tiles)

    st_ref[0] = c_scr[0:CONV_BUF, :]


def _conv_sample_kernel(x_ref, e1_ref, e2_ref, win_ref, wc_ref, wout_ref, lg_ref, lb_ref, y_ref, u_ref,
                        xb_scr, p_scr, *, t_seq):
    c = pl.program_id(0)
    n_proj = p_scr.shape[0]

    @pl.when(c == 0)
    def _():
        xb_scr[...] = x_ref[...].astype(bf16)

    @pl.when(c < n_proj)
    def _():
        p_scr[c] = _dot(xb_scr[...], win_ref[...])

    @pl.when(c == n_proj)
    def _():
        x = x_ref[...]
        u = p_scr[1] * p_scr[2]
        t = lax.broadcasted_iota(jnp.int32, u.shape, 0) % t_seq
        p1 = jnp.where(t >= 1, pltpu.roll(u, 1, 0), e1_ref[...])
        p2 = jnp.where(t >= 2, pltpu.roll(u, 2, 0), e2_ref[...])
        conv = p2 * wc_ref[0:1, :] + p1 * wc_ref[1:2, :] + u * wc_ref[2:3, :]
        y = _dot((p_scr[0] * conv).astype(bf16), wout_ref[...])
        y_ref[...] = _layer_norm(ALPHA * x + y, lg_ref[0:1, :], lb_ref[0:1, :])
        u_ref[...] = u


def _mlp_kernel(x_ref, wup_ref, wdn_ref, lg_ref, lb_ref, y_ref, xb_scr, h_scr, *, n_sub, sub_rows, ff_chunk):
    def sub_tasks(s):
        rs = slice(s * sub_rows, (s + 1) * sub_rows)
        buf = s % 2

        def cast():
            xb_scr[buf] = x_ref[rs, :].astype(bf16)

        def up(c):
            cs = slice(c * ff_chunk, (c + 1) * ff_chunk)

            def run():
                hcol = _dot(xb_scr[buf], wup_ref[0, :, cs])
                h_scr[buf, :, cs] = jnp.square(jnp.maximum(hcol, 0.0)).astype(bf16)
            return run

        def down():
            y = _dot(h_scr[buf], wdn_ref[0])
            y_ref[rs, :] = _layer_norm(ALPHA * x_ref[rs, :] + y, lg_ref[0, 0:1, :], lb_ref[0, 0:1, :])

        return [cast] + [up(c) for c in range(D_FF // ff_chunk)], down

    pending = None
    for s in range(n_sub):
        ups, down = sub_tasks(s)
        ups[0]()
        ups[1]()
        if pending is not None:
            pending()
        for task in ups[2:]:
            task()
        pending = down
    pending()


def _const_spec(shape):
    nd = len(shape)
    return pl.BlockSpec(shape, lambda *_: (0,) * nd, pipeline_mode=pl.Buffered(1))


def _layer_spec(shape, layer):
    nd = len(shape)
    return pl.BlockSpec((1,) + tuple(shape[1:]), lambda *_: (layer,) + (0,) * (nd - 1),
                        pipeline_mode=pl.Buffered(1))


def _mlp_stream_kernel(x_ref, wup_ref, wdn_ref, lg_ref, lb_ref, y_ref, xb_scr, acc_scr):
    c = pl.program_id(0)

    @pl.when(c == 0)
    def _():
        xb_scr[...] = x_ref[...].astype(bf16)

    hcol = _dot(xb_scr[...], wup_ref[0])
    part = _dot(jnp.square(jnp.maximum(hcol, 0.0)).astype(bf16), wdn_ref[0])

    @pl.when(c == 0)
    def _():
        acc_scr[...] = part

    @pl.when(c > 0)
    def _():
        acc_scr[...] += part

    @pl.when(c == pl.num_programs(0) - 1)
    def _():
        y_ref[...] = _layer_norm(ALPHA * x_ref[...] + acc_scr[...], lg_ref[0, 0:1, :], lb_ref[0, 0:1, :])


def _mlp_stream(x2d, wup, wdn, lg, lb, layer, ff_chunk):
    m = x2d.shape[0]
    return pl.pallas_call(
        _mlp_stream_kernel,
        grid=(D_FF // ff_chunk,),
        in_specs=[pl.BlockSpec((m, D), lambda c: (0, 0)),
                  pl.BlockSpec((1, D, ff_chunk), lambda c: (layer, 0, c)),
                  pl.BlockSpec((1, ff_chunk, D), lambda c: (layer, c, 0)),
                  _layer_spec(lg.shape, layer), _layer_spec(lb.shape, layer)],
        out_specs=pl.BlockSpec((m, D), lambda c: (0, 0)),
        out_shape=jax.ShapeDtypeStruct((m, D), f32),
        scratch_shapes=[pltpu.VMEM((m, D), bf16), pltpu.VMEM((m, D), f32)],
        compiler_params=pltpu.CompilerParams(dimension_semantics=("arbitrary",), vmem_limit_bytes=VMEM_LIMIT),
        name="mlp_stream",
    )(x2d, wup, wdn, lg, lb)


def _cast_kernel(src_ref, dst_ref):
    dst_ref[...] = src_ref[...].astype(dst_ref.dtype)


def _cast_bf16(a2d, row_block):
    rows, cols = a2d.shape
    return pl.pallas_call(
        _cast_kernel,
        grid=(rows // row_block,),
        in_specs=[pl.BlockSpec((row_block, cols), lambda i: (i, 0))],
        out_specs=pl.BlockSpec((row_block, cols), lambda i: (i, 0)),
        out_shape=jax.ShapeDtypeStruct((rows, cols), bf16),
        compiler_params=pltpu.CompilerParams(dimension_semantics=("arbitrary",)),
        name="cast_bf16",
    )(a2d)


def _mlp(x2d, wup, wdn, lg, lb, layer, n_sub, sub_rows):
    m = x2d.shape[0]
    tm = n_sub * sub_rows
    return pl.pallas_call(
        functools.partial(_mlp_kernel, n_sub=n_sub, sub_rows=sub_rows, ff_chunk=1024),
        grid=(m // tm,),
        in_specs=[pl.BlockSpec((tm, D), lambda i: (i, 0)),
                  _layer_spec(wup.shape, layer), _layer_spec(wdn.shape, layer),
                  _layer_spec(lg.shape, layer), _layer_spec(lb.shape, layer)],
        out_specs=pl.BlockSpec((tm, D), lambda i: (i, 0)),
        out_shape=jax.ShapeDtypeStruct((m, D), f32),
        scratch_shapes=[pltpu.VMEM((2, sub_rows, D), bf16), pltpu.VMEM((2, sub_rows, D_FF), bf16)],
        compiler_params=pltpu.CompilerParams(dimension_semantics=("arbitrary",), vmem_limit_bytes=VMEM_LIMIT),
        name="mlp",
    )(x2d, wup, wdn, lg, lb)


def _gla_prompt(x, wq, wgl, wgu, bgate, ng, wo, lg, lb, tb, side):
    bsz, t, _ = x.shape
    n_chunks = tb // KCHUNK
    n_blocks = bsz * t // tb
    n_steps = n_blocks // 2
    steps_per_seq = t // (2 * tb)
    x2d = x.reshape(bsz * t, D)
    side_specs = [pl.BlockSpec((a.shape[0] // n_steps, a.shape[1]), lambda i: (i, 0)) for a in side]
    outs = pl.pallas_call(
        functools.partial(_gla_prompt_kernel, n_chunks=n_chunks, steps_per_seq=steps_per_seq, n_side=len(side)),
        grid=(n_steps,),
        in_specs=[pl.BlockSpec((tb, D), lambda i: (0, 0), pipeline_mode=pl.Buffered(1)),
                  pl.BlockSpec((tb, D), lambda i: (2 * i + 1, 0)),
                  pl.BlockSpec((tb, D), lambda i: (jnp.minimum(2 * i + 2, n_blocks - 1), 0)),
                  pl.BlockSpec((2 * tb, D), lambda i: (i, 0)),
                  _const_spec(wq.shape), _const_spec(wgl.shape), _const_spec(wgu.shape),
                  _const_spec(bgate.shape), _const_spec(ng.shape), _const_spec(wo.shape),
                  _const_spec(lg.shape), _const_spec(lb.shape)] + side_specs,
        out_specs=[pl.BlockSpec((2 * tb, D), lambda i: (i, 0)),
                   pl.BlockSpec((1, H, DKH, DVH), lambda i: (i // steps_per_seq, 0, 0, 0))] + side_specs,
        out_shape=[jax.ShapeDtypeStruct((bsz * t, D), f32),
                   jax.ShapeDtypeStruct((bsz, H, DKH, DVH), f32)]
                  + [jax.ShapeDtypeStruct(a.shape, bf16) for a in side],
        scratch_shapes=[pltpu.VMEM((2, tb, D), bf16),
                        pltpu.VMEM((2, tb, DK), f32), pltpu.VMEM((2, tb, DK), f32),
                        pltpu.VMEM((2, tb, DV), f32), pltpu.VMEM((2, tb, DV), f32),
                        pltpu.VMEM((2, tb, DK), f32),
                        pltpu.VMEM((tb, DK), f32), pltpu.VMEM((tb, DV), f32), pltpu.VMEM((tb, DV), bf16),
                        pltpu.VMEM((H, DKH, DVH), f32)],
        compiler_params=pltpu.CompilerParams(dimension_semantics=("arbitrary",),
                                             vmem_limit_bytes=VMEM_LIMIT),
        name="gla_prompt",
    )(x2d, x2d, x2d, x2d, wq, wgl, wgu, bgate, ng, wo, lg, lb, *side)
    return outs[0].reshape(bsz, t, D), outs[1], outs[2:]


def _gla_sample(x_pad, state, wq, wgl, wgu, bgate, ng, wo, lg, lb, n_seq, t_valid):
    s_total = state.shape[0]
    rows = n_seq * SUB
    return pl.pallas_call(
        functools.partial(_gla_sample_kernel, n_seq=n_seq, t_valid=t_valid),
        grid=(s_total // n_seq,),
        in_specs=[pl.BlockSpec((rows, D), lambda i: (i, 0)),
                  pl.BlockSpec((n_seq, H, DKH, DVH), lambda i: (i, 0, 0, 0)),
                  _const_spec(wq.shape), _const_spec(wgl.shape), _const_spec(wgu.shape),
                  _const_spec(bgate.shape), _const_spec(ng.shape), _const_spec(wo.shape),
                  _const_spec(lg.shape), _const_spec(lb.shape)],
        out_specs=[pl.BlockSpec((rows, D), lambda i: (i, 0)),
                   pl.BlockSpec((n_seq, H, DKH, DVH), lambda i: (i, 0, 0, 0))],
        out_shape=[jax.ShapeDtypeStruct((s_total * SUB, D), f32),
                   jax.ShapeDtypeStruct(state.shape, f32)],
        scratch_shapes=[pltpu.VMEM((rows, DV), f32)],
        compiler_params=pltpu.CompilerParams(dimension_semantics=("arbitrary",), vmem_limit_bytes=VMEM_LIMIT),
        name="gla_sample",
    )(x_pad, state, wq, wgl, wgu, bgate, ng, wo, lg, lb)


def _conv_prompt(x, win, wc, wout, lg, lb, tb):
    bsz, t, _ = x.shape
    n_blocks = bsz * t // tb
    steps_per_seq = t // (2 * tb)
    x2d = x.reshape(bsz * t, D)
    y, st = pl.pallas_call(
        functools.partial(_conv_prompt_kernel, tb=tb, steps_per_seq=steps_per_seq),
        grid=(n_blocks // 2,),
        in_specs=[pl.BlockSpec((tb, D), lambda i: (0, 0), pipeline_mode=pl.Buffered(1)),
                  pl.BlockSpec((tb, D), lambda i: (2 * i + 1, 0)),
                  pl.BlockSpec((tb, D), lambda i: (jnp.minimum(2 * i + 2, n_blocks - 1), 0)),
                  pl.BlockSpec((2 * tb, D), lambda i: (i, 0)),
                  _const_spec(win.shape), _const_spec(wc.shape), _const_spec(wout.shape),
                  _const_spec(lg.shape), _const_spec(lb.shape)],
        out_specs=[pl.BlockSpec((2 * tb, D), lambda i: (i, 0)),
                   pl.BlockSpec((1, CONV_BUF, D), lambda i: (i // steps_per_seq, 0, 0))],
        out_shape=[jax.ShapeDtypeStruct((bsz * t, D), f32),
                   jax.ShapeDtypeStruct((bsz, CONV_BUF, D), f32)],
        scratch_shapes=[pltpu.VMEM((2, tb, D), bf16), pltpu.VMEM((2, tb, D), f32),
                        pltpu.VMEM((tb, D), f32), pltpu.VMEM((2, tb, D), f32),
                        pltpu.VMEM((tb, D), bf16), pltpu.VMEM((SUB, D), f32)],
        compiler_params=pltpu.CompilerParams(dimension_semantics=("arbitrary",), vmem_limit_bytes=VMEM_LIMIT),
        name="conv_prompt",
    )(x2d, x2d, x2d, x2d, win, wc, wout, lg, lb)
    return y.reshape(bsz, t, D), st


def _conv_sample(x2d, e1, e2, win, wc, wout, lg, lb, t_seq):
    m = x2d.shape[0]
    n_proj = win.shape[1] // D
    full = lambda shape: pl.BlockSpec(shape, lambda i: (0,) * len(shape))
    return pl.pallas_call(
        functools.partial(_conv_sample_kernel, t_seq=t_seq),
        grid=(n_proj + 1,),
        in_specs=[full((m, D)), full((m, D)), full((m, D)),
                  pl.BlockSpec((D, D), lambda c: (0, jnp.minimum(c, n_proj - 1))),
                  _const_spec(wc.shape), _const_spec(wout.shape),
                  _const_spec(lg.shape), _const_spec(lb.shape)],
        out_specs=[full((m, D)), full((m, D))],
        out_shape=[jax.ShapeDtypeStruct((m, D), f32), jax.ShapeDtypeStruct((m, D), f32)],
        scratch_shapes=[pltpu.VMEM((m, D), bf16), pltpu.VMEM((n_proj, m, D), f32)],
        compiler_params=pltpu.CompilerParams(dimension_semantics=("arbitrary",), vmem_limit_bytes=VMEM_LIMIT),
        name="conv_sample",
    )(x2d, e1, e2, win, wc, wout, lg, lb)


def kernel(x_prompt, x_sample, state_gla, state_conv, gla_w_in, gla_w_gate_up, gla_b_gate, gla_norm_g, gla_w_o, conv_w_in, conv_w_conv, conv_w_out, mlp_w_up, mlp_w_down, ln1_g, ln1_b, ln2_g, ln2_b):
    bsz, t, _ = x_prompt.shape
    n_dec, t_dec, _ = x_sample.shape
    assert t % CHUNK == 0 and t_dec <= SUB and t_dec >= 2

    w_in = gla_w_in[0]
    wq = _cast_bf16(w_in, row_block=128)
    wgl = jnp.pad(w_in[:, 2 * DK + 2 * DV:], ((0, 0), (0, RANK_PAD - RANK))).astype(bf16)
    wgu = jnp.pad(gla_w_gate_up[0], ((0, RANK_PAD - RANK), (0, 0))).astype(bf16)
    bgate = gla_b_gate[0].reshape(1, DK)
    ng = gla_norm_g[0].reshape(1, DV)
    wo = gla_w_o[0].astype(bf16)
    cwc = conv_w_conv[0]
    ln2g = ln2_g.reshape(DEPTH, 1, D)
    ln2b = ln2_b.reshape(DEPTH, 1, D)
    row = lambda a, i: a[i].reshape(1, D)
    mlp_prompt = functools.partial(_mlp, n_sub=4, sub_rows=256)
    mlp_sample = functools.partial(_mlp_stream, ff_chunk=1024)

    side = (mlp_w_up.reshape(DEPTH * D, D_FF), mlp_w_down.reshape(DEPTH * D_FF, D), conv_w_in[0], conv_w_out[0])
    xp, gla_p, (wup, wdn, cwin, cwout) = _gla_prompt(x_prompt, wq, wgl, wgu, bgate, ng, wo, row(ln1_g, 0),
                                                     row(ln1_b, 0), tb=256, side=side)
    wup = wup.reshape(DEPTH, D, D_FF)
    wdn = wdn.reshape(DEPTH, D_FF, D)
    xs_pad = jnp.pad(x_sample, ((0, 0), (0, SUB - t_dec), (0, 0))).reshape(n_dec * SUB, D)
    xs_pad, gla_s = _gla_sample(xs_pad, state_gla[0], wq, wgl, wgu, bgate, ng, wo, row(ln1_g, 0), row(ln1_b, 0),
                                n_seq=16, t_valid=t_dec)
    xs = xs_pad.reshape(n_dec, SUB, D)[:, :t_dec].reshape(n_dec * t_dec, D)
    xp = mlp_prompt(xp.reshape(bsz * t, D), wup, wdn, ln2g, ln2b, layer=0)
    xs = mlp_sample(xs, wup, wdn, ln2g, ln2b, layer=0)

    xp, conv_p = _conv_prompt(xp.reshape(bsz, t, D), cwin, cwc, cwout, row(ln1_g, 1), row(ln1_b, 1), tb=512)
    buf = state_conv[0]
    e1 = jnp.pad(buf[:, 1:2], ((0, 0), (0, t_dec - 1), (0, 0))).reshape(n_dec * t_dec, D)
    e2 = jnp.pad(buf, ((0, 0), (0, t_dec - 2), (0, 0))).reshape(n_dec * t_dec, D)
    xs, u_s = _conv_sample(xs, e1, e2, cwin, cwc, cwout, row(ln1_g, 1), row(ln1_b, 1), t_seq=t_dec)
    conv_s = u_s.reshape(n_dec, t_dec, D)[:, t_dec - 2:]
    xp = mlp_prompt(xp.reshape(bsz * t, D), wup, wdn, ln2g, ln2b, layer=1)
    xs = mlp_sample(xs, wup, wdn, ln2g, ln2b, layer=1)

    return (xp.reshape(bsz, t, D), xs.reshape(n_dec, t_dec, D), gla_p[None], gla_s[None],
            conv_p[None], conv_s[None])
```

```python
import functools

import jax
import jax.numpy as jnp
from jax import lax
from jax.experimental import pallas as pl
from jax.experimental.pallas import tpu as pltpu

bf16 = jnp.bfloat16
f32 = jnp.float32

D = 1024
H = 4
DK = 512
DV = 1024
DKH = DK // H
DVH = DV // H
RANK = 16
RANK_PAD = 128
TAU = 16.0
CHUNK = 64
KCHUNK = 128
D_FF = 4 * D
DEPTH = 2
ALPHA = (2 * DEPTH) ** 0.25
LN_EPS = 1e-5
RMS_EPS = 1e-6
Q_SCALE = DKH ** -0.5

SUB = 8
VMEM_LIMIT = 56 * 1024 * 1024

_NT = (((1,), (1,)), ((), ()))
_TN = (((0,), (0,)), ((), ()))


def _dot(a, b):
    return jnp.dot(a, b, preferred_element_type=f32)


def _dg(a, b, dims):
    return lax.dot_general(a, b, dims, preferred_element_type=f32)


def _layer_norm(y, g, b):
    mu = jnp.mean(y, axis=-1, keepdims=True)
    yc = y - mu
    var = jnp.mean(yc * yc, axis=-1, keepdims=True)
    return yc * lax.rsqrt(var + LN_EPS) * g + b


def _log_sigmoid(z):
    return -(jnp.maximum(-z, 0.0) + jnp.log(1.0 + jnp.exp(-jnp.abs(z))))


def _gla_project(xb, wq_ref, wgl_ref, wgu_ref, bgate_ref):
    q = _dot(xb, wq_ref[:, 0:DK]) * Q_SCALE
    k = _dot(xb, wq_ref[:, DK:2 * DK])
    v = _dot(xb, wq_ref[:, 2 * DK:2 * DK + DV])
    r = _dot(xb, wq_ref[:, 2 * DK + DV:2 * DK + 2 * DV])
    gl = _dot(xb, wgl_ref[...])
    z = _dot(gl.astype(bf16), wgu_ref[...]) + bgate_ref[...]
    g = _log_sigmoid(z) / TAU
    return q, k, v, r, g


def _gla_post(o, r, x, ng, wo_ref, lg, lb):
    parts = []
    for h in range(H):
        vs = slice(h * DVH, (h + 1) * DVH)
        oh = o[:, vs]
        ms = jnp.mean(oh * oh, axis=-1, keepdims=True)
        parts.append(oh * lax.rsqrt(ms + RMS_EPS) * ng[:, vs])
    on = jnp.concatenate(parts, axis=1)
    gated = on * (r * jax.nn.sigmoid(r))
    y = _dot(gated.astype(bf16), wo_ref[...])
    return _layer_norm(ALPHA * x + y, lg, lb)


PROJ_TILE = 512


def _interleave(main_tasks, filler_tasks):
    n_main, n_fill = len(main_tasks), len(filler_tasks)
    done = 0
    for idx, task in enumerate(main_tasks):
        task()
        want = ((idx + 1) * n_fill) // n_main
        while done < want:
            filler_tasks[done]()
            done += 1


def _gla_stage1_tasks(x_ref, slot, w_refs, scr):
    wq_ref, wgl_ref, wgu_ref, bgate_ref = w_refs
    xb_scr, q_scr, k_scr, v_scr, r_scr, g_scr = scr[:6]

    def cast_x():
        xb_scr[slot] = x_ref[...].astype(bf16)

    def proj(dst, c0, w0, scale):
        def run():
            acc = _dot(xb_scr[slot], wq_ref[:, w0:w0 + PROJ_TILE])
            dst[slot, :, c0:c0 + PROJ_TILE] = acc * scale if scale is not None else acc
        return run

    def gate():
        gl = _dot(xb_scr[slot], wgl_ref[...])
        z = _dot(gl.astype(bf16), wgu_ref[...]) + bgate_ref[...]
        g_scr[slot] = _log_sigmoid(z) / TAU

    tiles = []
    for dst, w_base, width, scale in ((q_scr, 0, DK, Q_SCALE), (k_scr, DK, DK, None),
                                      (v_scr, 2 * DK, DV, None), (r_scr, 2 * DK + DV, DV, None)):
        for c0 in range(0, width, PROJ_TILE):
            tiles.append(proj(dst, c0, w_base + c0, scale))
    return cast_x, gate, tiles


def _split3(a):
    a1 = a.astype(bf16)
    r1 = a - a1.astype(f32)
    a2 = r1.astype(bf16)
    a3 = (r1 - a2.astype(f32)).astype(bf16)
    return a1, a2, a3


def _gla_stage2_tasks(xres_ref, y_ref, r0, slot, scr, s_scr, ng_ref, wo_ref, lg_ref, lb_ref, n_chunks):
    _, q_scr, k_scr, v_scr, r_scr, g_scr, b_scr, o_scr, gated_scr = scr
    tb = n_chunks * KCHUNK
    mid = KCHUNK // 2

    def cumsum():
        row = lax.broadcasted_iota(jnp.int32, (tb, tb), 0)
        col = lax.broadcasted_iota(jnp.int32, (tb, tb), 1)
        tri = ((row >= col) & ((row // KCHUNK) == (col // KCHUNK))).astype(bf16)
        g1, g2, g3 = _split3(g_scr[slot])
        b_scr[...] = _dot(tri, g1) + _dot(tri, g2) + _dot(tri, g3)

    def chunk_prep(c, cell):
        rs = slice(c * KCHUNK, (c + 1) * KCHUNK)

        def run():
            b = b_scr[rs, :]
            b_mid = b[mid - 1:mid, :]
            b_last = b[KCHUNK - 1:KCHUNK, :]
            qc = q_scr[slot, rs, :]
            kc = k_scr[slot, rs, :]
            cell["qs"] = (qc * jnp.exp(b - b_mid)).astype(bf16)
            cell["ks"] = (kc * jnp.exp(b_mid - b)).astype(bf16)
            cell["qd"] = (qc * jnp.exp(b)).astype(bf16)
            cell["kk"] = (kc * jnp.exp(b_last - b)).astype(bf16)
            cell["vb"] = v_scr[slot, rs, :].astype(bf16)
            cell["dec_t"] = jnp.exp(jnp.broadcast_to(b_last, (KCHUNK, DK)).T)
        return run

    def head_scores(h, cell):
        ks = slice(h * DKH, (h + 1) * DKH)

        def run():
            ri = lax.broadcasted_iota(jnp.int32, (KCHUNK, KCHUNK), 0)
            ci = lax.broadcasted_iota(jnp.int32, (KCHUNK, KCHUNK), 1)
            sc = _dg(cell["qs"][:, ks], cell["ks"][:, ks], _NT)
            cell["sc", h] = jnp.where(ri >= ci, sc, 0.0).astype(bf16)
        return run

    def head_update(c, h, cell):
        rs = slice(c * KCHUNK, (c + 1) * KCHUNK)
        ks = slice(h * DKH, (h + 1) * DKH)
        vs = slice(h * DVH, (h + 1) * DVH)

        def run():
            st = s_scr[h]
            vh = cell["vb"][:, vs]
            lhs = jnp.concatenate([cell["sc", h], cell["qd"][:, ks]], axis=1)
            rhs = jnp.concatenate([vh, st.astype(bf16)], axis=0)
            o_scr[rs, vs] = _dot(lhs, rhs)
            dec_h = cell["dec_t"][ks, :]
            dec_m = jnp.concatenate([dec_h] * (DVH // KCHUNK), axis=1)
            s_scr[h] = dec_m * st + _dg(cell["kk"][:, ks], vh, _TN)
        return run

    def norm_gate(h):
        vs = slice(h * DVH, (h + 1) * DVH)

        def run():
            oh = o_scr[:, vs]
            rh = r_scr[slot, :, vs]
            ms = jnp.mean(oh * oh, axis=-1, keepdims=True)
            on = oh * lax.rsqrt(ms + RMS_EPS) * ng_ref[:, vs]
            gated_scr[:, vs] = (on * (rh * jax.nn.sigmoid(rh))).astype(bf16)
        return run

    def out_rows(m0, m1):
        def run():
            y = _dot(gated_scr[m0:m1, :], wo_ref[...])
            x = xres_ref[r0 + m0:r0 + m1, :]
            y_ref[r0 + m0:r0 + m1, :] = _layer_norm(ALPHA * x + y, lg_ref[...], lb_ref[...])
        return run

    tasks = [cumsum]
    for c in range(n_chunks):
        cell = {}
        tasks.append(chunk_prep(c, cell))
        tasks += [head_scores(h, cell) for h in range(H)]
        tasks += [head_update(c, h, cell) for h in range(H)]
    tasks += [norm_gate(h) for h in range(H)]
    half = tb // 2
    tasks += [out_rows(0, half), out_rows(half, tb)]
    return tasks


def _gla_prompt_kernel(x0_ref, xa_ref, xb_ref, xres_ref, wq_ref, wgl_ref, wgu_ref, bgate_ref, ng_ref, wo_ref,
                       lg_ref, lb_ref, *refs, n_chunks, steps_per_seq, n_side):
    side_in = refs[:n_side]
    y_ref, st_ref = refs[n_side:n_side + 2]
    side_out = refs[n_side + 2:2 * n_side + 2]
    (xb_scr, q_scr, k_scr, v_scr, r_scr, g_scr, b_scr, o_scr, gated_scr, s_scr) = refs[2 * n_side + 2:]
    i = pl.program_id(0)
    tb = n_chunks * KCHUNK
    w_refs = (wq_ref, wgl_ref, wgu_ref, bgate_ref)
    scr = (xb_scr, q_scr, k_scr, v_scr, r_scr, g_scr, b_scr, o_scr, gated_scr)
    post = (ng_ref, wo_ref, lg_ref, lb_ref)

    @pl.when(i == 0)
    def _():
        cast_x, gate, tiles = _gla_stage1_tasks(x0_ref, 0, w_refs, scr)
        for task in [cast_x, gate] + tiles:
            task()

    @pl.when(i % steps_per_seq == 0)
    def _():
        s_scr[...] = jnp.zeros_like(s_scr)

    def side_cast(src_ref, dst_ref):
        def run():
            dst_ref[...] = src_ref[...].astype(bf16)
        return run

    side_tasks = [side_cast(s, d) for s, d in zip(side_in, side_out)]
    for r0, slot, x_next in ((0, 0, xa_ref), (tb, 1, xb_ref)):
        cast_x, gate, tiles = _gla_stage1_tasks(x_next, 1 - slot, w_refs, scr)
        stage2 = _gla_stage2_tasks(xres_ref, y_ref, r0, slot, scr, s_scr, *post, n_chunks)
        cast_x()
        fillers = tiles + side_tasks[slot::2]
        _interleave([stage2[0], gate] + stage2[1:], fillers)

    @pl.when(i % steps_per_seq == steps_per_seq - 1)
    def _():
        for h in range(H):
            st_ref[0, h] = s_scr[h]


def _gla_sample_kernel(x_ref, st_ref, wq_ref, wgl_ref, wgu_ref, bgate_ref, ng_ref, wo_ref, lg_ref, lb_ref,
                       y_ref, sto_ref, o_scr, *, n_seq, t_valid):
    rows = n_seq * SUB
    x = x_ref[...]
    q, k, v, r, g = _gla_project(x.astype(bf16), wq_ref, wgl_ref, wgu_ref, bgate_ref)

    row = lax.broadcasted_iota(jnp.int32, (rows, rows), 0)
    col = lax.broadcasted_iota(jnp.int32, (rows, rows), 1)
    same_seq = (row // SUB) == (col // SUB)
    tri = (same_seq & (row >= col)).astype(f32)
    tri_last = (same_seq & ((col % SUB) < t_valid)).astype(f32)
    b = jnp.dot(tri, g, precision=lax.Precision.HIGHEST, preferred_element_type=f32)
    b_last = jnp.dot(tri_last, g, precision=lax.Precision.HIGHEST, preferred_element_type=f32)

    valid = (lax.broadcasted_iota(jnp.int32, (rows, DK), 0) % SUB) < t_valid
    qd = (q * jnp.exp(b)).astype(bf16)
    kd = jnp.where(valid, k * jnp.exp(-b), 0.0).astype(bf16)
    kk_t = jnp.where(valid, k * jnp.exp(b_last - b), 0.0).T.astype(bf16)
    vb = v.astype(bf16)
    dec_t = jnp.exp(b_last.T)

    intra_mask = same_seq & (row >= col)
    for h in range(H):
        ks = slice(h * DKH, (h + 1) * DKH)
        vs = slice(h * DVH, (h + 1) * DVH)
        sc = jnp.where(intra_mask, _dg(qd[:, ks], kd[:, ks], _NT), 0.0).astype(bf16)
        o_scr[:, vs] = _dot(sc, vb[:, vs])

    seq_of_col = lax.broadcasted_iota(jnp.int32, (DKH, rows), 1) // SUB

    def seq_dots(n):
        rs = slice(n * SUB, (n + 1) * SUB)
        res = []
        for h in range(H):
            ks = slice(h * DKH, (h + 1) * DKH)
            vs = slice(h * DVH, (h + 1) * DVH)
            s0 = st_ref[n, h]
            o_inter = _dot(qd[rs, ks], s0.astype(bf16))
            upd = _dot(jnp.where(seq_of_col == n, kk_t[ks, :], jnp.zeros((), bf16)), vb[:, vs])
            res.append((s0, o_inter, upd))
        return res

    def seq_combine(n, res):
        rs = slice(n * SUB, (n + 1) * SUB)
        for h, (s0, o_inter, upd) in enumerate(res):
            ks = slice(h * DKH, (h + 1) * DKH)
            vs = slice(h * DVH, (h + 1) * DVH)
            o_scr[rs, vs] = o_scr[rs, vs] + o_inter
            sto_ref[n, h] = dec_t[ks, n * SUB:n * SUB + 1] * s0 + upd

    pending = None
    for n in range(n_seq):
        res = seq_dots(n)
        if pending is not None:
            seq_combine(*pending)
        pending = (n, res)
    seq_combine(*pending)

    y_ref[...] = _gla_post(o_scr[...], r, x, ng_ref[...], wo_ref, lg_ref[...], lb_ref[...])


def _conv_prompt_kernel(x0_ref, xa_ref, xb_ref, xres_ref, win_ref, wc_ref, wout_ref, lg_ref, lb_ref,
                        y_ref, st_ref, xb_scr, bg_scr, cg_scr, u_scr, gated_scr, c_scr,
                        *, tb, steps_per_seq):
    i = pl.program_id(0)
    half = tb // 2
    col_tile = D // 2

    def stage1_tasks(x_ref, slot):
        def cast():
            xb_scr[slot] = x_ref[...].astype(bf16)

        def proj_bg(c0):
            def run():
                bg_scr[slot, :, c0:c0 + col_tile] = _dot(xb_scr[slot], win_ref[:, c0:c0 + col_tile])
            return run

        def proj_cg(c0):
            def run():
                cg_scr[:, c0:c0 + col_tile] = _dot(xb_scr[slot], win_ref[:, D + c0:D + c0 + col_tile])
            return run

        def proj_u(c0):
            def run():
                hh = _dot(xb_scr[slot], win_ref[:, 2 * D + c0:2 * D + c0 + col_tile])
                u_scr[slot, :, c0:c0 + col_tile] = cg_scr[:, c0:c0 + col_tile] * hh
            return run

        cols = range(0, D, col_tile)
        return cast, ([proj_bg(c) for c in cols] + [proj_cg(c) for c in cols] + [proj_u(c) for c in cols])

    def stage2_tasks(r0, slot):
        def conv(m0):
            def run():
                u = u_scr[slot, m0:m0 + half, :]
                t = lax.broadcasted_iota(jnp.int32, (half, D), 0)
                c0 = c_scr[0:1, :]
                c1 = c_scr[1:2, :]
                p1 = jnp.where(t == 0, c1, pltpu.roll(u, 1, 0))
                p2 = jnp.where(t == 0, c0, jnp.where(t == 1, c1, pltpu.roll(u, 2, 0)))
                cv = p2 * wc_ref[0:1, :] + p1 * wc_ref[1:2, :] + u * wc_ref[2:3, :]
                gated_scr[m0:m0 + half, :] = (bg_scr[slot, m0:m0 + half, :] * cv).astype(bf16)
                c_scr[0:2, :] = u[half - 2:half, :]
            return run

        def out(m0):
            def run():
                y = _dot(gated_scr[m0:m0 + half, :], wout_ref[...])
                x = xres_ref[r0 + m0:r0 + m0 + half, :]
                y_ref[r0 + m0:r0 + m0 + half, :] = _layer_norm(ALPHA * x + y, lg_ref[...], lb_ref[...])
            return run

        return [conv(0), conv(half), out(0), out(half)]

    @pl.when(i == 0)
    def _():
        cast, tiles = stage1_tasks(x0_ref, 0)
        cast()
        for task in tiles:
            task()

    @pl.when(i % steps_per_seq == 0)
    def _():
        c_scr[...] = jnp.zeros_like(c_scr)

    for r0, slot, x_next in ((0, 0, xa_ref), (tb, 1, xb_ref)):
        cast, tiles = stage1_tasks(x_next, 1 - slot)
        cast()
        _interleave(stage2_tasks(r0, slot), tiles)

    st_ref[0] = c_scr[0:2, :]


def _conv_sample_kernel(x_ref, e1_ref, e2_ref, win_ref, wc_ref, wout_ref, lg_ref, lb_ref, y_ref, u_ref,
                        xb_scr, p_scr, *, t_seq):
    c = pl.program_id(0)
    n_proj = p_scr.shape[0]

    @pl.when(c == 0)
    def _():
        xb_scr[...] = x_ref[...].astype(bf16)

    @pl.when(c < n_proj)
    def _():
        p_scr[c] = _dot(xb_scr[...], win_ref[...])

    @pl.when(c == n_proj)
    def _():
        x = x_ref[...]
        u = p_scr[1] * p_scr[2]
        t = lax.broadcasted_iota(jnp.int32, u.shape, 0) % t_seq
        p1 = jnp.where(t >= 1, pltpu.roll(u, 1, 0), e1_ref[...])
        p2 = jnp.where(t >= 2, pltpu.roll(u, 2, 0), e2_ref[...])
        conv = p2 * wc_ref[0:1, :] + p1 * wc_ref[1:2, :] + u * wc_ref[2:3, :]
        y = _dot((p_scr[0] * conv).astype(bf16), wout_ref[...])
        y_ref[...] = _layer_norm(ALPHA * x + y, lg_ref[...], lb_ref[...])
        u_ref[...] = u


def _mlp_kernel(x_ref, wup_ref, wdn_ref, lg_ref, lb_ref, y_ref, xb_scr, h_scr, *, n_sub, sub_rows, ff_chunk):
    def sub_tasks(s):
        rs = slice(s * sub_rows, (s + 1) * sub_rows)
        buf = s % 2

        def cast():
            xb_scr[buf] = x_ref[rs, :].astype(bf16)

        def up(c):
            cs = slice(c * ff_chunk, (c + 1) * ff_chunk)

            def run():
                hcol = _dot(xb_scr[buf], wup_ref[0, :, cs])
                h_scr[buf, :, cs] = jnp.square(jnp.maximum(hcol, 0.0)).astype(bf16)
            return run

        def down():
            y = _dot(h_scr[buf], wdn_ref[0])
            y_ref[rs, :] = _layer_norm(ALPHA * x_ref[rs, :] + y, lg_ref[0], lb_ref[0])

        return [cast] + [up(c) for c in range(D_FF // ff_chunk)], down

    pending = None
    for s in range(n_sub):
        ups, down = sub_tasks(s)
        ups[0]()
        ups[1]()
        if pending is not None:
            pending()
        for task in ups[2:]:
            task()
        pending = down
    pending()


def _const_spec(shape):
    nd = len(shape)
    return pl.BlockSpec(shape, lambda *_: (0,) * nd, pipeline_mode=pl.Buffered(1))


def _layer_spec(shape, layer):
    nd = len(shape)
    return pl.BlockSpec((1,) + tuple(shape[1:]), lambda *_: (layer,) + (0,) * (nd - 1),
                        pipeline_mode=pl.Buffered(1))


def _mlp_stream_kernel(x_ref, wup_ref, wdn_ref, lg_ref, lb_ref, y_ref, xb_scr, acc_scr):
    c = pl.program_id(0)

    @pl.when(c == 0)
    def _():
        xb_scr[...] = x_ref[...].astype(bf16)

    hcol = _dot(xb_scr[...], wup_ref[0])
    part = _dot(jnp.square(jnp.maximum(hcol, 0.0)).astype(bf16), wdn_ref[0])

    @pl.when(c == 0)
    def _():
        acc_scr[...] = part

    @pl.when(c > 0)
    def _():
        acc_scr[...] += part

    @pl.when(c == pl.num_programs(0) - 1)
    def _():
        y_ref[...] = _layer_norm(ALPHA * x_ref[...] + acc_scr[...], lg_ref[0], lb_ref[0])


def _mlp_stream(x2d, wup, wdn, lg, lb, layer, ff_chunk):
    m = x2d.shape[0]
    return pl.pallas_call(
        _mlp_stream_kernel,
        grid=(D_FF // ff_chunk,),
        in_specs=[pl.BlockSpec((m, D), lambda c: (0, 0)),
                  pl.BlockSpec((1, D, ff_chunk), lambda c: (layer, 0, c)),
                  pl.BlockSpec((1, ff_chunk, D), lambda c: (layer, c, 0)),
                  _layer_spec(lg.shape, layer), _layer_spec(lb.shape, layer)],
        out_specs=pl.BlockSpec((m, D), lambda c: (0, 0)),
        out_shape=jax.ShapeDtypeStruct((m, D), f32),
        scratch_shapes=[pltpu.VMEM((m, D), bf16), pltpu.VMEM((m, D), f32)],
        compiler_params=pltpu.CompilerParams(dimension_semantics=("arbitrary",), vmem_limit_bytes=VMEM_LIMIT),
        name="mlp_stream",
    )(x2d, wup, wdn, lg, lb)


def _mlp(x2d, wup, wdn, lg, lb, layer, n_sub, sub_rows):
    m = x2d.shape[0]
    tm = n_sub * sub_rows
    assert m % tm == 0
    return pl.pallas_call(
        functools.partial(_mlp_kernel, n_sub=n_sub, sub_rows=sub_rows, ff_chunk=1024),
        grid=(m // tm,),
        in_specs=[pl.BlockSpec((tm, D), lambda i: (i, 0)),
                  _layer_spec(wup.shape, layer), _layer_spec(wdn.shape, layer),
                  _layer_spec(lg.shape, layer), _layer_spec(lb.shape, layer)],
        out_specs=pl.BlockSpec((tm, D), lambda i: (i, 0)),
        out_shape=jax.ShapeDtypeStruct((m, D), f32),
        scratch_shapes=[pltpu.VMEM((2, sub_rows, D), bf16), pltpu.VMEM((2, sub_rows, D_FF), bf16)],
        compiler_params=pltpu.CompilerParams(dimension_semantics=("arbitrary",), vmem_limit_bytes=VMEM_LIMIT),
        name="mlp",
    )(x2d, wup, wdn, lg, lb)


def _gla_prompt(x, wq, wgl, wgu, bgate, ng, wo, lg, lb, tb, side):
    bsz, t, _ = x.shape
    assert tb % KCHUNK == 0 and t % (2 * tb) == 0
    n_chunks = tb // KCHUNK
    n_blocks = bsz * t // tb
    n_steps = n_blocks // 2
    steps_per_seq = t // (2 * tb)
    assert all(a.shape[0] % (16 * n_steps) == 0 for a in side)
    x2d = x.reshape(bsz * t, D)
    side_specs = [pl.BlockSpec((a.shape[0] // n_steps, a.shape[1]), lambda i: (i, 0)) for a in side]
    outs = pl.pallas_call(
        functools.partial(_gla_prompt_kernel, n_chunks=n_chunks, steps_per_seq=steps_per_seq, n_side=len(side)),
        grid=(n_steps,),
        in_specs=[pl.BlockSpec((tb, D), lambda i: (0, 0), pipeline_mode=pl.Buffered(1)),
                  pl.BlockSpec((tb, D), lambda i: (2 * i + 1, 0)),
                  pl.BlockSpec((tb, D), lambda i: (jnp.minimum(2 * i + 2, n_blocks - 1), 0)),
                  pl.BlockSpec((2 * tb, D), lambda i: (i, 0)),
                  _const_spec(wq.shape), _const_spec(wgl.shape), _const_spec(wgu.shape),
                  _const_spec(bgate.shape), _const_spec(ng.shape), _const_spec(wo.shape),
                  _const_spec(lg.shape), _const_spec(lb.shape)] + side_specs,
        out_specs=[pl.BlockSpec((2 * tb, D), lambda i: (i, 0)),
                   pl.BlockSpec((1, H, DKH, DVH), lambda i: (i // steps_per_seq, 0, 0, 0))] + side_specs,
        out_shape=[jax.ShapeDtypeStruct((bsz * t, D), f32),
                   jax.ShapeDtypeStruct((bsz, H, DKH, DVH), f32)]
                  + [jax.ShapeDtypeStruct(a.shape, bf16) for a in side],
        scratch_shapes=[pltpu.VMEM((2, tb, D), bf16),
                        pltpu.VMEM((2, tb, DK), f32), pltpu.VMEM((2, tb, DK), f32),
                        pltpu.VMEM((2, tb, DV), f32), pltpu.VMEM((2, tb, DV), f32),
                        pltpu.VMEM((2, tb, DK), f32),
                        pltpu.VMEM((tb, DK), f32), pltpu.VMEM((tb, DV), f32), pltpu.VMEM((tb, DV), bf16),
                        pltpu.VMEM((H, DKH, DVH), f32)],
        compiler_params=pltpu.CompilerParams(dimension_semantics=("arbitrary",),
                                             vmem_limit_bytes=VMEM_LIMIT),
        name="gla_prompt",
    )(x2d, x2d, x2d, x2d, wq, wgl, wgu, bgate, ng, wo, lg, lb, *side)
    return outs[0].reshape(bsz, t, D), outs[1], outs[2:]


def _gla_sample(x_pad, state, wq, wgl, wgu, bgate, ng, wo, lg, lb, n_seq, t_valid):
    s_total = state.shape[0]
    rows = n_seq * SUB
    assert s_total % n_seq == 0 and rows % 128 == 0
    return pl.pallas_call(
        functools.partial(_gla_sample_kernel, n_seq=n_seq, t_valid=t_valid),
        grid=(s_total // n_seq,),
        in_specs=[pl.BlockSpec((rows, D), lambda i: (i, 0)),
                  pl.BlockSpec((n_seq, H, DKH, DVH), lambda i: (i, 0, 0, 0)),
                  _const_spec(wq.shape), _const_spec(wgl.shape), _const_spec(wgu.shape),
                  _const_spec(bgate.shape), _const_spec(ng.shape), _const_spec(wo.shape),
                  _const_spec(lg.shape), _const_spec(lb.shape)],
        out_specs=[pl.BlockSpec((rows, D), lambda i: (i, 0)),
                   pl.BlockSpec((n_seq, H, DKH, DVH), lambda i: (i, 0, 0, 0))],
        out_shape=[jax.ShapeDtypeStruct((s_total * SUB, D), f32),
                   jax.ShapeDtypeStruct(state.shape, f32)],
        scratch_shapes=[pltpu.VMEM((rows, DV), f32)],
        compiler_params=pltpu.CompilerParams(dimension_semantics=("arbitrary",), vmem_limit_bytes=VMEM_LIMIT),
        name="gla_sample",
    )(x_pad, state, wq, wgl, wgu, bgate, ng, wo, lg, lb)


def _conv_prompt(x, win, wc, wout, lg, lb, tb):
    bsz, t, _ = x.shape
    assert t % (2 * tb) == 0
    n_blocks = bsz * t // tb
    steps_per_seq = t // (2 * tb)
    x2d = x.reshape(bsz * t, D)
    y, st = pl.pallas_call(
        functools.partial(_conv_prompt_kernel, tb=tb, steps_per_seq=steps_per_seq),
        grid=(n_blocks // 2,),
        in_specs=[pl.BlockSpec((tb, D), lambda i: (0, 0), pipeline_mode=pl.Buffered(1)),
                  pl.BlockSpec((tb, D), lambda i: (2 * i + 1, 0)),
                  pl.BlockSpec((tb, D), lambda i: (jnp.minimum(2 * i + 2, n_blocks - 1), 0)),
                  pl.BlockSpec((2 * tb, D), lambda i: (i, 0)),
                  _const_spec(win.shape), _const_spec(wc.shape), _const_spec(wout.shape),
                  _const_spec(lg.shape), _const_spec(lb.shape)],
        out_specs=[pl.BlockSpec((2 * tb, D), lambda i: (i, 0)),
                   pl.BlockSpec((1, 2, D), lambda i: (i // steps_per_seq, 0, 0))],
        out_shape=[jax.ShapeDtypeStruct((bsz * t, D), f32),
                   jax.ShapeDtypeStruct((bsz, 2, D), f32)],
        scratch_shapes=[pltpu.VMEM((2, tb, D), bf16), pltpu.VMEM((2, tb, D), f32),
                        pltpu.VMEM((tb, D), f32), pltpu.VMEM((2, tb, D), f32),
                        pltpu.VMEM((tb, D), bf16), pltpu.VMEM((SUB, D), f32)],
        compiler_params=pltpu.CompilerParams(dimension_semantics=("arbitrary",), vmem_limit_bytes=VMEM_LIMIT),
        name="conv_prompt",
    )(x2d, x2d, x2d, x2d, win, wc, wout, lg, lb)
    return y.reshape(bsz, t, D), st


def _conv_sample(x2d, e1, e2, win, wc, wout, lg, lb, t_seq):
    m = x2d.shape[0]
    n_proj = win.shape[1] // D
    full = lambda shape: pl.BlockSpec(shape, lambda i: (0,) * len(shape))
    return pl.pallas_call(
        functools.partial(_conv_sample_kernel, t_seq=t_seq),
        grid=(n_proj + 1,),
        in_specs=[full((m, D)), full((m, D)), full((m, D)),
                  pl.BlockSpec((D, D), lambda c: (0, jnp.minimum(c, n_proj - 1))),
                  _const_spec(wc.shape), _const_spec(wout.shape),
                  _const_spec(lg.shape), _const_spec(lb.shape)],
        out_specs=[full((m, D)), full((m, D))],
        out_shape=[jax.ShapeDtypeStruct((m, D), f32), jax.ShapeDtypeStruct((m, D), f32)],
        scratch_shapes=[pltpu.VMEM((m, D), bf16), pltpu.VMEM((n_proj, m, D), f32)],
        compiler_params=pltpu.CompilerParams(dimension_semantics=("arbitrary",), vmem_limit_bytes=VMEM_LIMIT),
        name="conv_sample",
    )(x2d, e1, e2, win, wc, wout, lg, lb)


def kernel(x_prompt, x_sample, state_gla, state_conv, gla_w_in, gla_w_gate_up, gla_b_gate, gla_norm_g, gla_w_o, conv_w_in, conv_w_conv, conv_w_out, mlp_w_up, mlp_w_down, ln1_g, ln1_b, ln2_g, ln2_b):
    bsz, t, _ = x_prompt.shape
    n_dec, t_dec, _ = x_sample.shape
    assert t % CHUNK == 0 and t_dec <= SUB and t_dec >= 2

    w_in = gla_w_in[0]
    wq = w_in.astype(bf16)
    wgl = jnp.pad(w_in[:, 2 * DK + 2 * DV:], ((0, 0), (0, RANK_PAD - RANK))).astype(bf16)
    wgu = jnp.pad(gla_w_gate_up[0], ((0, RANK_PAD - RANK), (0, 0))).astype(bf16)
    bgate = gla_b_gate[0].reshape(1, DK)
    ng = gla_norm_g[0].reshape(1, DV)
    wo = gla_w_o[0].astype(bf16)
    cwc = conv_w_conv[0]
    ln2g = ln2_g.reshape(DEPTH, 1, D)
    ln2b = ln2_b.reshape(DEPTH, 1, D)
    row = lambda a, i: a[i].reshape(1, D)
    mlp_prompt = functools.partial(_mlp, n_sub=4, sub_rows=256)
    mlp_sample = functools.partial(_mlp_stream, ff_chunk=1024)

    side = (mlp_w_up.reshape(DEPTH * D, D_FF), mlp_w_down.reshape(DEPTH * D_FF, D), conv_w_in[0], conv_w_out[0])
    xp, gla_p, (wup, wdn, cwin, cwout) = _gla_prompt(x_prompt, wq, wgl, wgu, bgate, ng, wo, row(ln1_g, 0),
                                                     row(ln1_b, 0), tb=256, side=side)
    wup = wup.reshape(DEPTH, D, D_FF)
    wdn = wdn.reshape(DEPTH, D_FF, D)
    xs_pad = jnp.pad(x_sample, ((0, 0), (0, SUB - t_dec), (0, 0))).reshape(n_dec * SUB, D)
    xs_pad, gla_s = _gla_sample(xs_pad, state_gla[0], wq, wgl, wgu, bgate, ng, wo, row(ln1_g, 0), row(ln1_b, 0),
                                n_seq=16, t_valid=t_dec)
    xs = xs_pad.reshape(n_dec, SUB, D)[:, :t_dec].reshape(n_dec * t_dec, D)
    xp = mlp_prompt(xp.reshape(bsz * t, D), wup, wdn, ln2g, ln2b, layer=0)
    xs = mlp_sample(xs, wup, wdn, ln2g, ln2b, layer=0)

    xp, conv_p = _conv_prompt(xp.reshape(bsz, t, D), cwin, cwc, cwout, row(ln1_g, 1), row(ln1_b, 1), tb=512)
    buf = state_conv[0]
    e1 = jnp.pad(buf[:, 1:2], ((0, 0), (0, t_dec - 1), (0, 0))).reshape(n_dec * t_dec, D)
    e2 = jnp.pad(buf, ((0, 0), (0, t_dec - 2), (0, 0))).reshape(n_dec * t_dec, D)
    xs, u_s = _conv_sample(xs, e1, e2, cwin, cwc, cwout, row(ln1_g, 1), row(ln1_b, 1), t_seq=t_dec)
    conv_s = u_s.reshape(n_dec, t_dec, D)[:, t_dec - 2:]
    xp = mlp_prompt(xp.reshape(bsz * t, D), wup, wdn, ln2g, ln2b, layer=1)
    xs = mlp_sample(xs, wup, wdn, ln2g, ln2b, layer=1)

    return (xp.reshape(bsz, t, D), xs.reshape(n_dec, t_dec, D), gla_p[None], gla_s[None],
            conv_p[None], conv_s[None])
```

```python
import functools

import jax
import jax.numpy as jnp
from jax import lax
from jax.experimental import pallas as pl
from jax.experimental.pallas import tpu as pltpu

bf16 = jnp.bfloat16
f32 = jnp.float32

D = 1024
H = 4
DK = 512
DV = 1024
DKH = DK // H
DVH = DV // H
RANK = 16
RANK_PAD = 128
TAU = 16.0
CHUNK = 64
KCHUNK = 128
D_FF = 4 * D
DEPTH = 2
ALPHA = (2 * DEPTH) ** 0.25
LN_EPS = 1e-5
RMS_EPS = 1e-6
Q_SCALE = DKH ** -0.5

SUB = 8
VMEM_LIMIT = 56 * 1024 * 1024

_NT = (((1,), (1,)), ((), ()))
_TN = (((0,), (0,)), ((), ()))


def _dot(a, b):
    return jnp.dot(a, b, preferred_element_type=f32)


def _dg(a, b, dims):
    return lax.dot_general(a, b, dims, preferred_element_type=f32)


def _layer_norm(y, g, b):
    mu = jnp.mean(y, axis=-1, keepdims=True)
    yc = y - mu
    var = jnp.mean(yc * yc, axis=-1, keepdims=True)
    return yc * lax.rsqrt(var + LN_EPS) * g + b


def _log_sigmoid(z):
    return -(jnp.maximum(-z, 0.0) + jnp.log(1.0 + jnp.exp(-jnp.abs(z))))


def _gla_project(xb, wq_ref, wgl_ref, wgu_ref, bgate_ref):
    q = _dot(xb, wq_ref[:, 0:DK]) * Q_SCALE
    k = _dot(xb, wq_ref[:, DK:2 * DK])
    v = _dot(xb, wq_ref[:, 2 * DK:2 * DK + DV])
    r = _dot(xb, wq_ref[:, 2 * DK + DV:2 * DK + 2 * DV])
    gl = _dot(xb, wgl_ref[...])
    z = _dot(gl.astype(bf16), wgu_ref[...]) + bgate_ref[...]
    g = _log_sigmoid(z) / TAU
    return q, k, v, r, g


def _gla_post(o, r, x, ng, wo_ref, lg, lb):
    parts = []
    for h in range(H):
        vs = slice(h * DVH, (h + 1) * DVH)
        oh = o[:, vs]
        ms = jnp.mean(oh * oh, axis=-1, keepdims=True)
        parts.append(oh * lax.rsqrt(ms + RMS_EPS) * ng[:, vs])
    on = jnp.concatenate(parts, axis=1)
    gated = on * (r * jax.nn.sigmoid(r))
    y = _dot(gated.astype(bf16), wo_ref[...])
    return _layer_norm(ALPHA * x + y, lg, lb)


PROJ_TILE = 512


def _interleave(main_tasks, filler_tasks):
    n_main, n_fill = len(main_tasks), len(filler_tasks)
    done = 0
    for idx, task in enumerate(main_tasks):
        task()
        want = ((idx + 1) * n_fill) // n_main
        while done < want:
            filler_tasks[done]()
            done += 1


def _gla_stage1_tasks(x_ref, slot, w_refs, scr):
    wq_ref, wgl_ref, wgu_ref, bgate_ref = w_refs
    xb_scr, q_scr, k_scr, v_scr, r_scr, g_scr = scr[:6]

    def cast_x():
        xb_scr[slot] = x_ref[...].astype(bf16)

    def proj(dst, c0, w0, scale):
        def run():
            acc = _dot(xb_scr[slot], wq_ref[:, w0:w0 + PROJ_TILE])
            dst[slot, :, c0:c0 + PROJ_TILE] = acc * scale if scale is not None else acc
        return run

    def gate():
        gl = _dot(xb_scr[slot], wgl_ref[...])
        z = _dot(gl.astype(bf16), wgu_ref[...]) + bgate_ref[...]
        g_scr[slot] = _log_sigmoid(z) / TAU

    tiles = []
    for dst, w_base, width, scale in ((q_scr, 0, DK, Q_SCALE), (k_scr, DK, DK, None),
                                      (v_scr, 2 * DK, DV, None), (r_scr, 2 * DK + DV, DV, None)):
        for c0 in range(0, width, PROJ_TILE):
            tiles.append(proj(dst, c0, w_base + c0, scale))
    return cast_x, gate, tiles


def _split3(a):
    a1 = a.astype(bf16)
    r1 = a - a1.astype(f32)
    a2 = r1.astype(bf16)
    a3 = (r1 - a2.astype(f32)).astype(bf16)
    return a1, a2, a3


def _gla_stage2_tasks(xres_ref, y_ref, r0, slot, scr, s_scr, ng_ref, wo_ref, lg_ref, lb_ref, n_chunks):
    _, q_scr, k_scr, v_scr, r_scr, g_scr, b_scr, o_scr, gated_scr = scr
    tb = n_chunks * KCHUNK
    mid = KCHUNK // 2

    def cumsum():
        row = lax.broadcasted_iota(jnp.int32, (tb, tb), 0)
        col = lax.broadcasted_iota(jnp.int32, (tb, tb), 1)
        tri = ((row >= col) & ((row // KCHUNK) == (col // KCHUNK))).astype(bf16)
        g1, g2, g3 = _split3(g_scr[slot])
        b_scr[...] = _dot(tri, g1) + _dot(tri, g2) + _dot(tri, g3)

    def chunk_prep(c, cell):
        rs = slice(c * KCHUNK, (c + 1) * KCHUNK)

        def run():
            b = b_scr[rs, :]
            b_mid = b[mid - 1:mid, :]
            b_last = b[KCHUNK - 1:KCHUNK, :]
            qc = q_scr[slot, rs, :]
            kc = k_scr[slot, rs, :]
            cell["qs"] = (qc * jnp.exp(b - b_mid)).astype(bf16)
            cell["ks"] = (kc * jnp.exp(b_mid - b)).astype(bf16)
            cell["qd"] = (qc * jnp.exp(b)).astype(bf16)
            cell["kk"] = (kc * jnp.exp(b_last - b)).astype(bf16)
            cell["vb"] = v_scr[slot, rs, :].astype(bf16)
            cell["dec_t"] = jnp.exp(jnp.broadcast_to(b_last, (KCHUNK, DK)).T)
        return run

    def head_scores(h, cell):
        ks = slice(h * DKH, (h + 1) * DKH)

        def run():
            ri = lax.broadcasted_iota(jnp.int32, (KCHUNK, KCHUNK), 0)
            ci = lax.broadcasted_iota(jnp.int32, (KCHUNK, KCHUNK), 1)
            sc = _dg(cell["qs"][:, ks], cell["ks"][:, ks], _NT)
            cell["sc", h] = jnp.where(ri >= ci, sc, 0.0).astype(bf16)
        return run

    def head_update(c, h, cell):
        rs = slice(c * KCHUNK, (c + 1) * KCHUNK)
        ks = slice(h * DKH, (h + 1) * DKH)
        vs = slice(h * DVH, (h + 1) * DVH)

        def run():
            st = s_scr[h]
            vh = cell["vb"][:, vs]
            lhs = jnp.concatenate([cell["sc", h], cell["qd"][:, ks]], axis=1)
            rhs = jnp.concatenate([vh, st.astype(bf16)], axis=0)
            o_scr[rs, vs] = _dot(lhs, rhs)
            dec_h = cell["dec_t"][ks, :]
            dec_m = jnp.concatenate([dec_h] * (DVH // KCHUNK), axis=1)
            s_scr[h] = dec_m * st + _dg(cell["kk"][:, ks], vh, _TN)
        return run

    def norm_gate(h):
        vs = slice(h * DVH, (h + 1) * DVH)

        def run():
            oh = o_scr[:, vs]
            rh = r_scr[slot, :, vs]
            ms = jnp.mean(oh * oh, axis=-1, keepdims=True)
            on = oh * lax.rsqrt(ms + RMS_EPS) * ng_ref[:, vs]
            gated_scr[:, vs] = (on * (rh * jax.nn.sigmoid(rh))).astype(bf16)
        return run

    def out_rows(m0, m1):
        def run():
            y = _dot(gated_scr[m0:m1, :], wo_ref[...])
            x = xres_ref[r0 + m0:r0 + m1, :]
            y_ref[r0 + m0:r0 + m1, :] = _layer_norm(ALPHA * x + y, lg_ref[...], lb_ref[...])
        return run

    tasks = [cumsum]
    for c in range(n_chunks):
        cell = {}
        tasks.append(chunk_prep(c, cell))
        tasks += [head_scores(h, cell) for h in range(H)]
        tasks += [head_update(c, h, cell) for h in range(H)]
    tasks += [norm_gate(h) for h in range(H)]
    half = tb // 2
    tasks += [out_rows(0, half), out_rows(half, tb)]
    return tasks


def _gla_prompt_kernel(x0_ref, xa_ref, xb_ref, xres_ref, wq_ref, wgl_ref, wgu_ref, bgate_ref, ng_ref, wo_ref,
                       lg_ref, lb_ref, *refs, n_chunks, steps_per_seq, n_side):
    side_in = refs[:n_side]
    y_ref, st_ref = refs[n_side:n_side + 2]
    side_out = refs[n_side + 2:2 * n_side + 2]
    (xb_scr, q_scr, k_scr, v_scr, r_scr, g_scr, b_scr, o_scr, gated_scr, s_scr) = refs[2 * n_side + 2:]
    i = pl.program_id(0)
    tb = n_chunks * KCHUNK
    w_refs = (wq_ref, wgl_ref, wgu_ref, bgate_ref)
    scr = (xb_scr, q_scr, k_scr, v_scr, r_scr, g_scr, b_scr, o_scr, gated_scr)
    post = (ng_ref, wo_ref, lg_ref, lb_ref)

    @pl.when(i == 0)
    def _():
        cast_x, gate, tiles = _gla_stage1_tasks(x0_ref, 0, w_refs, scr)
        for task in [cast_x, gate] + tiles:
            task()

    @pl.when(i % steps_per_seq == 0)
    def _():
        s_scr[...] = jnp.zeros_like(s_scr)

    def side_cast(src_ref, dst_ref):
        def run():
            dst_ref[...] = src_ref[...].astype(bf16)
        return run

    side_tasks = [side_cast(s, d) for s, d in zip(side_in, side_out)]
    for r0, slot, x_next in ((0, 0, xa_ref), (tb, 1, xb_ref)):
        cast_x, gate, tiles = _gla_stage1_tasks(x_next, 1 - slot, w_refs, scr)
        stage2 = _gla_stage2_tasks(xres_ref, y_ref, r0, slot, scr, s_scr, *post, n_chunks)
        cast_x()
        fillers = tiles + side_tasks[slot::2]
        _interleave([stage2[0], gate] + stage2[1:], fillers)

    @pl.when(i % steps_per_seq == steps_per_seq - 1)
    def _():
        for h in range(H):
            st_ref[0, h] = s_scr[h]


def _gla_sample_kernel(x_ref, st_ref, wq_ref, wgl_ref, wgu_ref, bgate_ref, ng_ref, wo_ref, lg_ref, lb_ref,
                       y_ref, sto_ref, o_scr, *, n_seq, t_valid):
    rows = n_seq * SUB
    x = x_ref[...]
    q, k, v, r, g = _gla_project(x.astype(bf16), wq_ref, wgl_ref, wgu_ref, bgate_ref)

    row = lax.broadcasted_iota(jnp.int32, (rows, rows), 0)
    col = lax.broadcasted_iota(jnp.int32, (rows, rows), 1)
    same_seq = (row // SUB) == (col // SUB)
    tri = (same_seq & (row >= col)).astype(f32)
    tri_last = (same_seq & ((col % SUB) < t_valid)).astype(f32)
    b = jnp.dot(tri, g, precision=lax.Precision.HIGHEST, preferred_element_type=f32)
    b_last = jnp.dot(tri_last, g, precision=lax.Precision.HIGHEST, preferred_element_type=f32)

    valid = (lax.broadcasted_iota(jnp.int32, (rows, DK), 0) % SUB) < t_valid
    qd = (q * jnp.exp(b)).astype(bf16)
    kd = jnp.where(valid, k * jnp.exp(-b), 0.0).astype(bf16)
    kk_t = jnp.where(valid, k * jnp.exp(b_last - b), 0.0).T.astype(bf16)
    vb = v.astype(bf16)
    dec_t = jnp.exp(b_last.T)

    intra_mask = same_seq & (row >= col)
    for h in range(H):
        ks = slice(h * DKH, (h + 1) * DKH)
        vs = slice(h * DVH, (h + 1) * DVH)
        sc = jnp.where(intra_mask, _dg(qd[:, ks], kd[:, ks], _NT), 0.0).astype(bf16)
        o_scr[:, vs] = _dot(sc, vb[:, vs])

    seq_of_col = lax.broadcasted_iota(jnp.int32, (DKH, rows), 1) // SUB

    def seq_dots(n):
        rs = slice(n * SUB, (n + 1) * SUB)
        res = []
        for h in range(H):
            ks = slice(h * DKH, (h + 1) * DKH)
            vs = slice(h * DVH, (h + 1) * DVH)
            s0 = st_ref[n, h]
            o_inter = _dot(qd[rs, ks], s0.astype(bf16))
            upd = _dot(jnp.where(seq_of_col == n, kk_t[ks, :], jnp.zeros((), bf16)), vb[:, vs])
            res.append((s0, o_inter, upd))
        return res

    def seq_combine(n, res):
        rs = slice(n * SUB, (n + 1) * SUB)
        for h, (s0, o_inter, upd) in enumerate(res):
            ks = slice(h * DKH, (h + 1) * DKH)
            vs = slice(h * DVH, (h + 1) * DVH)
            o_scr[rs, vs] = o_scr[rs, vs] + o_inter
            sto_ref[n, h] = dec_t[ks, n * SUB:n * SUB + 1] * s0 + upd

    pending = None
    for n in range(n_seq):
        res = seq_dots(n)
        if pending is not None:
            seq_combine(*pending)
        pending = (n, res)
    seq_combine(*pending)

    y_ref[...] = _gla_post(o_scr[...], r, x, ng_ref[...], wo_ref, lg_ref[...], lb_ref[...])


def _conv_prompt_kernel(x0_ref, xa_ref, xb_ref, xres_ref, win_ref, wc_ref, wout_ref, lg_ref, lb_ref,
                        y_ref, st_ref, xb_scr, bg_scr, cg_scr, u_scr, gated_scr, c_scr,
                        *, tb, steps_per_seq):
    i = pl.program_id(0)
    half = tb // 2
    col_tile = D // 2

    def stage1_tasks(x_ref, slot):
        def cast():
            xb_scr[slot] = x_ref[...].astype(bf16)

        def proj_bg(c0):
            def run():
                bg_scr[slot, :, c0:c0 + col_tile] = _dot(xb_scr[slot], win_ref[:, c0:c0 + col_tile])
            return run

        def proj_cg(c0):
            def run():
                cg_scr[:, c0:c0 + col_tile] = _dot(xb_scr[slot], win_ref[:, D + c0:D + c0 + col_tile])
            return run

        def proj_u(c0):
            def run():
                hh = _dot(xb_scr[slot], win_ref[:, 2 * D + c0:2 * D + c0 + col_tile])
                u_scr[slot, :, c0:c0 + col_tile] = cg_scr[:, c0:c0 + col_tile] * hh
            return run

        cols = range(0, D, col_tile)
        return cast, ([proj_bg(c) for c in cols] + [proj_cg(c) for c in cols] + [proj_u(c) for c in cols])

    def stage2_tasks(r0, slot):
        def conv(m0):
            def run():
                u = u_scr[slot, m0:m0 + half, :]
                t = lax.broadcasted_iota(jnp.int32, (half, D), 0)
                c0 = c_scr[0:1, :]
                c1 = c_scr[1:2, :]
                p1 = jnp.where(t == 0, c1, pltpu.roll(u, 1, 0))
                p2 = jnp.where(t == 0, c0, jnp.where(t == 1, c1, pltpu.roll(u, 2, 0)))
                cv = p2 * wc_ref[0:1, :] + p1 * wc_ref[1:2, :] + u * wc_ref[2:3, :]
                gated_scr[m0:m0 + half, :] = (bg_scr[slot, m0:m0 + half, :] * cv).astype(bf16)
                c_scr[0:2, :] = u[half - 2:half, :]
            return run

        def out(m0):
            def run():
                y = _dot(gated_scr[m0:m0 + half, :], wout_ref[...])
                x = xres_ref[r0 + m0:r0 + m0 + half, :]
                y_ref[r0 + m0:r0 + m0 + half, :] = _layer_norm(ALPHA * x + y, lg_ref[...], lb_ref[...])
            return run

        return [conv(0), conv(half), out(0), out(half)]

    @pl.when(i == 0)
    def _():
        cast, tiles = stage1_tasks(x0_ref, 0)
        cast()
        for task in tiles:
            task()

    @pl.when(i % steps_per_seq == 0)
    def _():
        c_scr[...] = jnp.zeros_like(c_scr)

    for r0, slot, x_next in ((0, 0, xa_ref), (tb, 1, xb_ref)):
        cast, tiles = stage1_tasks(x_next, 1 - slot)
        cast()
        _interleave(stage2_tasks(r0, slot), tiles)

    st_ref[0] = c_scr[0:2, :]


def _conv_sample_kernel(x_ref, e1_ref, e2_ref, win_ref, wc_ref, wout_ref, lg_ref, lb_ref, y_ref, u_ref,
                        xb_scr, p_scr, *, t_seq):
    c = pl.program_id(0)
    n_proj = p_scr.shape[0]

    @pl.when(c == 0)
    def _():
        xb_scr[...] = x_ref[...].astype(bf16)

    @pl.when(c < n_proj)
    def _():
        p_scr[c] = _dot(xb_scr[...], win_ref[...])

    @pl.when(c == n_proj)
    def _():
        x = x_ref[...]
        u = p_scr[1] * p_scr[2]
        t = lax.broadcasted_iota(jnp.int32, u.shape, 0) % t_seq
        p1 = jnp.where(t >= 1, pltpu.roll(u, 1, 0), e1_ref[...])
        p2 = jnp.where(t >= 2, pltpu.roll(u, 2, 0), e2_ref[...])
        conv = p2 * wc_ref[0:1, :] + p1 * wc_ref[1:2, :] + u * wc_ref[2:3, :]
        y = _dot((p_scr[0] * conv).astype(bf16), wout_ref[...])
        y_ref[...] = _layer_norm(ALPHA * x + y, lg_ref[...], lb_ref[...])
        u_ref[...] = u


def _mlp_kernel(x_ref, wup_ref, wdn_ref, lg_ref, lb_ref, y_ref, xb_scr, h_scr, *, n_sub, sub_rows, ff_chunk):
    def sub_tasks(s):
        rs = slice(s * sub_rows, (s + 1) * sub_rows)
        buf = s % 2

        def cast():
            xb_scr[buf] = x_ref[rs, :].astype(bf16)

        def up(c):
            cs = slice(c * ff_chunk, (c + 1) * ff_chunk)

            def run():
                hcol = _dot(xb_scr[buf], wup_ref[0, :, cs])
                h_scr[buf, :, cs] = jnp.square(jnp.maximum(hcol, 0.0)).astype(bf16)
            return run

        def down():
            y = _dot(h_scr[buf], wdn_ref[0])
            y_ref[rs, :] = _layer_norm(ALPHA * x_ref[rs, :] + y, lg_ref[0], lb_ref[0])

        return [cast] + [up(c) for c in range(D_FF // ff_chunk)], down

    pending = None
    for s in range(n_sub):
        ups, down = sub_tasks(s)
        ups[0]()
        ups[1]()
        if pending is not None:
            pending()
        for task in ups[2:]:
            task()
        pending = down
    pending()


def _const_spec(shape):
    nd = len(shape)
    return pl.BlockSpec(shape, lambda *_: (0,) * nd, pipeline_mode=pl.Buffered(1))


def _layer_spec(shape, layer):
    nd = len(shape)
    return pl.BlockSpec((1,) + tuple(shape[1:]), lambda *_: (layer,) + (0,) * (nd - 1),
                        pipeline_mode=pl.Buffered(1))


def _mlp_stream_kernel(x_ref, wup_ref, wdn_ref, lg_ref, lb_ref, y_ref, xb_scr, acc_scr):
    c = pl.program_id(0)

    @pl.when(c == 0)
    def _():
        xb_scr[...] = x_ref[...].astype(bf16)

    hcol = _dot(xb_scr[...], wup_ref[0])
    part = _dot(jnp.square(jnp.maximum(hcol, 0.0)).astype(bf16), wdn_ref[0])

    @pl.when(c == 0)
    def _():
        acc_scr[...] = part

    @pl.when(c > 0)
    def _():
        acc_scr[...] += part

    @pl.when(c == pl.num_programs(0) - 1)
    def _():
        y_ref[...] = _layer_norm(ALPHA * x_ref[...] + acc_scr[...], lg_ref[0], lb_ref[0])


def _mlp_stream(x2d, wup, wdn, lg, lb, layer, ff_chunk):
    m = x2d.shape[0]
    return pl.pallas_call(
        _mlp_stream_kernel,
        grid=(D_FF // ff_chunk,),
        in_specs=[pl.BlockSpec((m, D), lambda c: (0, 0)),
                  pl.BlockSpec((1, D, ff_chunk), lambda c: (layer, 0, c)),
                  pl.BlockSpec((1, ff_chunk, D), lambda c: (layer, c, 0)),
                  _layer_spec(lg.shape, layer), _layer_spec(lb.shape, layer)],
        out_specs=pl.BlockSpec((m, D), lambda c: (0, 0)),
        out_shape=jax.ShapeDtypeStruct((m, D), f32),
        scratch_shapes=[pltpu.VMEM((m, D), bf16), pltpu.VMEM((m, D), f32)],
        compiler_params=pltpu.CompilerParams(dimension_semantics=("arbitrary",), vmem_limit_bytes=VMEM_LIMIT),
        name="mlp_stream",
    )(x2d, wup, wdn, lg, lb)


def _sample_tail_kernel(x_ref, e1_ref, e2_ref, wup_ref, wdn_ref, win_ref, wc_ref, wout_ref, l2g_ref, l2b_ref,
                        l1g_ref, l1b_ref, y_ref, u_ref, xcur_scr, xb_scr, acc_scr, p_scr, *, n_ff, n_proj, t_seq):
    c = pl.program_id(0)
    c_conv = n_ff
    c_mlp1 = n_ff + n_proj + 1

    @pl.when(c == 0)
    def _():
        xcur_scr[...] = x_ref[...]

    def mlp_phase(c0, layer):
        @pl.when(c == c0)
        def _():
            xb_scr[...] = xcur_scr[...].astype(bf16)
            acc_scr[...] = jnp.zeros_like(acc_scr)

        @pl.when((c >= c0) & (c < c0 + n_ff))
        def _():
            hcol = _dot(xb_scr[...], wup_ref[0])
            acc_scr[...] += _dot(jnp.square(jnp.maximum(hcol, 0.0)).astype(bf16), wdn_ref[0])

        @pl.when(c == c0 + n_ff - 1)
        def _():
            xcur_scr[...] = _layer_norm(ALPHA * xcur_scr[...] + acc_scr[...], l2g_ref[layer], l2b_ref[layer])

    mlp_phase(0, 0)

    @pl.when(c == c_conv)
    def _():
        xb_scr[...] = xcur_scr[...].astype(bf16)

    @pl.when((c >= c_conv) & (c < c_conv + n_proj))
    def _():
        p_scr[c - c_conv] = _dot(xb_scr[...], win_ref[...])

    @pl.when(c == c_conv + n_proj)
    def _():
        u = p_scr[1] * p_scr[2]
        t = lax.broadcasted_iota(jnp.int32, u.shape, 0) % t_seq
        p1 = jnp.where(t >= 1, pltpu.roll(u, 1, 0), e1_ref[...])
        p2 = jnp.where(t >= 2, pltpu.roll(u, 2, 0), e2_ref[...])
        conv = p2 * wc_ref[0:1, :] + p1 * wc_ref[1:2, :] + u * wc_ref[2:3, :]
        y = _dot((p_scr[0] * conv).astype(bf16), wout_ref[...])
        xcur_scr[...] = _layer_norm(ALPHA * xcur_scr[...] + y, l1g_ref[...], l1b_ref[...])
        u_ref[...] = u

    mlp_phase(c_mlp1, 1)

    @pl.when(c == pl.num_programs(0) - 1)
    def _():
        y_ref[...] = xcur_scr[...]


def _sample_tail(x2d, e1, e2, wup, wdn, win, wc, wout, l2g, l2b, l1g, l1b, t_seq, ff_chunk):
    m = x2d.shape[0]
    n_ff = D_FF // ff_chunk
    n_proj = win.shape[1] // D
    c_mlp1 = n_ff + n_proj + 1

    def ff_idx(c):
        layer = jnp.where(c >= c_mlp1, 1, 0)
        chunk = jnp.where(c < n_ff, c, jnp.where(c < c_mlp1, n_ff - 1, c - c_mlp1))
        return layer, chunk

    return pl.pallas_call(
        functools.partial(_sample_tail_kernel, n_ff=n_ff, n_proj=n_proj, t_seq=t_seq),
        grid=(2 * n_ff + n_proj + 1,),
        in_specs=[_const_spec((m, D)), _const_spec((m, D)), _const_spec((m, D)),
                  pl.BlockSpec((1, D, ff_chunk), lambda c: (ff_idx(c)[0], 0, ff_idx(c)[1])),
                  pl.BlockSpec((1, ff_chunk, D), lambda c: (ff_idx(c)[0], ff_idx(c)[1], 0)),
                  pl.BlockSpec((D, D), lambda c: (0, jnp.clip(c - n_ff, 0, n_proj - 1))),
                  _const_spec(wc.shape), _const_spec(wout.shape),
                  _const_spec(l2g.shape), _const_spec(l2b.shape),
                  _const_spec(l1g.shape), _const_spec(l1b.shape)],
        out_specs=[pl.BlockSpec((m, D), lambda c: (0, 0)), pl.BlockSpec((m, D), lambda c: (0, 0))],
        out_shape=[jax.ShapeDtypeStruct((m, D), f32), jax.ShapeDtypeStruct((m, D), f32)],
        scratch_shapes=[pltpu.VMEM((m, D), f32), pltpu.VMEM((m, D), bf16), pltpu.VMEM((m, D), f32),
                        pltpu.VMEM((n_proj, m, D), f32)],
        compiler_params=pltpu.CompilerParams(dimension_semantics=("arbitrary",), vmem_limit_bytes=VMEM_LIMIT),
        name="sample_tail",
    )(x2d, e1, e2, wup, wdn, win, wc, wout, l2g, l2b, l1g, l1b)


def _mlp(x2d, wup, wdn, lg, lb, layer, n_sub, sub_rows):
    m = x2d.shape[0]
    tm = n_sub * sub_rows
    assert m % tm == 0
    return pl.pallas_call(
        functools.partial(_mlp_kernel, n_sub=n_sub, sub_rows=sub_rows, ff_chunk=1024),
        grid=(m // tm,),
        in_specs=[pl.BlockSpec((tm, D), lambda i: (i, 0)),
                  _layer_spec(wup.shape, layer), _layer_spec(wdn.shape, layer),
                  _layer_spec(lg.shape, layer), _layer_spec(lb.shape, layer)],
        out_specs=pl.BlockSpec((tm, D), lambda i: (i, 0)),
        out_shape=jax.ShapeDtypeStruct((m, D), f32),
        scratch_shapes=[pltpu.VMEM((2, sub_rows, D), bf16), pltpu.VMEM((2, sub_rows, D_FF), bf16)],
        compiler_params=pltpu.CompilerParams(dimension_semantics=("arbitrary",), vmem_limit_bytes=VMEM_LIMIT),
        name="mlp",
    )(x2d, wup, wdn, lg, lb)


def _gla_prompt(x, wq, wgl, wgu, bgate, ng, wo, lg, lb, tb, side):
    bsz, t, _ = x.shape
    assert tb % KCHUNK == 0 and t % (2 * tb) == 0
    n_chunks = tb // KCHUNK
    n_blocks = bsz * t // tb
    n_steps = n_blocks // 2
    steps_per_seq = t // (2 * tb)
    assert all(a.shape[0] % (16 * n_steps) == 0 for a in side)
    x2d = x.reshape(bsz * t, D)
    side_specs = [pl.BlockSpec((a.shape[0] // n_steps, a.shape[1]), lambda i: (i, 0)) for a in side]
    outs = pl.pallas_call(
        functools.partial(_gla_prompt_kernel, n_chunks=n_chunks, steps_per_seq=steps_per_seq, n_side=len(side)),
        grid=(n_steps,),
        in_specs=[pl.BlockSpec((tb, D), lambda i: (0, 0), pipeline_mode=pl.Buffered(1)),
                  pl.BlockSpec((tb, D), lambda i: (2 * i + 1, 0)),
                  pl.BlockSpec((tb, D), lambda i: (jnp.minimum(2 * i + 2, n_blocks - 1), 0)),
                  pl.BlockSpec((2 * tb, D), lambda i: (i, 0)),
                  _const_spec(wq.shape), _const_spec(wgl.shape), _const_spec(wgu.shape),
                  _const_spec(bgate.shape), _const_spec(ng.shape), _const_spec(wo.shape),
                  _const_spec(lg.shape), _const_spec(lb.shape)] + side_specs,
        out_specs=[pl.BlockSpec((2 * tb, D), lambda i: (i, 0)),
                   pl.BlockSpec((1, H, DKH, DVH), lambda i: (i // steps_per_seq, 0, 0, 0))] + side_specs,
        out_shape=[jax.ShapeDtypeStruct((bsz * t, D), f32),
                   jax.ShapeDtypeStruct((bsz, H, DKH, DVH), f32)]
                  + [jax.ShapeDtypeStruct(a.shape, bf16) for a in side],
        scratch_shapes=[pltpu.VMEM((2, tb, D), bf16),
                        pltpu.VMEM((2, tb, DK), f32), pltpu.VMEM((2, tb, DK), f32),
                        pltpu.VMEM((2, tb, DV), f32), pltpu.VMEM((2, tb, DV), f32),
                        pltpu.VMEM((2, tb, DK), f32),
                        pltpu.VMEM((tb, DK), f32), pltpu.VMEM((tb, DV), f32), pltpu.VMEM((tb, DV), bf16),
                        pltpu.VMEM((H, DKH, DVH), f32)],
        compiler_params=pltpu.CompilerParams(dimension_semantics=("arbitrary",),
                                             vmem_limit_bytes=VMEM_LIMIT),
        name="gla_prompt",
    )(x2d, x2d, x2d, x2d, wq, wgl, wgu, bgate, ng, wo, lg, lb, *side)
    return outs[0].reshape(bsz, t, D), outs[1], outs[2:]


def _gla_sample(x_pad, state, wq, wgl, wgu, bgate, ng, wo, lg, lb, n_seq, t_valid):
    s_total = state.shape[0]
    rows = n_seq * SUB
    assert s_total % n_seq == 0 and rows % 128 == 0
    return pl.pallas_call(
        functools.partial(_gla_sample_kernel, n_seq=n_seq, t_valid=t_valid),
        grid=(s_total // n_seq,),
        in_specs=[pl.BlockSpec((rows, D), lambda i: (i, 0)),
                  pl.BlockSpec((n_seq, H, DKH, DVH), lambda i: (i, 0, 0, 0)),
                  _const_spec(wq.shape), _const_spec(wgl.shape), _const_spec(wgu.shape),
                  _const_spec(bgate.shape), _const_spec(ng.shape), _const_spec(wo.shape),
                  _const_spec(lg.shape), _const_spec(lb.shape)],
        out_specs=[pl.BlockSpec((rows, D), lambda i: (i, 0)),
                   pl.BlockSpec((n_seq, H, DKH, DVH), lambda i: (i, 0, 0, 0))],
        out_shape=[jax.ShapeDtypeStruct((s_total * SUB, D), f32),
                   jax.ShapeDtypeStruct(state.shape, f32)],
        scratch_shapes=[pltpu.VMEM((rows, DV), f32)],
        compiler_params=pltpu.CompilerParams(dimension_semantics=("arbitrary",), vmem_limit_bytes=VMEM_LIMIT),
        name="gla_sample",
    )(x_pad, state, wq, wgl, wgu, bgate, ng, wo, lg, lb)


def _conv_prompt(x, win, wc, wout, lg, lb, tb):
    bsz, t, _ = x.shape
    assert t % (2 * tb) == 0
    n_blocks = bsz * t // tb
    steps_per_seq = t // (2 * tb)
    x2d = x.reshape(bsz * t, D)
    y, st = pl.pallas_call(
        functools.partial(_conv_prompt_kernel, tb=tb, steps_per_seq=steps_per_seq),
        grid=(n_blocks // 2,),
        in_specs=[pl.BlockSpec((tb, D), lambda i: (0, 0), pipeline_mode=pl.Buffered(1)),
                  pl.BlockSpec((tb, D), lambda i: (2 * i + 1, 0)),
                  pl.BlockSpec((tb, D), lambda i: (jnp.minimum(2 * i + 2, n_blocks - 1), 0)),
                  pl.BlockSpec((2 * tb, D), lambda i: (i, 0)),
                  _const_spec(win.shape), _const_spec(wc.shape), _const_spec(wout.shape),
                  _const_spec(lg.shape), _const_spec(lb.shape)],
        out_specs=[pl.BlockSpec((2 * tb, D), lambda i: (i, 0)),
                   pl.BlockSpec((1, 2, D), lambda i: (i // steps_per_seq, 0, 0))],
        out_shape=[jax.ShapeDtypeStruct((bsz * t, D), f32),
                   jax.ShapeDtypeStruct((bsz, 2, D), f32)],
        scratch_shapes=[pltpu.VMEM((2, tb, D), bf16), pltpu.VMEM((2, tb, D), f32),
                        pltpu.VMEM((tb, D), f32), pltpu.VMEM((2, tb, D), f32),
                        pltpu.VMEM((tb, D), bf16), pltpu.VMEM((SUB, D), f32)],
        compiler_params=pltpu.CompilerParams(dimension_semantics=("arbitrary",), vmem_limit_bytes=VMEM_LIMIT),
        name="conv_prompt",
    )(x2d, x2d, x2d, x2d, win, wc, wout, lg, lb)
    return y.reshape(bsz, t, D), st


def _conv_sample(x2d, e1, e2, win, wc, wout, lg, lb, t_seq):
    m = x2d.shape[0]
    n_proj = win.shape[1] // D
    full = lambda shape: pl.BlockSpec(shape, lambda i: (0,) * len(shape))
    return pl.pallas_call(
        functools.partial(_conv_sample_kernel, t_seq=t_seq),
        grid=(n_proj + 1,),
        in_specs=[full((m, D)), full((m, D)), full((m, D)),
                  pl.BlockSpec((D, D), lambda c: (0, jnp.minimum(c, n_proj - 1))),
                  _const_spec(wc.shape), _const_spec(wout.shape),
                  _const_spec(lg.shape), _const_spec(lb.shape)],
        out_specs=[full((m, D)), full((m, D))],
        out_shape=[jax.ShapeDtypeStruct((m, D), f32), jax.ShapeDtypeStruct((m, D), f32)],
        scratch_shapes=[pltpu.VMEM((m, D), bf16), pltpu.VMEM((n_proj, m, D), f32)],
        compiler_params=pltpu.CompilerParams(dimension_semantics=("arbitrary",), vmem_limit_bytes=VMEM_LIMIT),
        name="conv_sample",
    )(x2d, e1, e2, win, wc, wout, lg, lb)


def kernel(x_prompt, x_sample, state_gla, state_conv, gla_w_in, gla_w_gate_up, gla_b_gate, gla_norm_g, gla_w_o, conv_w_in, conv_w_conv, conv_w_out, mlp_w_up, mlp_w_down, ln1_g, ln1_b, ln2_g, ln2_b):
    bsz, t, _ = x_prompt.shape
    n_dec, t_dec, _ = x_sample.shape
    assert t % CHUNK == 0 and t_dec <= SUB and t_dec >= 2

    w_in = gla_w_in[0]
    wq = w_in.astype(bf16)
    wgl = jnp.pad(w_in[:, 2 * DK + 2 * DV:], ((0, 0), (0, RANK_PAD - RANK))).astype(bf16)
    wgu = jnp.pad(gla_w_gate_up[0], ((0, RANK_PAD - RANK), (0, 0))).astype(bf16)
    bgate = gla_b_gate[0].reshape(1, DK)
    ng = gla_norm_g[0].reshape(1, DV)
    wo = gla_w_o[0].astype(bf16)
    cwc = conv_w_conv[0]
    ln2g = ln2_g.reshape(DEPTH, 1, D)
    ln2b = ln2_b.reshape(DEPTH, 1, D)
    row = lambda a, i: a[i].reshape(1, D)
    mlp_prompt = functools.partial(_mlp, n_sub=4, sub_rows=256)
    mlp_sample = functools.partial(_mlp_stream, ff_chunk=1024)

    side = (mlp_w_up.reshape(DEPTH * D, D_FF), mlp_w_down.reshape(DEPTH * D_FF, D), conv_w_in[0], conv_w_out[0])
    xp, gla_p, (wup, wdn, cwin, cwout) = _gla_prompt(x_prompt, wq, wgl, wgu, bgate, ng, wo, row(ln1_g, 0),
                                                     row(ln1_b, 0), tb=256, side=side)
    wup = wup.reshape(DEPTH, D, D_FF)
    wdn = wdn.reshape(DEPTH, D_FF, D)
    xs_pad = jnp.pad(x_sample, ((0, 0), (0, SUB - t_dec), (0, 0))).reshape(n_dec * SUB, D)
    xs_pad, gla_s = _gla_sample(xs_pad, state_gla[0], wq, wgl, wgu, bgate, ng, wo, row(ln1_g, 0), row(ln1_b, 0),
                                n_seq=16, t_valid=t_dec)
    xs = xs_pad.reshape(n_dec, SUB, D)[:, :t_dec].reshape(n_dec * t_dec, D)
    xp = mlp_prompt(xp.reshape(bsz * t, D), wup, wdn, ln2g, ln2b, layer=0)

    xp, conv_p = _conv_prompt(xp.reshape(bsz, t, D), cwin, cwc, cwout, row(ln1_g, 1), row(ln1_b, 1), tb=512)
    buf = state_conv[0]
    e1 = jnp.pad(buf[:, 1:2], ((0, 0), (0, t_dec - 1), (0, 0))).reshape(n_dec * t_dec, D)
    e2 = jnp.pad(buf, ((0, 0), (0, t_dec - 2), (0, 0))).reshape(n_dec * t_dec, D)
    xs, u_s = _sample_tail(xs, e1, e2, wup, wdn, cwin, cwc, cwout, ln2g, ln2b, row(ln1_g, 1), row(ln1_b, 1),
                           t_seq=t_dec, ff_chunk=1024)
    conv_s = u_s.reshape(n_dec, t_dec, D)[:, t_dec - 2:]
    xp = mlp_prompt(xp.reshape(bsz * t, D), wup, wdn, ln2g, ln2b, layer=1)

    return (xp.reshape(bsz, t, D), xs.reshape(n_dec, t_dec, D), gla_p[None], gla_s[None],
            conv_p[None], conv_s[None])
```

```python
import functools

import jax
import jax.numpy as jnp
from jax import lax
from jax.experimental import pallas as pl
from jax.experimental.pallas import tpu as pltpu

bf16 = jnp.bfloat16
f32 = jnp.float32

D = 1024
H = 4
DK = 512
DV = 1024
DKH = DK // H
DVH = DV // H
RANK = 16
RANK_PAD = 128
TAU = 16.0
CHUNK = 64
KCHUNK = 128
D_FF = 4 * D
DEPTH = 2
ALPHA = (2 * DEPTH) ** 0.25
LN_EPS = 1e-5
RMS_EPS = 1e-6
Q_SCALE = DKH ** -0.5

SUB = 8
VMEM_LIMIT = 56 * 1024 * 1024

_NT = (((1,), (1,)), ((), ()))
_TN = (((0,), (0,)), ((), ()))


def _dot(a, b):
    return jnp.dot(a, b, preferred_element_type=f32)


def _dg(a, b, dims):
    return lax.dot_general(a, b, dims, preferred_element_type=f32)


def _layer_norm(y, g, b):
    mu = jnp.mean(y, axis=-1, keepdims=True)
    yc = y - mu
    var = jnp.mean(yc * yc, axis=-1, keepdims=True)
    return yc * lax.rsqrt(var + LN_EPS) * g + b


def _log_sigmoid(z):
    return -(jnp.maximum(-z, 0.0) + jnp.log(1.0 + jnp.exp(-jnp.abs(z))))


def _gla_project(xb, wq_ref, wgl_ref, wgu_ref, bgate_ref):
    q = _dot(xb, wq_ref[:, 0:DK]) * Q_SCALE
    k = _dot(xb, wq_ref[:, DK:2 * DK])
    v = _dot(xb, wq_ref[:, 2 * DK:2 * DK + DV])
    r = _dot(xb, wq_ref[:, 2 * DK + DV:2 * DK + 2 * DV])
    gl = _dot(xb, wgl_ref[...])
    z = _dot(gl.astype(bf16), wgu_ref[...]) + bgate_ref[...]
    g = _log_sigmoid(z) / TAU
    return q, k, v, r, g


def _gla_post(o, r, x, ng, wo_ref, lg, lb):
    parts = []
    for h in range(H):
        vs = slice(h * DVH, (h + 1) * DVH)
        oh = o[:, vs]
        ms = jnp.mean(oh * oh, axis=-1, keepdims=True)
        parts.append(oh * lax.rsqrt(ms + RMS_EPS) * ng[:, vs])
    on = jnp.concatenate(parts, axis=1)
    gated = on * (r * jax.nn.sigmoid(r))
    y = _dot(gated.astype(bf16), wo_ref[...])
    return _layer_norm(ALPHA * x + y, lg, lb)


PROJ_TILE = 512


def _interleave(main_tasks, filler_tasks):
    n_main, n_fill = len(main_tasks), len(filler_tasks)
    done = 0
    for idx, task in enumerate(main_tasks):
        task()
        want = ((idx + 1) * n_fill) // n_main
        while done < want:
            filler_tasks[done]()
            done += 1


def _gla_stage1_tasks(x_ref, slot, w_refs, scr):
    wq_ref, wgl_ref, wgu_ref, bgate_ref = w_refs
    xb_scr, q_scr, k_scr, v_scr, r_scr, g_scr = scr[:6]

    def cast_x():
        xb_scr[slot] = x_ref[...].astype(bf16)

    def proj(dst, c0, w0, scale):
        def run():
            acc = _dot(xb_scr[slot], wq_ref[:, w0:w0 + PROJ_TILE])
            dst[slot, :, c0:c0 + PROJ_TILE] = acc * scale if scale is not None else acc
        return run

    def gate():
        gl = _dot(xb_scr[slot], wgl_ref[...])
        z = _dot(gl.astype(bf16), wgu_ref[...]) + bgate_ref[...]
        g_scr[slot] = _log_sigmoid(z) / TAU

    tiles = []
    for dst, w_base, width, scale in ((q_scr, 0, DK, Q_SCALE), (k_scr, DK, DK, None),
                                      (v_scr, 2 * DK, DV, None), (r_scr, 2 * DK + DV, DV, None)):
        for c0 in range(0, width, PROJ_TILE):
            tiles.append(proj(dst, c0, w_base + c0, scale))
    return cast_x, gate, tiles


def _split3(a):
    a1 = a.astype(bf16)
    r1 = a - a1.astype(f32)
    a2 = r1.astype(bf16)
    a3 = (r1 - a2.astype(f32)).astype(bf16)
    return a1, a2, a3


def _gla_stage2_tasks(xres_ref, y_ref, r0, slot, scr, s_scr, ng_ref, wo_ref, lg_ref, lb_ref, n_chunks):
    _, q_scr, k_scr, v_scr, r_scr, g_scr, b_scr, o_scr, gated_scr = scr
    tb = n_chunks * KCHUNK
    mid = KCHUNK // 2

    def cumsum():
        row = lax.broadcasted_iota(jnp.int32, (tb, tb), 0)
        col = lax.broadcasted_iota(jnp.int32, (tb, tb), 1)
        tri = ((row >= col) & ((row // KCHUNK) == (col // KCHUNK))).astype(bf16)
        g1, g2, g3 = _split3(g_scr[slot])
        b_scr[...] = _dot(tri, g1) + _dot(tri, g2) + _dot(tri, g3)

    def chunk_prep(c, cell):
        rs = slice(c * KCHUNK, (c + 1) * KCHUNK)

        def run():
            b = b_scr[rs, :]
            b_mid = b[mid - 1:mid, :]
            b_last = b[KCHUNK - 1:KCHUNK, :]
            qc = q_scr[slot, rs, :]
            kc = k_scr[slot, rs, :]
            cell["qs"] = (qc * jnp.exp(b - b_mid)).astype(bf16)
            cell["ks"] = (kc * jnp.exp(b_mid - b)).astype(bf16)
            cell["qd"] = (qc * jnp.exp(b)).astype(bf16)
            cell["kk"] = (kc * jnp.exp(b_last - b)).astype(bf16)
            cell["vb"] = v_scr[slot, rs, :].astype(bf16)
            cell["dec_t"] = jnp.exp(jnp.broadcast_to(b_last, (KCHUNK, DK)).T)
        return run

    def head_scores(h, cell):
        ks = slice(h * DKH, (h + 1) * DKH)

        def run():
            ri = lax.broadcasted_iota(jnp.int32, (KCHUNK, KCHUNK), 0)
            ci = lax.broadcasted_iota(jnp.int32, (KCHUNK, KCHUNK), 1)
            sc = _dg(cell["qs"][:, ks], cell["ks"][:, ks], _NT)
            cell["sc", h] = jnp.where(ri >= ci, sc, 0.0).astype(bf16)
        return run

    def head_update(c, h, cell):
        rs = slice(c * KCHUNK, (c + 1) * KCHUNK)
        ks = slice(h * DKH, (h + 1) * DKH)
        vs = slice(h * DVH, (h + 1) * DVH)

        def run():
            st = s_scr[h]
            vh = cell["vb"][:, vs]
            lhs = jnp.concatenate([cell["sc", h], cell["qd"][:, ks]], axis=1)
            rhs = jnp.concatenate([vh, st.astype(bf16)], axis=0)
            o_scr[rs, vs] = _dot(lhs, rhs)
            dec_h = cell["dec_t"][ks, :]
            dec_m = jnp.concatenate([dec_h] * (DVH // KCHUNK), axis=1)
            s_scr[h] = dec_m * st + _dg(cell["kk"][:, ks], vh, _TN)
        return run

    def norm_gate(h):
        vs = slice(h * DVH, (h + 1) * DVH)

        def run():
            oh = o_scr[:, vs]
            rh = r_scr[slot, :, vs]
            ms = jnp.mean(oh * oh, axis=-1, keepdims=True)
            on = oh * lax.rsqrt(ms + RMS_EPS) * ng_ref[:, vs]
            gated_scr[:, vs] = (on * (rh * jax.nn.sigmoid(rh))).astype(bf16)
        return run

    def out_rows(m0, m1):
        def run():
            y = _dot(gated_scr[m0:m1, :], wo_ref[...])
            x = xres_ref[r0 + m0:r0 + m1, :]
            y_ref[r0 + m0:r0 + m1, :] = _layer_norm(ALPHA * x + y, lg_ref[...], lb_ref[...])
        return run

    tasks = [cumsum]
    for c in range(n_chunks):
        cell = {}
        tasks.append(chunk_prep(c, cell))
        tasks += [head_scores(h, cell) for h in range(H)]
        tasks += [head_update(c, h, cell) for h in range(H)]
    tasks += [norm_gate(h) for h in range(H)]
    half = tb // 2
    tasks += [out_rows(0, half), out_rows(half, tb)]
    return tasks


def _gla_prompt_kernel(x0_ref, xa_ref, xb_ref, xres_ref, wq_ref, wgl_ref, wgu_ref, bgate_ref, ng_ref, wo_ref,
                       lg_ref, lb_ref, *refs, n_chunks, steps_per_seq, n_side):
    side_in = refs[:n_side]
    y_ref, st_ref = refs[n_side:n_side + 2]
    side_out = refs[n_side + 2:2 * n_side + 2]
    (xb_scr, q_scr, k_scr, v_scr, r_scr, g_scr, b_scr, o_scr, gated_scr, s_scr) = refs[2 * n_side + 2:]
    i = pl.program_id(0)
    tb = n_chunks * KCHUNK
    w_refs = (wq_ref, wgl_ref, wgu_ref, bgate_ref)
    scr = (xb_scr, q_scr, k_scr, v_scr, r_scr, g_scr, b_scr, o_scr, gated_scr)
    post = (ng_ref, wo_ref, lg_ref, lb_ref)

    @pl.when(i == 0)
    def _():
        cast_x, gate, tiles = _gla_stage1_tasks(x0_ref, 0, w_refs, scr)
        for task in [cast_x, gate] + tiles:
            task()

    @pl.when(i % steps_per_seq == 0)
    def _():
        s_scr[...] = jnp.zeros_like(s_scr)

    def side_cast(src_ref, dst_ref):
        def run():
            dst_ref[...] = src_ref[...].astype(bf16)
        return run

    side_tasks = [side_cast(s, d) for s, d in zip(side_in, side_out)]
    for r0, slot, x_next in ((0, 0, xa_ref), (tb, 1, xb_ref)):
        cast_x, gate, tiles = _gla_stage1_tasks(x_next, 1 - slot, w_refs, scr)
        stage2 = _gla_stage2_tasks(xres_ref, y_ref, r0, slot, scr, s_scr, *post, n_chunks)
        cast_x()
        fillers = tiles + side_tasks[slot::2]
        _interleave([stage2[0], gate] + stage2[1:], fillers)

    @pl.when(i % steps_per_seq == steps_per_seq - 1)
    def _():
        for h in range(H):
            st_ref[0, h] = s_scr[h]


def _gla_sample_kernel(x_ref, st_ref, wq_ref, wgl_ref, wgu_ref, bgate_ref, ng_ref, wo_ref, lg_ref, lb_ref,
                       y_ref, sto_ref, o_scr, *, n_seq, t_valid):
    rows = n_seq * SUB
    x = x_ref[...]
    q, k, v, r, g = _gla_project(x.astype(bf16), wq_ref, wgl_ref, wgu_ref, bgate_ref)

    row = lax.broadcasted_iota(jnp.int32, (rows, rows), 0)
    col = lax.broadcasted_iota(jnp.int32, (rows, rows), 1)
    same_seq = (row // SUB) == (col // SUB)
    tri = (same_seq & (row >= col)).astype(f32)
    tri_last = (same_seq & ((col % SUB) < t_valid)).astype(f32)
    b = jnp.dot(tri, g, precision=lax.Precision.HIGHEST, preferred_element_type=f32)
    b_last = jnp.dot(tri_last, g, precision=lax.Precision.HIGHEST, preferred_element_type=f32)

    valid = (lax.broadcasted_iota(jnp.int32, (rows, DK), 0) % SUB) < t_valid
    qd = (q * jnp.exp(b)).astype(bf16)
    kd = jnp.where(valid, k * jnp.exp(-b), 0.0).astype(bf16)
    kk_t = jnp.where(valid, k * jnp.exp(b_last - b), 0.0).T.astype(bf16)
    vb = v.astype(bf16)
    dec_t = jnp.exp(b_last.T)

    intra_mask = same_seq & (row >= col)
    for h in range(H):
        ks = slice(h * DKH, (h + 1) * DKH)
        vs = slice(h * DVH, (h + 1) * DVH)
        sc = jnp.where(intra_mask, _dg(qd[:, ks], kd[:, ks], _NT), 0.0).astype(bf16)
        o_scr[:, vs] = _dot(sc, vb[:, vs])

    seq_of_col = lax.broadcasted_iota(jnp.int32, (DKH, rows), 1) // SUB

    def seq_dots(n):
        rs = slice(n * SUB, (n + 1) * SUB)
        res = []
        for h in range(H):
            ks = slice(h * DKH, (h + 1) * DKH)
            vs = slice(h * DVH, (h + 1) * DVH)
            s0 = st_ref[n, h]
            o_inter = _dot(qd[rs, ks], s0.astype(bf16))
            upd = _dot(jnp.where(seq_of_col == n, kk_t[ks, :], jnp.zeros((), bf16)), vb[:, vs])
            res.append((s0, o_inter, upd))
        return res

    def seq_combine(n, res):
        rs = slice(n * SUB, (n + 1) * SUB)
        for h, (s0, o_inter, upd) in enumerate(res):
            ks = slice(h * DKH, (h + 1) * DKH)
            vs = slice(h * DVH, (h + 1) * DVH)
            o_scr[rs, vs] = o_scr[rs, vs] + o_inter
            sto_ref[n, h] = dec_t[ks, n * SUB:n * SUB + 1] * s0 + upd

    pending = None
    for n in range(n_seq):
        res = seq_dots(n)
        if pending is not None:
            seq_combine(*pending)
        pending = (n, res)
    seq_combine(*pending)

    y_ref[...] = _gla_post(o_scr[...], r, x, ng_ref[...], wo_ref, lg_ref[...], lb_ref[...])


def _conv_prompt_kernel(x0_ref, xa_ref, xb_ref, xres_ref, win_ref, wc_ref, wout_ref, lg_ref, lb_ref,
                        y_ref, st_ref, xb_scr, bg_scr, cg_scr, u_scr, gated_scr, c_scr,
                        *, tb, steps_per_seq):
    i = pl.program_id(0)
    half = tb // 2
    col_tile = D // 2

    def stage1_tasks(x_ref, slot):
        def cast():
            xb_scr[slot] = x_ref[...].astype(bf16)

        def proj_bg(c0):
            def run():
                bg_scr[slot, :, c0:c0 + col_tile] = _dot(xb_scr[slot], win_ref[:, c0:c0 + col_tile])
            return run

        def proj_cg(c0):
            def run():
                cg_scr[:, c0:c0 + col_tile] = _dot(xb_scr[slot], win_ref[:, D + c0:D + c0 + col_tile])
            return run

        def proj_u(c0):
            def run():
                hh = _dot(xb_scr[slot], win_ref[:, 2 * D + c0:2 * D + c0 + col_tile])
                u_scr[slot, :, c0:c0 + col_tile] = cg_scr[:, c0:c0 + col_tile] * hh
            return run

        cols = range(0, D, col_tile)
        return cast, ([proj_bg(c) for c in cols] + [proj_cg(c) for c in cols] + [proj_u(c) for c in cols])

    def stage2_tasks(r0, slot):
        def conv(m0):
            def run():
                u = u_scr[slot, m0:m0 + half, :]
                t = lax.broadcasted_iota(jnp.int32, (half, D), 0)
                c0 = c_scr[0:1, :]
                c1 = c_scr[1:2, :]
                p1 = jnp.where(t == 0, c1, pltpu.roll(u, 1, 0))
                p2 = jnp.where(t == 0, c0, jnp.where(t == 1, c1, pltpu.roll(u, 2, 0)))
                cv = p2 * wc_ref[0:1, :] + p1 * wc_ref[1:2, :] + u * wc_ref[2:3, :]
                gated_scr[m0:m0 + half, :] = (bg_scr[slot, m0:m0 + half, :] * cv).astype(bf16)
                c_scr[0:2, :] = u[half - 2:half, :]
            return run

        def out(m0):
            def run():
                y = _dot(gated_scr[m0:m0 + half, :], wout_ref[...])
                x = xres_ref[r0 + m0:r0 + m0 + half, :]
                y_ref[r0 + m0:r0 + m0 + half, :] = _layer_norm(ALPHA * x + y, lg_ref[...], lb_ref[...])
            return run

        return [conv(0), conv(half), out(0), out(half)]

    @pl.when(i == 0)
    def _():
        cast, tiles = stage1_tasks(x0_ref, 0)
        cast()
        for task in tiles:
            task()

    @pl.when(i % steps_per_seq == 0)
    def _():
        c_scr[...] = jnp.zeros_like(c_scr)

    for r0, slot, x_next in ((0, 0, xa_ref), (tb, 1, xb_ref)):
        cast, tiles = stage1_tasks(x_next, 1 - slot)
        cast()
        _interleave(stage2_tasks(r0, slot), tiles)

    st_ref[0] = c_scr[0:2, :]


def _conv_sample_kernel(x_ref, e1_ref, e2_ref, win_ref, wc_ref, wout_ref, lg_ref, lb_ref, y_ref, u_ref,
                        xb_scr, p_scr, *, t_seq):
    c = pl.program_id(0)
    n_proj = p_scr.shape[0]

    @pl.when(c == 0)
    def _():
        xb_scr[...] = x_ref[...].astype(bf16)

    @pl.when(c < n_proj)
    def _():
        p_scr[c] = _dot(xb_scr[...], win_ref[...])

    @pl.when(c == n_proj)
    def _():
        x = x_ref[...]
        u = p_scr[1] * p_scr[2]
        t = lax.broadcasted_iota(jnp.int32, u.shape, 0) % t_seq
        p1 = jnp.where(t >= 1, pltpu.roll(u, 1, 0), e1_ref[...])
        p2 = jnp.where(t >= 2, pltpu.roll(u, 2, 0), e2_ref[...])
        conv = p2 * wc_ref[0:1, :] + p1 * wc_ref[1:2, :] + u * wc_ref[2:3, :]
        y = _dot((p_scr[0] * conv).astype(bf16), wout_ref[...])
        y_ref[...] = _layer_norm(ALPHA * x + y, lg_ref[...], lb_ref[...])
        u_ref[...] = u


def _mlp_kernel(x_ref, wup_ref, wdn_ref, lg_ref, lb_ref, y_ref, xb_scr, h_scr, *, n_sub, sub_rows, ff_chunk):
    def sub_tasks(s):
        rs = slice(s * sub_rows, (s + 1) * sub_rows)
        buf = s % 2

        def cast():
            xb_scr[buf] = x_ref[rs, :].astype(bf16)

        def up(c):
            cs = slice(c * ff_chunk, (c + 1) * ff_chunk)

            def run():
                hcol = _dot(xb_scr[buf], wup_ref[0, :, cs])
                h_scr[buf, :, cs] = jnp.square(jnp.maximum(hcol, 0.0)).astype(bf16)
            return run

        def down():
            y = _dot(h_scr[buf], wdn_ref[0])
            y_ref[rs, :] = _layer_norm(ALPHA * x_ref[rs, :] + y, lg_ref[0], lb_ref[0])

        return [cast] + [up(c) for c in range(D_FF // ff_chunk)], down

    pending = None
    for s in range(n_sub):
        ups, down = sub_tasks(s)
        ups[0]()
        ups[1]()
        if pending is not None:
            pending()
        for task in ups[2:]:
            task()
        pending = down
    pending()


def _const_spec(shape):
    nd = len(shape)
    return pl.BlockSpec(shape, lambda *_: (0,) * nd, pipeline_mode=pl.Buffered(1))


def _layer_spec(shape, layer):
    nd = len(shape)
    return pl.BlockSpec((1,) + tuple(shape[1:]), lambda *_: (layer,) + (0,) * (nd - 1),
                        pipeline_mode=pl.Buffered(1))


def _mlp_stream_kernel(x_ref, wup_ref, wdn_ref, lg_ref, lb_ref, y_ref, xb_scr, acc_scr):
    c = pl.program_id(0)

    @pl.when(c == 0)
    def _():
        xb_scr[...] = x_ref[...].astype(bf16)

    hcol = _dot(xb_scr[...], wup_ref[0])
    part = _dot(jnp.square(jnp.maximum(hcol, 0.0)).astype(bf16), wdn_ref[0])

    @pl.when(c == 0)
    def _():
        acc_scr[...] = part

    @pl.when(c > 0)
    def _():
        acc_scr[...] += part

    @pl.when(c == pl.num_programs(0) - 1)
    def _():
        y_ref[...] = _layer_norm(ALPHA * x_ref[...] + acc_scr[...], lg_ref[0], lb_ref[0])


def _mlp_stream(x2d, wup, wdn, lg, lb, layer, ff_chunk):
    m = x2d.shape[0]
    return pl.pallas_call(
        _mlp_stream_kernel,
        grid=(D_FF // ff_chunk,),
        in_specs=[pl.BlockSpec((m, D), lambda c: (0, 0)),
                  pl.BlockSpec((1, D, ff_chunk), lambda c: (layer, 0, c)),
                  pl.BlockSpec((1, ff_chunk, D), lambda c: (layer, c, 0)),
                  _layer_spec(lg.shape, layer), _layer_spec(lb.shape, layer)],
        out_specs=pl.BlockSpec((m, D), lambda c: (0, 0)),
        out_shape=jax.ShapeDtypeStruct((m, D), f32),
        scratch_shapes=[pltpu.VMEM((m, D), bf16), pltpu.VMEM((m, D), f32)],
        compiler_params=pltpu.CompilerParams(dimension_semantics=("arbitrary",), vmem_limit_bytes=VMEM_LIMIT),
        name="mlp_stream",
    )(x2d, wup, wdn, lg, lb)


def _sample_tail_kernel(x_ref, e1_ref, e2_ref, wup_ref, wdn_ref, win_ref, wc_ref, wout_ref, l2g_ref, l2b_ref,
                        l1g_ref, l1b_ref, y_ref, u_ref, xcur_scr, xb_scr, acc_scr, p_scr, *, n_ff, n_proj, t_seq):
    c = pl.program_id(0)
    c_conv = n_ff
    c_mlp1 = n_ff + n_proj + 1

    @pl.when(c == 0)
    def _():
        xcur_scr[...] = x_ref[...]

    def mlp_phase(c0, layer):
        @pl.when(c == c0)
        def _():
            xb_scr[...] = xcur_scr[...].astype(bf16)
            acc_scr[...] = jnp.zeros_like(acc_scr)

        @pl.when((c >= c0) & (c < c0 + n_ff))
        def _():
            hcol = _dot(xb_scr[...], wup_ref[0])
            acc_scr[...] += _dot(jnp.square(jnp.maximum(hcol, 0.0)).astype(bf16), wdn_ref[0])

        @pl.when(c == c0 + n_ff - 1)
        def _():
            xcur_scr[...] = _layer_norm(ALPHA * xcur_scr[...] + acc_scr[...], l2g_ref[layer], l2b_ref[layer])

    mlp_phase(0, 0)

    @pl.when(c == c_conv)
    def _():
        xb_scr[...] = xcur_scr[...].astype(bf16)

    @pl.when((c >= c_conv) & (c < c_conv + n_proj))
    def _():
        p_scr[c - c_conv] = _dot(xb_scr[...], win_ref[...])

    @pl.when(c == c_conv + n_proj)
    def _():
        u = p_scr[1] * p_scr[2]
        t = lax.broadcasted_iota(jnp.int32, u.shape, 0) % t_seq
        p1 = jnp.where(t >= 1, pltpu.roll(u, 1, 0), e1_ref[...])
        p2 = jnp.where(t >= 2, pltpu.roll(u, 2, 0), e2_ref[...])
        conv = p2 * wc_ref[0:1, :] + p1 * wc_ref[1:2, :] + u * wc_ref[2:3, :]
        y = _dot((p_scr[0] * conv).astype(bf16), wout_ref[...])
        xcur_scr[...] = _layer_norm(ALPHA * xcur_scr[...] + y, l1g_ref[...], l1b_ref[...])
        u_ref[...] = u

    mlp_phase(c_mlp1, 1)

    @pl.when(c == pl.num_programs(0) - 1)
    def _():
        y_ref[...] = xcur_scr[...]


def _sample_tail(x2d, e1, e2, wup, wdn, win, wc, wout, l2g, l2b, l1g, l1b, t_seq, ff_chunk):
    m = x2d.shape[0]
    n_ff = D_FF // ff_chunk
    n_proj = win.shape[1] // D
    c_mlp1 = n_ff + n_proj + 1

    def ff_idx(c):
        layer = jnp.where(c >= c_mlp1, 1, 0)
        chunk = jnp.where(c < n_ff, c, jnp.where(c < c_mlp1, n_ff - 1, c - c_mlp1))
        return layer, chunk

    return pl.pallas_call(
        functools.partial(_sample_tail_kernel, n_ff=n_ff, n_proj=n_proj, t_seq=t_seq),
        grid=(2 * n_ff + n_proj + 1,),
        in_specs=[_const_spec((m, D)), _const_spec((m, D)), _const_spec((m, D)),
                  pl.BlockSpec((1, D, ff_chunk), lambda c: (ff_idx(c)[0], 0, ff_idx(c)[1])),
                  pl.BlockSpec((1, ff_chunk, D), lambda c: (ff_idx(c)[0], ff_idx(c)[1], 0)),
                  pl.BlockSpec((D, D), lambda c: (0, jnp.clip(c - n_ff, 0, n_proj - 1))),
                  _const_spec(wc.shape), _const_spec(wout.shape),
                  _const_spec(l2g.shape), _const_spec(l2b.shape),
                  _const_spec(l1g.shape), _const_spec(l1b.shape)],
        out_specs=[pl.BlockSpec((m, D), lambda c: (0, 0)), pl.BlockSpec((m, D), lambda c: (0, 0))],
        out_shape=[jax.ShapeDtypeStruct((m, D), f32), jax.ShapeDtypeStruct((m, D), f32)],
        scratch_shapes=[pltpu.VMEM((m, D), f32), pltpu.VMEM((m, D), bf16), pltpu.VMEM((m, D), f32),
                        pltpu.VMEM((n_proj, m, D), f32)],
        compiler_params=pltpu.CompilerParams(dimension_semantics=("arbitrary",), vmem_limit_bytes=VMEM_LIMIT),
        name="sample_tail",
    )(x2d, e1, e2, wup, wdn, win, wc, wout, l2g, l2b, l1g, l1b)


def _mlp(x2d, wup, wdn, lg, lb, layer, n_sub, sub_rows):
    m = x2d.shape[0]
    tm = n_sub * sub_rows
    assert m % tm == 0
    return pl.pallas_call(
        functools.partial(_mlp_kernel, n_sub=n_sub, sub_rows=sub_rows, ff_chunk=1024),
        grid=(m // tm,),
        in_specs=[pl.BlockSpec((tm, D), lambda i: (i, 0)),
                  _layer_spec(wup.shape, layer), _layer_spec(wdn.shape, layer),
                  _layer_spec(lg.shape, layer), _layer_spec(lb.shape, layer)],
        out_specs=pl.BlockSpec((tm, D), lambda i: (i, 0)),
        out_shape=jax.ShapeDtypeStruct((m, D), f32),
        scratch_shapes=[pltpu.VMEM((2, sub_rows, D), bf16), pltpu.VMEM((2, sub_rows, D_FF), bf16)],
        compiler_params=pltpu.CompilerParams(dimension_semantics=("arbitrary",), vmem_limit_bytes=VMEM_LIMIT),
        name="mlp",
    )(x2d, wup, wdn, lg, lb)


def _gla_prompt(x, wq, wgl, wgu, bgate, ng, wo, lg, lb, tb, side):
    bsz, t, _ = x.shape
    assert tb % KCHUNK == 0 and t % (2 * tb) == 0
    n_chunks = tb // KCHUNK
    n_blocks = bsz * t // tb
    n_steps = n_blocks // 2
    steps_per_seq = t // (2 * tb)
    assert all(a.shape[0] % (16 * n_steps) == 0 for a in side)
    x2d = x.reshape(bsz * t, D)
    side_specs = [pl.BlockSpec((a.shape[0] // n_steps, a.shape[1]), lambda i: (i, 0)) for a in side]
    outs = pl.pallas_call(
        functools.partial(_gla_prompt_kernel, n_chunks=n_chunks, steps_per_seq=steps_per_seq, n_side=len(side)),
        grid=(n_steps,),
        in_specs=[pl.BlockSpec((tb, D), lambda i: (0, 0), pipeline_mode=pl.Buffered(1)),
                  pl.BlockSpec((tb, D), lambda i: (2 * i + 1, 0)),
                  pl.BlockSpec((tb, D), lambda i: (jnp.minimum(2 * i + 2, n_blocks - 1), 0)),
                  pl.BlockSpec((2 * tb, D), lambda i: (i, 0)),
                  _const_spec(wq.shape), _const_spec(wgl.shape), _const_spec(wgu.shape),
                  _const_spec(bgate.shape), _const_spec(ng.shape), _const_spec(wo.shape),
                  _const_spec(lg.shape), _const_spec(lb.shape)] + side_specs,
        out_specs=[pl.BlockSpec((2 * tb, D), lambda i: (i, 0)),
                   pl.BlockSpec((1, H, DKH, DVH), lambda i: (i // steps_per_seq, 0, 0, 0))] + side_specs,
        out_shape=[jax.ShapeDtypeStruct((bsz * t, D), f32),
                   jax.ShapeDtypeStruct((bsz, H, DKH, DVH), f32)]
                  + [jax.ShapeDtypeStruct(a.shape, bf16) for a in side],
        scratch_shapes=[pltpu.VMEM((2, tb, D), bf16),
                        pltpu.VMEM((2, tb, DK), f32), pltpu.VMEM((2, tb, DK), f32),
                        pltpu.VMEM((2, tb, DV), f32), pltpu.VMEM((2, tb, DV), f32),
                        pltpu.VMEM((2, tb, DK), f32),
                        pltpu.VMEM((tb, DK), f32), pltpu.VMEM((tb, DV), f32), pltpu.VMEM((tb, DV), bf16),
                        pltpu.VMEM((H, DKH, DVH), f32)],
        compiler_params=pltpu.CompilerParams(dimension_semantics=("arbitrary",),
                                             vmem_limit_bytes=VMEM_LIMIT),
        name="gla_prompt",
    )(x2d, x2d, x2d, x2d, wq, wgl, wgu, bgate, ng, wo, lg, lb, *side)
    return outs[0].reshape(bsz, t, D), outs[1], outs[2:]


def _gla_sample(x_pad, state, wq, wgl, wgu, bgate, ng, wo, lg, lb, n_seq, t_valid):
    s_total = state.shape[0]
    rows = n_seq * SUB
    assert s_total % n_seq == 0 and rows % 128 == 0
    return pl.pallas_call(
        functools.partial(_gla_sample_kernel, n_seq=n_seq, t_valid=t_valid),
        grid=(s_total // n_seq,),
        in_specs=[pl.BlockSpec((rows, D), lambda i: (i, 0)),
                  pl.BlockSpec((n_seq, H, DKH, DVH), lambda i: (i, 0, 0, 0)),
                  _const_spec(wq.shape), _const_spec(wgl.shape), _const_spec(wgu.shape),
                  _const_spec(bgate.shape), _const_spec(ng.shape), _const_spec(wo.shape),
                  _const_spec(lg.shape), _const_spec(lb.shape)],
        out_specs=[pl.BlockSpec((rows, D), lambda i: (i, 0)),
                   pl.BlockSpec((n_seq, H, DKH, DVH), lambda i: (i, 0, 0, 0))],
        out_shape=[jax.ShapeDtypeStruct((s_total * SUB, D), f32),
                   jax.ShapeDtypeStruct(state.shape, f32)],
        scratch_shapes=[pltpu.VMEM((rows, DV), f32)],
        compiler_params=pltpu.CompilerParams(dimension_semantics=("arbitrary",), vmem_limit_bytes=VMEM_LIMIT),
        name="gla_sample",
    )(x_pad, state, wq, wgl, wgu, bgate, ng, wo, lg, lb)


def _conv_prompt(x, win, wc, wout, lg, lb, tb):
    bsz, t, _ = x.shape
    assert t % (2 * tb) == 0
    n_blocks = bsz * t // tb
    steps_per_seq = t // (2 * tb)
    x2d = x.reshape(bsz * t, D)
    y, st = pl.pallas_call(
        functools.partial(_conv_prompt_kernel, tb=tb, steps_per_seq=steps_per_seq),
        grid=(n_blocks // 2,),
        in_specs=[pl.BlockSpec((tb, D), lambda i: (0, 0), pipeline_mode=pl.Buffered(1)),
                  pl.BlockSpec((tb, D), lambda i: (2 * i + 1, 0)),
                  pl.BlockSpec((tb, D), lambda i: (jnp.minimum(2 * i + 2, n_blocks - 1), 0)),
                  pl.BlockSpec((2 * tb, D), lambda i: (i, 0)),
                  _const_spec(win.shape), _const_spec(wc.shape), _const_spec(wout.shape),
                  _const_spec(lg.shape), _const_spec(lb.shape)],
        out_specs=[pl.BlockSpec((2 * tb, D), lambda i: (i, 0)),
                   pl.BlockSpec((1, 2, D), lambda i: (i // steps_per_seq, 0, 0))],
        out_shape=[jax.ShapeDtypeStruct((bsz * t, D), f32),
                   jax.ShapeDtypeStruct((bsz, 2, D), f32)],
        scratch_shapes=[pltpu.VMEM((2, tb, D), bf16), pltpu.VMEM((2, tb, D), f32),
                        pltpu.VMEM((tb, D), f32), pltpu.VMEM((2, tb, D), f32),
                        pltpu.VMEM((tb, D), bf16), pltpu.VMEM((SUB, D), f32)],
        compiler_params=pltpu.CompilerParams(dimension_semantics=("arbitrary",), vmem_limit_bytes=VMEM_LIMIT),
        name="conv_prompt",
    )(x2d, x2d, x2d, x2d, win, wc, wout, lg, lb)
    return y.reshape(bsz, t, D), st


def _conv_sample(x2d, e1, e2, win, wc, wout, lg, lb, t_seq):
    m = x2d.shape[0]
    n_proj = win.shape[1] // D
    full = lambda shape: pl.BlockSpec(shape, lambda i: (0,) * len(shape))
    return pl.pallas_call(
        functools.partial(_conv_sample_kernel, t_seq=t_seq),
        grid=(n_proj + 1,),
        in_specs=[full((m, D)), full((m, D)), full((m, D)),
                  pl.BlockSpec((D, D), lambda c: (0, jnp.minimum(c, n_proj - 1))),
                  _const_spec(wc.shape), _const_spec(wout.shape),
                  _const_spec(lg.shape), _const_spec(lb.shape)],
        out_specs=[full((m, D)), full((m, D))],
        out_shape=[jax.ShapeDtypeStruct((m, D), f32), jax.ShapeDtypeStruct((m, D), f32)],
        scratch_shapes=[pltpu.VMEM((m, D), bf16), pltpu.VMEM((n_proj, m, D), f32)],
        compiler_params=pltpu.CompilerParams(dimension_semantics=("arbitrary",), vmem_limit_bytes=VMEM_LIMIT),
        name="conv_sample",
    )(x2d, e1, e2, win, wc, wout, lg, lb)


def kernel(x_prompt, x_sample, state_gla, state_conv, gla_w_in, gla_w_gate_up, gla_b_gate, gla_norm_g, gla_w_o, conv_w_in, conv_w_conv, conv_w_out, mlp_w_up, mlp_w_down, ln1_g, ln1_b, ln2_g, ln2_b):
    bsz, t, _ = x_prompt.shape
    n_dec, t_dec, _ = x_sample.shape
    assert t % CHUNK == 0 and t_dec <= SUB and t_dec >= 2

    w_in = gla_w_in[0]
    wq = w_in.astype(bf16)
    wgl = jnp.pad(w_in[:, 2 * DK + 2 * DV:], ((0, 0), (0, RANK_PAD - RANK))).astype(bf16)
    wgu = jnp.pad(gla_w_gate_up[0], ((0, RANK_PAD - RANK), (0, 0))).astype(bf16)
    bgate = gla_b_gate[0].reshape(1, DK)
    ng = gla_norm_g[0].reshape(1, DV)
    wo = gla_w_o[0].astype(bf16)
    cwc = conv_w_conv[0]
    ln2g = ln2_g.reshape(DEPTH, 1, D)
    ln2b = ln2_b.reshape(DEPTH, 1, D)
    row = lambda a, i: a[i].reshape(1, D)
    mlp_prompt = functools.partial(_mlp, n_sub=4, sub_rows=256)
    mlp_sample = functools.partial(_mlp_stream, ff_chunk=1024)

    side = (mlp_w_up.reshape(DEPTH * D, D_FF), mlp_w_down.reshape(DEPTH * D_FF, D), conv_w_in[0], conv_w_out[0])
    xp, gla_p, (wup, wdn, cwin, cwout) = _gla_prompt(x_prompt, wq, wgl, wgu, bgate, ng, wo, row(ln1_g, 0),
                                                     row(ln1_b, 0), tb=256, side=side)
    wup = wup.reshape(DEPTH, D, D_FF)
    wdn = wdn.reshape(DEPTH, D_FF, D)
    xs_pad = jnp.pad(x_sample, ((0, 0), (0, SUB - t_dec), (0, 0))).reshape(n_dec * SUB, D)
    xs_pad, gla_s = _gla_sample(xs_pad, state_gla[0], wq, wgl, wgu, bgate, ng, wo, row(ln1_g, 0), row(ln1_b, 0),
                                n_seq=16, t_valid=t_dec)
    xs = xs_pad.reshape(n_dec, SUB, D)[:, :t_dec].reshape(n_dec * t_dec, D)
    xp = mlp_prompt(xp.reshape(bsz * t, D), wup, wdn, ln2g, ln2b, layer=0)

    xp, conv_p = _conv_prompt(xp.reshape(bsz, t, D), cwin, cwc, cwout, row(ln1_g, 1), row(ln1_b, 1), tb=512)
    buf = state_conv[0]
    e1 = jnp.pad(buf[:, 1:2], ((0, 0), (0, t_dec - 1), (0, 0))).reshape(n_dec * t_dec, D)
    e2 = jnp.pad(buf, ((0, 0), (0, t_dec - 2), (0, 0))).reshape(n_dec * t_dec, D)
    xs, u_s = _sample_tail(xs, e1, e2, wup, wdn, cwin, cwc, cwout, ln2g, ln2b, row(ln1_g, 1), row(ln1_b, 1),
                           t_seq=t_dec, ff_chunk=2048)
    conv_s = u_s.reshape(n_dec, t_dec, D)[:, t_dec - 2:]
    xp = mlp_prompt(xp.reshape(bsz * t, D), wup, wdn, ln2g, ln2b, layer=1)

    return (xp.reshape(bsz, t, D), xs.reshape(n_dec, t_dec, D), gla_p[None], gla_s[None],
            conv_p[None], conv_s[None])
```
